```python
import math
import jax, jax.numpy as jnp
from jax import lax
import numpy as np

D_MODEL = 1024
BATCH = 16
SEQ = 2048
DEPTH = 4
DEC_BATCH = 8
DEC_SEQ = 4096
PAST_LEN = 128

N_MIXERS = 2
N_RET_LAYERS = (DEPTH + 1) // 2
N_NA_LAYERS = DEPTH // 2

RET_HEADS = 4
RET_QK_DIM = 256
RET_V_DIM = 512
RET_CHUNK = 128
RET_QK_WIDTH = RET_HEADS * RET_QK_DIM
RET_V_WIDTH = RET_HEADS * RET_V_DIM
RET_IN_COLS = 2 * RET_QK_WIDTH + 3 * RET_V_WIDTH
ROPE_BASE = 10000.0

NA_HEADS = 16
NA_HEAD_DIM = D_MODEL // NA_HEADS
NA_KH_MAX = 8
NA_KW = 16
GRID_W = 64
NA_QBW = 16
NA_KBW = 32
NA_NCB = GRID_W // NA_QBW

N_EXPERTS = 16
EC_CAPACITY_FACTOR = 2
D_FF_EXPERT = 2 * D_MODEL

EPS = 1e-6
NEG_INF = -1e30

kernel_name = "hybrid_retention_natten_ec_encoder"


def rmsnorm(x, g):
    xf = x.astype(jnp.float32)
    y = xf * lax.rsqrt(jnp.mean(xf * xf, axis=-1, keepdims=True) + EPS)
    return (y * g.astype(jnp.float32)).astype(x.dtype)


def modulate(h, shift, scale):
    return h * (1.0 + scale[:, None, :]) + shift[:, None, :]


def rotary(x):
    T, d = x.shape[1], x.shape[-1]
    inv = 1.0 / (ROPE_BASE ** (jnp.arange(0, d, 2, dtype=jnp.float32) / d))
    ang = jnp.arange(T, dtype=jnp.float32)[:, None] * inv[None, :]
    cos = jnp.cos(ang)[None, :, None, :]
    sin = jnp.sin(ang)[None, :, None, :]
    x1, x2 = x[..., : d // 2], x[..., d // 2:]
    return jnp.concatenate([x1 * cos - x2 * sin, x1 * sin + x2 * cos], axis=-1)


def chunkwise_retention(q, k, v, log_gamma):
    B, T, H, dk = q.shape
    dv = v.shape[-1]
    n_chunks = T // RET_CHUNK

    def to_chunks(a):
        return a.reshape(B, n_chunks, RET_CHUNK, H, a.shape[-1]).transpose(1, 0, 3, 2, 4)

    pos = jnp.arange(RET_CHUNK, dtype=jnp.float32)
    rel = pos[:, None] - pos[None, :]
    lg = log_gamma[:, None, None]
    intra_decay = jnp.where(rel >= 0, jnp.exp(jnp.maximum(rel, 0.0) * lg), 0.0)
    query_decay = jnp.exp((pos + 1.0) * log_gamma[:, None])[..., None]
    key_decay = jnp.exp((RET_CHUNK - 1.0 - pos) * log_gamma[:, None])[..., None]
    chunk_decay = jnp.exp(RET_CHUNK * log_gamma)[:, None, None]

    def step(state, qkv):
        qc, kc, vc = qkv
        scores = jnp.einsum('bhqd,bhkd->bhqk', qc, kc) * intra_decay
        out = (jnp.einsum('bhqk,bhkv->bhqv', scores, vc)
               + jnp.einsum('bhqd,bhdv->bhqv', qc, state) * query_decay)
        state = state * chunk_decay + jnp.einsum('bhkd,bhkv->bhdv', kc * key_decay, vc)
        return state, out

    s0 = jnp.zeros((B, H, dk, dv), jnp.float32)
    _, o = lax.scan(step, s0, (to_chunks(q), to_chunks(k), to_chunks(v)))
    return o.transpose(1, 0, 3, 2, 4).reshape(B, T, H, dv)


def head_groupnorm(o):
    mu = jnp.mean(o, axis=-1, keepdims=True)
    var = jnp.mean(jnp.square(o - mu), axis=-1, keepdims=True)
    y = (o - mu) * lax.rsqrt(var + EPS)
    return y.reshape(o.shape[0], o.shape[1], -1)


def retention_mixer(h, w_in, decay_logit, w_out):
    B, T, _ = h.shape
    proj = jnp.einsum('btd,de->bte', h, w_in).astype(jnp.float32)
    q, k, v, g_f, g_b = jnp.split(
        proj, [RET_QK_WIDTH, 2 * RET_QK_WIDTH, 2 * RET_QK_WIDTH + RET_V_WIDTH,
               2 * RET_QK_WIDTH + 2 * RET_V_WIDTH], axis=-1)
    q = rotary(q.reshape(B, T, RET_HEADS, RET_QK_DIM))
    k = rotary(k.reshape(B, T, RET_HEADS, RET_QK_DIM)) * (RET_QK_DIM ** -0.5)
    v = v.reshape(B, T, RET_HEADS, RET_V_DIM)
    q2 = jnp.concatenate([q, q[:, ::-1]], axis=2)
    k2 = jnp.concatenate([k, k[:, ::-1]], axis=2)
    v2 = jnp.concatenate([v, v[:, ::-1]], axis=2)
    log_gamma = jax.nn.log_sigmoid(decay_logit.astype(jnp.float32).reshape(-1))
    o = chunkwise_retention(q2, k2, v2, log_gamma)
    o_f = o[:, :, :RET_HEADS]
    o_b = o[:, ::-1, RET_HEADS:]
    y = head_groupnorm(o_f) * jax.nn.silu(g_f) + head_groupnorm(o_b) * jax.nn.silu(g_b)
    return jnp.einsum('btv,vd->btd', y.astype(h.dtype), w_out)


def head_rmsnorm(x, g):
    xf = x.astype(jnp.float32)
    y = xf * lax.rsqrt(jnp.mean(xf * xf, axis=-1, keepdims=True) + EPS)
    return y * g.astype(jnp.float32)


def na_column_tables():
    q_cols = np.arange(GRID_W).reshape(NA_NCB, NA_QBW)
    k_start = np.clip(q_cols[:, 0] - NA_KW // 2, 0, GRID_W - NA_KBW)
    k_cols = k_start[:, None] + np.arange(NA_KBW)[None, :]
    w_start = np.clip(q_cols - NA_KW // 2, 0, GRID_W - NA_KW)
    kc = k_cols[:, None, :]
    ws = w_start[:, :, None]
    valid = (kc >= ws) & (kc < ws + NA_KW)
    dc = np.clip(kc - q_cols[:, :, None], -(NA_KW - 1), NA_KW - 1) + NA_KW - 1
    return k_cols, valid, dc


def neighbourhood_attention(h, w_in, q_gain, k_gain, rpb, w_out):
    B, T, _ = h.shape
    rows = T // GRID_W
    kh = min(NA_KH_MAX, rows)
    qkv = jnp.einsum('btd,de->bte', h, w_in)
    q, k, v = jnp.split(qkv, 3, axis=-1)
    q = head_rmsnorm(q.reshape(B, T, NA_HEADS, NA_HEAD_DIM), q_gain) * (NA_HEAD_DIM ** -0.5)
    k = head_rmsnorm(k.reshape(B, T, NA_HEADS, NA_HEAD_DIM), k_gain)
    qg = q.astype(h.dtype).reshape(B, rows, GRID_W, NA_HEADS, NA_HEAD_DIM)
    kg = k.astype(h.dtype).reshape(B, rows, GRID_W, NA_HEADS, NA_HEAD_DIM)
    vg = v.reshape(B, rows, GRID_W, NA_HEADS, NA_HEAD_DIM)
    k_cols, valid, dc = na_column_tables()
    mask = jnp.asarray(valid)[None, None, :, :, None, :]
    row_idx = jnp.arange(kh)
    rpb32 = rpb.astype(jnp.float32)

    def one_row(r):
        rs = jnp.clip(r - kh // 2, 0, rows - kh)
        kb = lax.dynamic_slice_in_dim(kg, rs, kh, axis=1)[:, :, k_cols]
        vb = lax.dynamic_slice_in_dim(vg, rs, kh, axis=1)[:, :, k_cols]
        qb = lax.dynamic_index_in_dim(qg, r, axis=1, keepdims=False).reshape(
            B, NA_NCB, NA_QBW, NA_HEADS, NA_HEAD_DIM)
        s = jnp.einsum('bnqhd,binchd->bhnqic', qb, kb).astype(jnp.float32)
        dr = rs + row_idx - r + NA_KH_MAX - 1
        bias = rpb32[:, dr][:, :, dc].transpose(0, 2, 3, 1, 4)
        s = jnp.where(mask, s + bias[None], NEG_INF)
        sh = s.shape
        p = jax.nn.softmax(s.reshape(sh[:4] + (kh * NA_KBW,)), axis=-1).reshape(sh)
        o = jnp.einsum('bhnqic,binchd->bnqhd', p.astype(vb.dtype), vb)
        return o.reshape(B, GRID_W, NA_HEADS * NA_HEAD_DIM)

    out = lax.map(one_row, jnp.arange(rows))
    out = out.transpose(1, 0, 2, 3).reshape(B, T, NA_HEADS * NA_HEAD_DIM)
    return jnp.einsum('bte,ed->btd', out, w_out)


def expert_choice_ffn(h, router_w, w_gate, w_up, w_down):
    B, T, D = h.shape
    n_tok = B * T
    cap = EC_CAPACITY_FACTOR * n_tok // N_EXPERTS
    tok = h.reshape(n_tok, D)
    aff = jax.nn.softmax(jnp.einsum('nd,de->ne', tok, router_w).astype(jnp.float32), axis=-1)
    gate, idx = lax.top_k(aff.T, cap)
    xe = tok[idx]
    hid = jax.nn.silu(jnp.einsum('ecd,edf->ecf', xe, w_gate)) * jnp.einsum('ecd,edf->ecf', xe, w_up)
    ye = jnp.einsum('ecf,efd->ecd', hid, w_down) * gate[..., None].astype(h.dtype)
    out = jnp.zeros_like(tok).at[idx.reshape(-1)].add(ye.reshape(-1, D))
    return out.reshape(B, T, D)


def run_trunk(x, c, norm_mix_g, norm_ffn_g, ada_w, ada_b, ret_w_in, ret_decay_logit, ret_w_out,
              na_w_in, na_q_gain, na_k_gain, na_rpb, na_w_out,
              moe_router, moe_w_gate, moe_w_up, moe_w_down):
    c_act = jax.nn.silu(c)
    for i in range(DEPTH):
        mod = jnp.einsum('bd,de->be', c_act, ada_w[i]) + ada_b[i]
        sh1, sc1, g1, sh2, sc2, g2 = jnp.split(mod, 6, axis=-1)
        h = modulate(rmsnorm(x, norm_mix_g[i]), sh1, sc1)
        j = i // N_MIXERS
        if i % N_MIXERS == 0:
            m = retention_mixer(h, ret_w_in[j], ret_decay_logit[j], ret_w_out[j])
        else:
            m = neighbourhood_attention(h, na_w_in[j], na_q_gain[j], na_k_gain[j], na_rpb[j], na_w_out[j])
        x = x + g1[:, None, :] * m
        h = modulate(rmsnorm(x, norm_ffn_g[i]), sh2, sc2)
        x = x + g2[:, None, :] * expert_choice_ffn(h, moe_router[i], moe_w_gate[i], moe_w_up[i], moe_w_down[i])
    return x


def setup_inputs(seed: int = 0) -> dict:
    key = jax.random.key(seed)
    ks = jax.random.split(key, 20)
    f32 = jnp.float32
    D = D_MODEL
    nrm = lambda k, shape, scale: jax.random.normal(k, shape, f32) * scale
    gam = 1.0 - 2.0 ** (-5.0 - np.arange(RET_HEADS, dtype=np.float64))
    base_logit = jnp.asarray(np.log(gam / (1.0 - gam)), dtype=f32)
    return {
        "x_prompt": nrm(ks[0], (BATCH, SEQ, D), 1.0),
        "x_sample": nrm(ks[1], (DEC_BATCH, DEC_SEQ, D), 1.0),
        "c_prompt": nrm(ks[2], (BATCH, D), 1.0),
        "c_sample": nrm(ks[3], (DEC_BATCH, D), 1.0),
        "norm_mix_g": 1.0 + nrm(ks[4], (DEPTH, D), 0.02),
        "norm_ffn_g": 1.0 + nrm(ks[5], (DEPTH, D), 0.02),
        "ada_w": nrm(ks[6], (DEPTH, D, 6 * D), 0.5 * D ** -0.5),
        "ada_b": nrm(ks[7], (DEPTH, 6 * D), 0.02),
        "ret_w_in": nrm(ks[8], (N_RET_LAYERS, D, RET_IN_COLS), D ** -0.5),
        "ret_decay_logit": base_logit[None, None, :] + nrm(ks[9], (N_RET_LAYERS, 2, RET_HEADS), 0.1),
        "ret_w_out": nrm(ks[10], (N_RET_LAYERS, RET_V_WIDTH, D), RET_V_WIDTH ** -0.5),
        "na_w_in": nrm(ks[11], (N_NA_LAYERS, D, 3 * D), D ** -0.5),
        "na_q_gain": 1.0 + nrm(ks[12], (N_NA_LAYERS, NA_HEAD_DIM), 0.02),
        "na_k_gain": 1.0 + nrm(ks[13], (N_NA_LAYERS, NA_HEAD_DIM), 0.02),
        "na_rpb": nrm(ks[14], (N_NA_LAYERS, NA_HEADS, 2 * NA_KH_MAX - 1, 2 * NA_KW - 1), 0.1),
        "na_w_out": nrm(ks[15], (N_NA_LAYERS, D, D), D ** -0.5),
        "moe_router": nrm(ks[16], (DEPTH, D, N_EXPERTS), D ** -0.5),
        "moe_w_gate": nrm(ks[17], (DEPTH, N_EXPERTS, D, D_FF_EXPERT), D ** -0.5),
        "moe_w_up": nrm(ks[18], (DEPTH, N_EXPERTS, D, D_FF_EXPERT), D ** -0.5),
        "moe_w_down": nrm(ks[19], (DEPTH, N_EXPERTS, D_FF_EXPERT, D), D_FF_EXPERT ** -0.5),
    }


def reference(x_prompt, x_sample, c_prompt, c_sample, norm_mix_g, norm_ffn_g, ada_w, ada_b,
              ret_w_in, ret_decay_logit, ret_w_out, na_w_in, na_q_gain, na_k_gain, na_rpb, na_w_out,
              moe_router, moe_w_gate, moe_w_up, moe_w_down):
    y_prompt = run_trunk(x_prompt, c_prompt, norm_mix_g, norm_ffn_g, ada_w, ada_b, ret_w_in,
                         ret_decay_logit, ret_w_out, na_w_in, na_q_gain, na_k_gain, na_rpb, na_w_out,
                         moe_router, moe_w_gate, moe_w_up, moe_w_down)
    y_sample = run_trunk(x_sample, c_sample, norm_mix_g, norm_ffn_g, ada_w, ada_b, ret_w_in,
                         ret_decay_logit, ret_w_out, na_w_in, na_q_gain, na_k_gain, na_rpb, na_w_out,
                         moe_router, moe_w_gate, moe_w_up, moe_w_down)
    return (y_prompt, y_sample)
```

```python
import functools
import math

import numpy as np
import jax
import jax.numpy as jnp
from jax import lax
from jax.experimental import pallas as pl
from jax.experimental.pallas import tpu as pltpu

F32 = jnp.float32
BF16 = jnp.bfloat16
I32 = jnp.int32

D_MODEL = 1024
DEPTH = 4

RET_HEADS = 4
RET_QK_DIM = 256
RET_V_DIM = 512
RET_QK_WIDTH = RET_HEADS * RET_QK_DIM
RET_V_WIDTH = RET_HEADS * RET_V_DIM
RET_IN_COLS = 2 * RET_QK_WIDTH + 3 * RET_V_WIDTH
ROPE_BASE = 10000.0
RET_BLOCK = 256

NA_HEADS = 16
NA_HEAD_DIM = 64
NA_KH = 8
NA_KW = 16
GRID_W = 64
NA_ROWS_PER_GROUP = 4
NA_WIN_ROWS = NA_ROWS_PER_GROUP + NA_KH - 1
NA_Q = NA_ROWS_PER_GROUP * GRID_W
NA_K = NA_WIN_ROWS * GRID_W

N_EXPERTS = 16
EC_CAPACITY_FACTOR = 2
D_FF = 2 * D_MODEL

EPS = 1e-6
NEG_INF = -1e30

LANES = 128
BF16_SUBLANES = 16
VMEM_LIMIT = 56 * 1024 * 1024

HIGHEST = lax.Precision.HIGHEST


def _cparams(sem):
    return pltpu.CompilerParams(dimension_semantics=sem, vmem_limit_bytes=VMEM_LIMIT)


def _tile(n, pref):
    t = min(pref, n)
    while n % t:
        t //= 2
    return t


def _silu(x):
    return x * (1.0 / (1.0 + jnp.exp(-x)))


def _ones_where(mask):
    return jnp.where(mask, 1.0, 0.0).astype(BF16)


def _dot(a, b):
    return jnp.dot(a, b, preferred_element_type=F32)


def _dot_nt(a, b):
    return lax.dot_general(a, b, (((1,), (1,)), ((), ())), preferred_element_type=F32)


def _dot_tn(a, b):
    return lax.dot_general(a, b, (((0,), (0,)), ((), ())), preferred_element_type=F32)


def _ada_kernel(c_ref, w_ref, b_ref, o_ref):
    ca = _silu(c_ref[...])
    o_ref[...] = jnp.dot(ca, w_ref[...], preferred_element_type=F32, precision=HIGHEST) + b_ref[...]


def _ada_mod(c_all, ada_w, ada_b):
    R, D = c_all.shape
    n_out = ada_w.shape[-1]
    tn = min(1536, n_out)
    return pl.pallas_call(
        _ada_kernel,
        grid=(DEPTH, n_out // tn),
        in_specs=[
            pl.BlockSpec((R, D), lambda l, j: (0, 0)),
            pl.BlockSpec((None, D, tn), lambda l, j: (l, 0, j)),
            pl.BlockSpec((None, 1, tn), lambda l, j: (l, 0, j)),
        ],
        out_specs=pl.BlockSpec((None, R, tn), lambda l, j: (l, 0, j)),
        out_shape=jax.ShapeDtypeStruct((DEPTH, R, n_out), F32),
        compiler_params=_cparams(("arbitrary", "arbitrary")),
        name="ada_mod",
    )(c_all, ada_w, ada_b.reshape(DEPTH, 1, n_out))


def _norm_mod(x, g, sh, sc):
    ms = jnp.mean(x * x, axis=-1, keepdims=True)
    y = x * lax.rsqrt(ms + EPS) * g
    return y * (1.0 + sc) + sh


def _proj_kernel(x_ref, g_ref, sh_ref, sc_ref, w_ref, o_ref, h_scr):
    @pl.when(pl.program_id(2) == 0)
    def _():
        h_scr[...] = _norm_mod(x_ref[...], g_ref[...], sh_ref[...], sc_ref[...]).astype(BF16)

    o_ref[...] = _dot(h_scr[...], w_ref[...]).astype(o_ref.dtype)


def _proj_qknorm_kernel(x_ref, g_ref, sh_ref, sc_ref, w_ref, bd_ref, gain_ref, o_ref, h_scr, *, qk_blocks, tn):
    j = pl.program_id(2)

    @pl.when(j == 0)
    def _():
        h_scr[...] = _norm_mod(x_ref[...], g_ref[...], sh_ref[...], sc_ref[...]).astype(BF16)

    acc = _dot(h_scr[...], w_ref[...])

    @pl.when(j < qk_blocks)
    def _():
        for c in range(tn // 256):
            a = acc[:, c * 256:(c + 1) * 256]
            ss = _dot((a * a).astype(BF16), bd_ref[...])
            r = lax.rsqrt(ss * (1.0 / NA_HEAD_DIM) + EPS)
            o_ref[:, c * 256:(c + 1) * 256] = (a * r * gain_ref[:, c * 256:(c + 1) * 256]).astype(o_ref.dtype)

    @pl.when(j >= qk_blocks)
    def _():
        o_ref[...] = acc.astype(o_ref.dtype)


def _proj(x, g, sh, sc, w, qk=None):
    B, T, D = x.shape
    n_out = w.shape[1]
    tm = _tile(T, 1024)
    tn = _tile(n_out, 1024)
    grid = (B, T // tm, n_out // tn)
    in_specs = [
        pl.BlockSpec((None, tm, D), lambda b, i, j: (b, i, 0)),
        pl.BlockSpec((1, D), lambda b, i, j: (0, 0)),
        pl.BlockSpec((None, 1, D), lambda b, i, j: (b, 0, 0)),
        pl.BlockSpec((None, 1, D), lambda b, i, j: (b, 0, 0)),
        pl.BlockSpec((D, tn), lambda b, i, j: (0, j)),
    ]
    args = [x, g, sh, sc, w]
    if qk is None:
        body = _proj_kernel
        name = "proj"
    else:
        bd, gain, n_qk = qk
        in_specs += [
            pl.BlockSpec((256, 256), lambda b, i, j: (0, 0)),
            pl.BlockSpec((1, tn), lambda b, i, j: (0, j)),
        ]
        args += [bd, gain]
        body = functools.partial(_proj_qknorm_kernel, qk_blocks=n_qk // tn, tn=tn)
        name = "proj_qknorm"
    return pl.pallas_call(
        body,
        grid=grid,
        in_specs=in_specs,
        out_specs=pl.BlockSpec((None, tm, tn), lambda b, i, j: (b, i, j)),
        out_shape=jax.ShapeDtypeStruct((B, T, n_out), BF16),
        scratch_shapes=[pltpu.VMEM((tm, D), BF16)],
        compiler_params=_cparams(("parallel", "parallel", "arbitrary")),
        name=name,
    )(*args)


def _outproj_kernel(y_ref, w_ref, x_ref, g_ref, o_ref):
    o_ref[...] = x_ref[...] + g_ref[...] * _dot(y_ref[...], w_ref[...])


def _out_proj(y, w, x, gate):
    B, T, K = y.shape
    D = w.shape[1]
    tm = _tile(T, 512)
    return pl.pallas_call(
        _outproj_kernel,
        grid=(B, T // tm),
        in_specs=[
            pl.BlockSpec((None, tm, K), lambda b, i: (b, i, 0)),
            pl.BlockSpec((K, D), lambda b, i: (0, 0)),
            pl.BlockSpec((None, tm, D), lambda b, i: (b, i, 0)),
            pl.BlockSpec((None, 1, D), lambda b, i: (b, 0, 0)),
        ],
        out_specs=pl.BlockSpec((None, tm, D), lambda b, i: (b, i, 0)),
        out_shape=jax.ShapeDtypeStruct((B, T, D), F32),
        compiler_params=_cparams(("parallel", "parallel")),
        name="out_proj",
    )(y, w, x, gate)


def _retention_kernel(*refs, has_prev):
    if has_prev:
        (q_ref, k_ref, v_ref, gate_ref, cos_ref, sin_ref, intra_ref, qdec_ref, kdec_ref, cdec_ref,
         prev_ref, o_ref, state) = refs
    else:
        (q_ref, k_ref, v_ref, gate_ref, cos_ref, sin_ref, intra_ref, qdec_ref, kdec_ref, cdec_ref,
         o_ref, state) = refs
        prev_ref = None

    @pl.when(pl.program_id(2) == 0)
    def _():
        state[...] = jnp.zeros_like(state)

    cos = cos_ref[...]
    sin = sin_ref[...]
    half = RET_QK_DIM // 2

    def rot(a):
        a1 = a[:, :half]
        a2 = a[:, half:]
        return jnp.concatenate([a1 * cos - a2 * sin, a1 * sin + a2 * cos], axis=-1)

    qr = rot(q_ref[...].astype(F32))
    kr = rot(k_ref[...].astype(F32)) * (RET_QK_DIM ** -0.5)
    v = v_ref[...]
    qb = qr.astype(BF16)
    scores = _dot_nt(qb, kr.astype(BF16)) * intra_ref[...]
    st = state[...]
    o = _dot(scores.astype(BF16), v) + _dot(qb, st.astype(BF16)) * qdec_ref[...]
    state[...] = st * cdec_ref[...] + _dot_tn((kr * kdec_ref[...]).astype(BF16), v)

    mu = jnp.mean(o, axis=-1, keepdims=True)
    oc = o - mu
    var = jnp.mean(oc * oc, axis=-1, keepdims=True)
    y = oc * lax.rsqrt(var + EPS) * _silu(gate_ref[...].astype(F32))
    if has_prev:
        y = y + prev_ref[...].astype(F32)
    o_ref[...] = y.astype(o_ref.dtype)


def _retention_dir(proj, tabs, cos, sin, direction, prev):
    B, T, _ = proj.shape
    rc = min(RET_BLOCK, T)
    nc = T // rc
    intra, qdec, kdec, cdec = tabs
    if direction == 0:
        cidx = lambda c: c
    else:
        cidx = lambda c: nc - 1 - c
    gate_base = (2 * RET_QK_WIDTH + (1 + direction) * RET_V_WIDTH) // RET_V_DIM
    hoff = direction * RET_HEADS
    in_specs = [
        pl.BlockSpec((None, rc, RET_QK_DIM), lambda b, h, c: (b, cidx(c), h)),
        pl.BlockSpec((None, rc, RET_QK_DIM), lambda b, h, c: (b, cidx(c), RET_HEADS + h)),
        pl.BlockSpec((None, rc, RET_V_DIM), lambda b, h, c: (b, cidx(c), 2 * RET_QK_WIDTH // RET_V_DIM + h)),
        pl.BlockSpec((None, rc, RET_V_DIM), lambda b, h, c: (b, cidx(c), gate_base + h)),
        pl.BlockSpec((rc, RET_QK_DIM // 2), lambda b, h, c: (cidx(c), 0)),
        pl.BlockSpec((rc, RET_QK_DIM // 2), lambda b, h, c: (cidx(c), 0)),
        pl.BlockSpec((None, rc, rc), lambda b, h, c: (hoff + h, 0, 0)),
        pl.BlockSpec((None, rc, 1), lambda b, h, c: (hoff + h, 0, 0)),
        pl.BlockSpec((None, rc, 1), lambda b, h, c: (hoff + h, 0, 0)),
        pl.BlockSpec((None, 1, 1), lambda b, h, c: (hoff + h, 0, 0)),
    ]
    args = [proj, proj, proj, proj, cos, sin, intra, qdec, kdec, cdec]
    if prev is not None:
        in_specs.append(pl.BlockSpec((None, rc, RET_V_DIM), lambda b, h, c: (b, cidx(c), h)))
        args.append(prev)
    return pl.pallas_call(
        functools.partial(_retention_kernel, has_prev=prev is not None),
        grid=(B, RET_HEADS, nc),
        in_specs=in_specs,
        out_specs=pl.BlockSpec((None, rc, RET_V_DIM), lambda b, h, c: (b, cidx(c), h)),
        out_shape=jax.ShapeDtypeStruct((B, T, RET_V_WIDTH), BF16),
        scratch_shapes=[pltpu.VMEM((RET_QK_DIM, RET_V_DIM), F32)],
        compiler_params=_cparams(("parallel", "parallel", "arbitrary")),
        name="retention_bwd" if direction else "retention_fwd",
    )(*args)


def _retention_tables(decay_logit, rc):
    lg = jax.nn.log_sigmoid(decay_logit.astype(F32).reshape(-1))
    pos = jnp.arange(rc, dtype=F32)
    rel = pos[:, None] - pos[None, :]
    l3 = lg[:, None, None]
    fwd = jnp.where(rel >= 0, jnp.exp(jnp.maximum(rel, 0.0) * l3), 0.0)
    bwd = jnp.where(rel <= 0, jnp.exp(jnp.maximum(-rel, 0.0) * l3), 0.0)
    is_bwd = (jnp.arange(2 * RET_HEADS) >= RET_HEADS)
    intra = jnp.where(is_bwd[:, None, None], bwd, fwd)
    q_pow = jnp.where(is_bwd[:, None], rc - pos[None, :], pos[None, :] + 1.0)
    k_pow = jnp.where(is_bwd[:, None], pos[None, :], rc - 1.0 - pos[None, :])
    qdec = jnp.exp(q_pow * lg[:, None])[..., None]
    kdec = jnp.exp(k_pow * lg[:, None])[..., None]
    cdec = jnp.exp(rc * lg)[:, None, None]
    return intra, qdec, kdec, cdec


def _rotary_tables(T):
    d = RET_QK_DIM
    inv = 1.0 / (ROPE_BASE ** (jnp.arange(0, d, 2, dtype=F32) / d))
    ang = jnp.arange(T, dtype=F32)[:, None] * inv[None, :]
    return jnp.cos(ang), jnp.sin(ang)


def _na_kernel(q_ref, k_ref, v_ref, bias_ref, o_ref, *, rows):
    n_groups = rows // NA_ROWS_PER_GROUP
    lane = lax.broadcasted_iota(I32, (NA_Q, LANES), 1)
    first = lane < NA_HEAD_DIM

    def group(gi, carry):
        ustart = jnp.clip(gi * NA_ROWS_PER_GROUP - NA_KH // 2, 0, rows - NA_WIN_ROWS)
        pat = jnp.where(gi == 0, 0, jnp.where(gi == n_groups - 1, 2, 1))
        kstart = pl.multiple_of(ustart * GRID_W, GRID_W)
        qstart = pl.multiple_of(gi * NA_Q, NA_Q)
        kw = k_ref[pl.ds(kstart, NA_K), :]
        vw = v_ref[pl.ds(kstart, NA_K), :]
        q = q_ref[pl.ds(qstart, NA_Q), :].astype(F32)
        outs = []
        for hh in range(2):
            qm = jnp.where(first if hh == 0 else jnp.logical_not(first), q, 0.0).astype(BF16)
            s = _dot_nt(qm, kw) + bias_ref[pat, hh]
            m = jnp.max(s, axis=-1, keepdims=True)
            e = jnp.exp(s - m)
            l = jnp.sum(e, axis=-1, keepdims=True)
            outs.append(_dot(e.astype(BF16), vw) / l)
        o_ref[pl.ds(qstart, NA_Q), :] = jnp.where(first, outs[0], outs[1]).astype(o_ref.dtype)
        return carry

    lax.fori_loop(0, n_groups, group, 0)


def _na_attention(qkv, bias):
    B, T, _ = qkv.shape
    rows = T // GRID_W
    assert rows % NA_ROWS_PER_GROUP == 0 and rows >= NA_WIN_ROWS + 1
    n_pairs = D_MODEL // LANES
    blk = lambda off: pl.BlockSpec((None, T, LANES), lambda p, b: (b, 0, off + p))
    return pl.pallas_call(
        functools.partial(_na_kernel, rows=rows),
        grid=(n_pairs, B),
        in_specs=[
            blk(0), blk(n_pairs), blk(2 * n_pairs),
            pl.BlockSpec((3, 2, NA_Q, NA_K), lambda p, b: (0, p, 0, 0)),
        ],
        out_specs=pl.BlockSpec((None, T, LANES), lambda p, b: (b, 0, p)),
        out_shape=jax.ShapeDtypeStruct((B, T, D_MODEL), BF16),
        compiler_params=_cparams(("parallel", "parallel")),
        name="na_attention",
    )(qkv, qkv, qkv, bias)


def _na_bias_table(rpb):
    i = np.arange(NA_ROWS_PER_GROUP)[:, None, None, None]
    qc = np.arange(GRID_W)[None, :, None, None]
    u = np.arange(NA_WIN_ROWS)[None, None, :, None]
    kc = np.arange(GRID_W)[None, None, None, :]
    ws = np.clip(qc - NA_KW // 2, 0, GRID_W - NA_KW)
    col_ok = (kc >= ws) & (kc < ws + NA_KW)
    dc = np.clip(kc - qc, -(NA_KW - 1), NA_KW - 1) + NA_KW - 1
    lo = [0 * i, i, 0 * i + NA_ROWS_PER_GROUP - 1]
    qrow = [i, i + NA_KH // 2, i + NA_KH - 1]
    valid, dr = [], []
    shape = (NA_ROWS_PER_GROUP, GRID_W, NA_WIN_ROWS, GRID_W)
    for p in range(3):
        ok = (u >= lo[p]) & (u < lo[p] + NA_KH) & col_ok
        valid.append(np.broadcast_to(ok, shape).reshape(NA_Q, NA_K))
        d = np.clip(u - qrow[p] + NA_KH - 1, 0, 2 * NA_KH - 2)
        dr.append(np.broadcast_to(d, shape).reshape(NA_Q, NA_K))
    dcf = np.broadcast_to(dc, shape).reshape(NA_Q, NA_K)
    valid = np.stack(valid)
    dr = np.stack(dr)
    vals = rpb.astype(F32)[:, dr, dcf[None]]
    vals = jnp.where(jnp.asarray(valid)[None], vals, NEG_INF)
    return vals.transpose(1, 0, 2, 3)


def _ffn_pre_kernel(x_ref, g_ref, sh_ref, sc_ref, r_ref, h_ref, aff_ref):
    h = _norm_mod(x_ref[...], g_ref[...], sh_ref[...], sc_ref[...])
    h_ref[...] = h.astype(BF16)
    logits = jnp.dot(h, r_ref[...], preferred_element_type=F32, precision=HIGHEST)
    m = jnp.max(logits, axis=-1, keepdims=True)
    e = jnp.exp(logits - m)
    aff_ref[...] = e / jnp.sum(e, axis=-1, keepdims=True)


def _ffn_pre(x, g, sh, sc, router):
    B, T, D = x.shape
    tm = _tile(T, 1024)
    nt = T // tm
    E = router.shape[1]
    return pl.pallas_call(
        _ffn_pre_kernel,
        grid=(B, nt),
        in_specs=[
            pl.BlockSpec((None, tm, D), lambda b, i: (b, i, 0)),
            pl.BlockSpec((1, D), lambda b, i: (0, 0)),
            pl.BlockSpec((None, 1, D), lambda b, i: (b, 0, 0)),
            pl.BlockSpec((None, 1, D), lambda b, i: (b, 0, 0)),
            pl.BlockSpec((D, E), lambda b, i: (0, 0)),
        ],
        out_specs=[
            pl.BlockSpec((tm, D), lambda b, i: (b * nt + i, 0)),
            pl.BlockSpec((tm, E), lambda b, i: (b * nt + i, 0)),
        ],
        out_shape=[jax.ShapeDtypeStruct((B * T, D), BF16), jax.ShapeDtypeStruct((B * T, E), F32)],
        compiler_params=_cparams(("parallel", "parallel")),
        name="ffn_pre",
    )(x, g, sh, sc, router)


def _select_kernel(aff_ref, pos_ref, pre_ref, *, cap, n_tok, rb):
    E = N_EXPERTS
    per_row = LANES // E
    rows = n_tok // per_row
    bits = pltpu.bitcast(aff_ref[...], I32)
    lane = lax.broadcasted_iota(I32, (rows, LANES), 1)
    tok = lax.broadcasted_iota(I32, (rows, LANES), 0) * per_row + lane // E

    def count(pred):
        c = jnp.sum(pred.astype(I32), axis=0, keepdims=True)
        c = jnp.broadcast_to(c, (8, LANES))
        sh = E
        while sh < LANES:
            c = c + pltpu.roll(c, sh, 1)
            sh *= 2
        return c[0:1]

    def thr_step(i, thr):
        cand = thr | jnp.left_shift(jnp.int32(1), 30 - i)
        return jnp.where(count(bits >= cand) >= cap, cand, thr)

    thr = lax.fori_loop(0, 31, thr_step, jnp.zeros((1, LANES), I32))
    gt = bits > thr
    eq = bits == thr
    need = cap - count(gt)

    nbits = max(1, int(math.ceil(math.log2(n_tok))))

    def tie_step(i, jmax):
        cand = jmax + jnp.left_shift(jnp.int32(1), nbits - 1 - i)
        return jnp.where(count(jnp.logical_and(eq, tok < cand)) < need, cand, jmax)

    jmax = lax.fori_loop(0, nbits, tie_step, jnp.zeros((1, LANES), I32))

    li = lax.broadcasted_iota(I32, (LANES, LANES), 0)
    lj = lax.broadcasted_iota(I32, (LANES, LANES), 1)
    same_e = (li % E) == (lj % E)
    a_before = _ones_where(jnp.logical_and(same_e, li // E < lj // E))
    a_all = _ones_where(same_e)
    ri = lax.broadcasted_iota(I32, (rb, rb), 0)
    rj = lax.broadcasted_iota(I32, (rb, rb), 1)
    lower = _ones_where(rj < ri)

    running = jnp.zeros((1, LANES), F32)
    for blk in range(rows // rb):
        sl = slice(blk * rb, (blk + 1) * rb)
        b_blk = pltpu.bitcast(aff_ref[sl, :], I32)
        t_blk = ((lax.broadcasted_iota(I32, (rb, LANES), 0) + blk * rb) * per_row
                 + lax.broadcasted_iota(I32, (rb, LANES), 1) // E)
        keep = jnp.logical_or(b_blk > thr, jnp.logical_and(b_blk == thr, t_blk <= jmax))
        kb = _ones_where(keep)
        row_tot = _dot(kb, a_all)
        within = _dot(lower, row_tot.astype(BF16))
        pre = running + within + _dot(kb, a_before)
        running = running + jnp.sum(row_tot, axis=0, keepdims=True)
        pre_i = pre.astype(I32)
        pre_ref[sl, :] = pre_i
        pos_ref[sl, :] = jnp.where(keep, pre_i, -1)


def _select(aff, cap):
    n_tok, E = aff.shape
    rows = n_tok * E // LANES
    rb = min(256, rows)
    aff8 = aff.reshape(rows, LANES)
    pos, pre = pl.pallas_call(
        functools.partial(_select_kernel, cap=cap, n_tok=n_tok, rb=rb),
        out_shape=[jax.ShapeDtypeStruct((rows, LANES), I32)] * 2,
        compiler_params=pltpu.CompilerParams(vmem_limit_bytes=VMEM_LIMIT),
        name="expert_select",
    )(aff8)
    return pos.reshape(n_tok, E), pre.reshape(n_tok, E)


def _dispatch_kernel(ws_ref, pos_ref, aff_ref, h_ref, xe_ref, gs_ref, *, n_tiles, win):
    e = pl.program_id(0)
    t = pl.program_id(1)

    @pl.when(t == 0)
    def _():
        xe_ref[...] = jnp.zeros_like(xe_ref)
        gs_ref[...] = jnp.zeros_like(gs_ref)

    ws = pl.multiple_of(ws_ref[e * n_tiles + t], BF16_SUBLANES)
    tn = pos_ref.shape[-1]
    hit = lax.broadcasted_iota(I32, (win, tn), 0) == (pos_ref[...] - ws)
    rows = _dot(_ones_where(hit), h_ref[...])
    xe_ref[pl.ds(ws, win), :] = (xe_ref[pl.ds(ws, win), :].astype(F32) + rows).astype(xe_ref.dtype)
    g = jnp.sum(jnp.where(hit, aff_ref[...], 0.0), axis=1, keepdims=True)
    gs_ref[pl.ds(ws, win), :] = gs_ref[pl.ds(ws, win), :] + g


def _dispatch(wstart, pos_t, aff_t, h, cap, tn):
    E = pos_t.shape[0]
    N, D = h.shape
    n_tiles = N // tn
    win = tn + BF16_SUBLANES
    grid_spec = pltpu.PrefetchScalarGridSpec(
        num_scalar_prefetch=1,
        grid=(E, n_tiles),
        in_specs=[
            pl.BlockSpec((None, 1, tn), lambda e, t, ws: (e, 0, t)),
            pl.BlockSpec((None, 1, tn), lambda e, t, ws: (e, 0, t)),
            pl.BlockSpec((tn, D), lambda e, t, ws: (t, 0)),
        ],
        out_specs=[
            pl.BlockSpec((None, cap, D), lambda e, t, ws: (e, 0, 0)),
            pl.BlockSpec((None, cap, 1), lambda e, t, ws: (e, 0, 0)),
        ],
    )
    return pl.pallas_call(
        functools.partial(_dispatch_kernel, n_tiles=n_tiles, win=win),
        grid_spec=grid_spec,
        out_shape=[jax.ShapeDtypeStruct((E, cap, D), BF16), jax.ShapeDtypeStruct((E, cap, 1), F32)],
        compiler_params=_cparams(("parallel", "arbitrary")),
        name="moe_dispatch",
    )(wstart, pos_t, aff_t, h)


def _ffn_kernel(x_ref, gs_ref, wg_ref, wu_ref, wd_ref, o_ref, acc, *, n_f):
    f = pl.program_id(2)

    @pl.when(f == 0)
    def _():
        acc[...] = jnp.zeros_like(acc)

    x = x_ref[...]
    hid = _silu(_dot(x, wg_ref[...])) * _dot(x, wu_ref[...])
    acc[...] += _dot(hid.astype(BF16), wd_ref[...])

    @pl.when(f == n_f - 1)
    def _():
        o_ref[...] = (acc[...] * gs_ref[...]).astype(o_ref.dtype)


def _expert_ffn(xe, gs, wg, wu, wd):
    E, C, D = xe.shape
    F = wg.shape[-1]
    tm = _tile(C, 1024)
    tf = _tile(F, 512)
    n_f = F // tf
    return pl.pallas_call(
        functools.partial(_ffn_kernel, n_f=n_f),
        grid=(E, C // tm, n_f),
        in_specs=[
            pl.BlockSpec((None, tm, D), lambda e, m, f: (e, m, 0)),
            pl.BlockSpec((None, tm, 1), lambda e, m, f: (e, m, 0)),
            pl.BlockSpec((None, D, tf), lambda e, m, f: (e, 0, f)),
            pl.BlockSpec((None, D, tf), lambda e, m, f: (e, 0, f)),
            pl.BlockSpec((None, tf, D), lambda e, m, f: (e, f, 0)),
        ],
        out_specs=pl.BlockSpec((None, tm, D), lambda e, m, f: (e, m, 0)),
        out_shape=jax.ShapeDtypeStruct((E, C, D), BF16),
        scratch_shapes=[pltpu.VMEM((tm, D), F32)],
        compiler_params=_cparams(("parallel", "parallel", "arbitrary")),
        name="expert_ffn",
    )(xe, gs, wg, wu, wd)


def _combine_kernel(ws_ref, pos_ref, ye_ref, x_ref, g_ref, o_ref, acc, *, n_tiles, win, n_exp):
    t = pl.program_id(0)
    e = pl.program_id(1)

    @pl.when(e == 0)
    def _():
        acc[...] = jnp.zeros_like(acc)

    ws = ws_ref[e * n_tiles + t]
    tn = pos_ref.shape[-1]
    hit = lax.broadcasted_iota(I32, (win, tn), 0) == (pos_ref[...] - ws)
    acc[...] += _dot_tn(_ones_where(hit), ye_ref[0])

    @pl.when(e == n_exp - 1)
    def _():
        o_ref[...] = x_ref[...] + g_ref[...] * acc[...]


def _combine(wstart, pos_t, ye, x, gate, tn):
    E, C, D = ye.shape
    B, T, _ = x.shape
    N = B * T
    n_tiles = N // tn
    tpb = T // tn
    win = tn + BF16_SUBLANES
    grid_spec = pltpu.PrefetchScalarGridSpec(
        num_scalar_prefetch=1,
        grid=(n_tiles, E),
        in_specs=[
            pl.BlockSpec((None, 1, tn), lambda t, e, ws: (e, 0, t)),
            pl.BlockSpec((pl.Element(1), pl.Element(win), pl.Element(D)),
                         lambda t, e, ws: (e, pl.multiple_of(ws[e * n_tiles + t], BF16_SUBLANES), 0)),
            pl.BlockSpec((None, tn, D), lambda t, e, ws: (t // tpb, t % tpb, 0)),
            pl.BlockSpec((None, 1, D), lambda t, e, ws: (t // tpb, 0, 0)),
        ],
        out_specs=pl.BlockSpec((None, tn, D), lambda t, e, ws: (t // tpb, t % tpb, 0)),
        scratch_shapes=[pltpu.VMEM((tn, D), F32)],
    )
    return pl.pallas_call(
        functools.partial(_combine_kernel, n_tiles=n_tiles, win=win, n_exp=E),
        grid_spec=grid_spec,
        out_shape=jax.ShapeDtypeStruct((B, T, D), F32),
        compiler_params=_cparams(("parallel", "arbitrary")),
        name="moe_combine",
    )(wstart, pos_t, ye, x, gate)


def _moe_tile(n_tok, cap):
    tn = 256
    while tn + BF16_SUBLANES > cap or n_tok % tn:
        tn //= 2
    return tn


def _moe_layer(x, g, sh, sc, gate, router, wg, wu, wd):
    B, T, D = x.shape
    N = B * T
    E = router.shape[1]
    cap = EC_CAPACITY_FACTOR * N // E
    tn = _moe_tile(T, cap)
    win = tn + BF16_SUBLANES
    h, aff = _ffn_pre(x, g, sh, sc, router)
    pos, pre = _select(aff, cap)
    start = pre[::tn].T
    wstart = jnp.minimum(start // BF16_SUBLANES * BF16_SUBLANES, cap - win).reshape(-1).astype(I32)
    pos_t = pos.T.reshape(E, 1, N)
    aff_t = aff.T.reshape(E, 1, N)
    xe, gs = _dispatch(wstart, pos_t, aff_t, h, cap, tn)
    ye = _expert_ffn(xe, gs, wg, wu, wd)
    return _combine(wstart, pos_t, ye, x, gate, tn)


def _trunk(x, mod, p):
    B, T, D = x.shape
    rc = min(RET_BLOCK, T)
    cos, sin = _rotary_tables(T)
    for i in range(DEPTH):
        sh1, sc1, g1, sh2, sc2, g2 = [mod[i, :, k][:, None, :] for k in range(6)]
        j = i // 2
        if i % 2 == 0:
            proj = _proj(x, p["norm_mix_g"][i][None], sh1, sc1, p["ret_w_in"][j])
            tabs = _retention_tables(p["ret_decay_logit"][j], rc)
            y_f = _retention_dir(proj, tabs, cos, sin, 0, None)
            y = _retention_dir(proj, tabs, cos, sin, 1, y_f)
            x = _out_proj(y, p["ret_w_out"][j], x, g1)
        else:
            qkv = _proj(x, p["norm_mix_g"][i][None], sh1, sc1, p["na_w_in"][j],
                        qk=(p["na_bd"], p["na_gain"][j], 2 * D))
            o = _na_attention(qkv, p["na_bias"][j])
            x = _out_proj(o, p["na_w_out"][j], x, g1)
        x = _moe_layer(x, p["norm_ffn_g"][i][None], sh2, sc2, g2, p["moe_router"][i],
                       p["moe_w_gate"][i], p["moe_w_up"][i], p["moe_w_down"][i])
    return x


def kernel(x_prompt, x_sample, c_prompt, c_sample, norm_mix_g, norm_ffn_g, ada_w, ada_b, ret_w_in, ret_decay_logit, ret_w_out, na_w_in, na_q_gain, na_k_gain, na_rpb, na_w_out, moe_router, moe_w_gate, moe_w_up, moe_w_down):
    D = D_MODEL
    bp, bs = c_prompt.shape[0], c_sample.shape[0]
    c_all = jnp.concatenate([c_prompt, c_sample], axis=0)
    pad = (-c_all.shape[0]) % 8
    if pad:
        c_all = jnp.pad(c_all, ((0, pad), (0, 0)))
    mod = _ada_mod(c_all, ada_w, ada_b).reshape(DEPTH, c_all.shape[0], 6, D)

    heads_row = lambda v: jnp.tile(v.astype(F32), (1, NA_HEADS))
    na_gain = jnp.concatenate(
        [heads_row(na_q_gain) * (NA_HEAD_DIM ** -0.5), heads_row(na_k_gain),
         jnp.ones((na_q_gain.shape[0], D), F32)], axis=-1)[:, None, :]
    bd = np.kron(np.eye(256 // NA_HEAD_DIM), np.ones((NA_HEAD_DIM, NA_HEAD_DIM)))
    p = dict(
        norm_mix_g=norm_mix_g, norm_ffn_g=norm_ffn_g,
        ret_w_in=ret_w_in.astype(BF16), ret_decay_logit=ret_decay_logit, ret_w_out=ret_w_out.astype(BF16),
        na_w_in=na_w_in.astype(BF16), na_w_out=na_w_out.astype(BF16),
        na_gain=na_gain, na_bd=jnp.asarray(bd, BF16),
        na_bias=jnp.stack([_na_bias_table(na_rpb[l]) for l in range(na_rpb.shape[0])]),
        moe_router=moe_router,
        moe_w_gate=moe_w_gate.astype(BF16), moe_w_up=moe_w_up.astype(BF16), moe_w_down=moe_w_down.astype(BF16),
    )
    y_prompt = _trunk(x_prompt, mod[:, :bp], p)
    y_sample = _trunk(x_sample, mod[:, bp:bp + bs], p)
    return (y_prompt, y_sample)
```

```python
import functools
import math

import numpy as np
import jax
import jax.numpy as jnp
from jax import lax
from jax.experimental import pallas as pl
from jax.experimental.pallas import tpu as pltpu

F32 = jnp.float32
BF16 = jnp.bfloat16
I32 = jnp.int32

D_MODEL = 1024
DEPTH = 4

RET_HEADS = 4
RET_QK_DIM = 256
RET_V_DIM = 512
RET_QK_WIDTH = RET_HEADS * RET_QK_DIM
RET_V_WIDTH = RET_HEADS * RET_V_DIM
RET_IN_COLS = 2 * RET_QK_WIDTH + 3 * RET_V_WIDTH
ROPE_BASE = 10000.0
RET_BLOCK = 256

NA_HEADS = 16
NA_HEAD_DIM = 64
NA_KH = 8
NA_KW = 16
GRID_W = 64
NA_ROWS_PER_GROUP = 4
NA_WIN_ROWS = NA_ROWS_PER_GROUP + NA_KH - 1
NA_Q = NA_ROWS_PER_GROUP * GRID_W
NA_K = NA_WIN_ROWS * GRID_W

N_EXPERTS = 16
EC_CAPACITY_FACTOR = 2
D_FF = 2 * D_MODEL

EPS = 1e-6
NEG_INF = -1e30

LANES = 128
BF16_SUBLANES = 16
VMEM_LIMIT = 56 * 1024 * 1024

HIGHEST = lax.Precision.HIGHEST


def _cparams(sem):
    return pltpu.CompilerParams(dimension_semantics=sem, vmem_limit_bytes=VMEM_LIMIT)


def _tile(n, pref):
    t = min(pref, n)
    while n % t:
        t //= 2
    return t


def _silu(x):
    return x * (1.0 / (1.0 + jnp.exp(-x)))


def _ones_where(mask):
    return jnp.where(mask, 1.0, 0.0).astype(BF16)


def _dot(a, b):
    return jnp.dot(a, b, preferred_element_type=F32)


def _dot_nt(a, b):
    return lax.dot_general(a, b, (((1,), (1,)), ((), ())), preferred_element_type=F32)


def _dot_tn(a, b):
    return lax.dot_general(a, b, (((0,), (0,)), ((), ())), preferred_element_type=F32)


def _ada_kernel(c_ref, w_ref, b_ref, o_ref):
    ca = _silu(c_ref[...])
    o_ref[...] = jnp.dot(ca, w_ref[...], preferred_element_type=F32, precision=HIGHEST) + b_ref[...]


def _ada_mod(c_all, ada_w, ada_b):
    R, D = c_all.shape
    n_out = ada_w.shape[-1]
    tn = min(1536, n_out)
    return pl.pallas_call(
        _ada_kernel,
        grid=(DEPTH, n_out // tn),
        in_specs=[
            pl.BlockSpec((R, D), lambda l, j: (0, 0)),
            pl.BlockSpec((None, D, tn), lambda l, j: (l, 0, j)),
            pl.BlockSpec((None, 1, tn), lambda l, j: (l, 0, j)),
        ],
        out_specs=pl.BlockSpec((None, R, tn), lambda l, j: (l, 0, j)),
        out_shape=jax.ShapeDtypeStruct((DEPTH, R, n_out), F32),
        compiler_params=_cparams(("arbitrary", "arbitrary")),
        name="ada_mod",
    )(c_all, ada_w, ada_b.reshape(DEPTH, 1, n_out))


def _norm_mod(x, g, sh, sc):
    ms = jnp.mean(x * x, axis=-1, keepdims=True)
    y = x * lax.rsqrt(ms + EPS) * g
    return y * (1.0 + sc) + sh


def _proj_kernel(x_ref, g_ref, sh_ref, sc_ref, w_ref, o_ref, h_scr):
    @pl.when(pl.program_id(2) == 0)
    def _():
        h_scr[...] = _norm_mod(x_ref[...], g_ref[...], sh_ref[...], sc_ref[...]).astype(BF16)

    o_ref[...] = _dot(h_scr[...], w_ref[...]).astype(o_ref.dtype)


def _proj_qknorm_kernel(x_ref, g_ref, sh_ref, sc_ref, w_ref, bd_ref, gain_ref, o_ref, h_scr, *, qk_blocks, tn):
    j = pl.program_id(2)

    @pl.when(j == 0)
    def _():
        h_scr[...] = _norm_mod(x_ref[...], g_ref[...], sh_ref[...], sc_ref[...]).astype(BF16)

    acc = _dot(h_scr[...], w_ref[...])

    @pl.when(j < qk_blocks)
    def _():
        for c in range(tn // 256):
            a = acc[:, c * 256:(c + 1) * 256]
            ss = _dot((a * a).astype(BF16), bd_ref[...])
            r = lax.rsqrt(ss * (1.0 / NA_HEAD_DIM) + EPS)
            o_ref[:, c * 256:(c + 1) * 256] = (a * r * gain_ref[:, c * 256:(c + 1) * 256]).astype(o_ref.dtype)

    @pl.when(j >= qk_blocks)
    def _():
        o_ref[...] = acc.astype(o_ref.dtype)


def _proj(x, g, sh, sc, w, qk=None):
    B, T, D = x.shape
    n_out = w.shape[1]
    tm = _tile(T, 1024)
    tn = _tile(n_out, 1024)
    grid = (B, T // tm, n_out // tn)
    in_specs = [
        pl.BlockSpec((None, tm, D), lambda b, i, j: (b, i, 0)),
        pl.BlockSpec((1, D), lambda b, i, j: (0, 0)),
        pl.BlockSpec((None, 1, D), lambda b, i, j: (b, 0, 0)),
        pl.BlockSpec((None, 1, D), lambda b, i, j: (b, 0, 0)),
        pl.BlockSpec((D, tn), lambda b, i, j: (0, j)),
    ]
    args = [x, g, sh, sc, w]
    if qk is None:
        body = _proj_kernel
        name = "proj"
    else:
        bd, gain, n_qk = qk
        in_specs += [
            pl.BlockSpec((256, 256), lambda b, i, j: (0, 0)),
            pl.BlockSpec((1, tn), lambda b, i, j: (0, j)),
        ]
        args += [bd, gain]
        body = functools.partial(_proj_qknorm_kernel, qk_blocks=n_qk // tn, tn=tn)
        name = "proj_qknorm"
    return pl.pallas_call(
        body,
        grid=grid,
        in_specs=in_specs,
        out_specs=pl.BlockSpec((None, tm, tn), lambda b, i, j: (b, i, j)),
        out_shape=jax.ShapeDtypeStruct((B, T, n_out), BF16),
        scratch_shapes=[pltpu.VMEM((tm, D), BF16)],
        compiler_params=_cparams(("parallel", "parallel", "arbitrary")),
        name=name,
    )(*args)


def _outproj_kernel(y_ref, w_ref, x_ref, g_ref, o_ref):
    o_ref[...] = x_ref[...] + g_ref[...] * _dot(y_ref[...], w_ref[...])


def _out_proj(y, w, x, gate):
    B, T, K = y.shape
    D = w.shape[1]
    tm = _tile(T, 512)
    return pl.pallas_call(
        _outproj_kernel,
        grid=(B, T // tm),
        in_specs=[
            pl.BlockSpec((None, tm, K), lambda b, i: (b, i, 0)),
            pl.BlockSpec((K, D), lambda b, i: (0, 0)),
            pl.BlockSpec((None, tm, D), lambda b, i: (b, i, 0)),
            pl.BlockSpec((None, 1, D), lambda b, i: (b, 0, 0)),
        ],
        out_specs=pl.BlockSpec((None, tm, D), lambda b, i: (b, i, 0)),
        out_shape=jax.ShapeDtypeStruct((B, T, D), F32),
        compiler_params=_cparams(("parallel", "parallel")),
        name="out_proj",
    )(y, w, x, gate)


def _retention_kernel(*refs, has_prev):
    if has_prev:
        (q_ref, k_ref, v_ref, gate_ref, cos_ref, sin_ref, intra_ref, qdec_ref, kdec_ref, cdec_ref,
         prev_ref, o_ref, state) = refs
    else:
        (q_ref, k_ref, v_ref, gate_ref, cos_ref, sin_ref, intra_ref, qdec_ref, kdec_ref, cdec_ref,
         o_ref, state) = refs
        prev_ref = None

    @pl.when(pl.program_id(2) == 0)
    def _():
        state[...] = jnp.zeros_like(state)

    cos = cos_ref[...]
    sin = sin_ref[...]
    half = RET_QK_DIM // 2

    def rot(a):
        a1 = a[:, :half]
        a2 = a[:, half:]
        return jnp.concatenate([a1 * cos - a2 * sin, a1 * sin + a2 * cos], axis=-1)

    qr = rot(q_ref[...].astype(F32))
    kr = rot(k_ref[...].astype(F32)) * (RET_QK_DIM ** -0.5)
    v = v_ref[...]
    qb = qr.astype(BF16)
    scores = _dot_nt(qb, kr.astype(BF16)) * intra_ref[...]
    st = state[...]
    o = _dot(scores.astype(BF16), v) + _dot(qb, st.astype(BF16)) * qdec_ref[...]
    state[...] = st * cdec_ref[...] + _dot_tn((kr * kdec_ref[...]).astype(BF16), v)

    mu = jnp.mean(o, axis=-1, keepdims=True)
    oc = o - mu
    var = jnp.mean(oc * oc, axis=-1, keepdims=True)
    y = oc * lax.rsqrt(var + EPS) * _silu(gate_ref[...].astype(F32))
    if has_prev:
        y = y + prev_ref[...].astype(F32)
    o_ref[...] = y.astype(o_ref.dtype)


def _retention_dir(proj, tabs, cos, sin, direction, prev):
    B, T, _ = proj.shape
    rc = min(RET_BLOCK, T)
    nc = T // rc
    intra, qdec, kdec, cdec = tabs
    if direction == 0:
        cidx = lambda c: c
    else:
        cidx = lambda c: nc - 1 - c
    gate_base = (2 * RET_QK_WIDTH + (1 + direction) * RET_V_WIDTH) // RET_V_DIM
    hoff = direction * RET_HEADS
    in_specs = [
        pl.BlockSpec((None, rc, RET_QK_DIM), lambda b, h, c: (b, cidx(c), h)),
        pl.BlockSpec((None, rc, RET_QK_DIM), lambda b, h, c: (b, cidx(c), RET_HEADS + h)),
        pl.BlockSpec((None, rc, RET_V_DIM), lambda b, h, c: (b, cidx(c), 2 * RET_QK_WIDTH // RET_V_DIM + h)),
        pl.BlockSpec((None, rc, RET_V_DIM), lambda b, h, c: (b, cidx(c), gate_base + h)),
        pl.BlockSpec((rc, RET_QK_DIM // 2), lambda b, h, c: (cidx(c), 0)),
        pl.BlockSpec((rc, RET_QK_DIM // 2), lambda b, h, c: (cidx(c), 0)),
        pl.BlockSpec((None, rc, rc), lambda b, h, c: (hoff + h, 0, 0)),
        pl.BlockSpec((None, rc, 1), lambda b, h, c: (hoff + h, 0, 0)),
        pl.BlockSpec((None, rc, 1), lambda b, h, c: (hoff + h, 0, 0)),
        pl.BlockSpec((None, 1, 1), lambda b, h, c: (hoff + h, 0, 0)),
    ]
    args = [proj, proj, proj, proj, cos, sin, intra, qdec, kdec, cdec]
    if prev is not None:
        in_specs.append(pl.BlockSpec((None, rc, RET_V_DIM), lambda b, h, c: (b, cidx(c), h)))
        args.append(prev)
    return pl.pallas_call(
        functools.partial(_retention_kernel, has_prev=prev is not None),
        grid=(B, RET_HEADS, nc),
        in_specs=in_specs,
        out_specs=pl.BlockSpec((None, rc, RET_V_DIM), lambda b, h, c: (b, cidx(c), h)),
        out_shape=jax.ShapeDtypeStruct((B, T, RET_V_WIDTH), BF16),
        scratch_shapes=[pltpu.VMEM((RET_QK_DIM, RET_V_DIM), F32)],
        compiler_params=_cparams(("parallel", "parallel", "arbitrary")),
        name="retention_bwd" if direction else "retention_fwd",
    )(*args)


def _retention_tables(decay_logit, rc):
    lg = jax.nn.log_sigmoid(decay_logit.astype(F32).reshape(-1))
    pos = jnp.arange(rc, dtype=F32)
    rel = pos[:, None] - pos[None, :]
    l3 = lg[:, None, None]
    fwd = jnp.where(rel >= 0, jnp.exp(jnp.maximum(rel, 0.0) * l3), 0.0)
    bwd = jnp.where(rel <= 0, jnp.exp(jnp.maximum(-rel, 0.0) * l3), 0.0)
    is_bwd = (jnp.arange(2 * RET_HEADS) >= RET_HEADS)
    intra = jnp.where(is_bwd[:, None, None], bwd, fwd)
    q_pow = jnp.where(is_bwd[:, None], rc - pos[None, :], pos[None, :] + 1.0)
    k_pow = jnp.where(is_bwd[:, None], pos[None, :], rc - 1.0 - pos[None, :])
    qdec = jnp.exp(q_pow * lg[:, None])[..., None]
    kdec = jnp.exp(k_pow * lg[:, None])[..., None]
    cdec = jnp.exp(rc * lg)[:, None, None]
    return intra, qdec, kdec, cdec


def _rotary_tables(T):
    d = RET_QK_DIM
    inv = 1.0 / (ROPE_BASE ** (jnp.arange(0, d, 2, dtype=F32) / d))
    ang = jnp.arange(T, dtype=F32)[:, None] * inv[None, :]
    return jnp.cos(ang), jnp.sin(ang)


def _na_kernel(q_ref, k_ref, v_ref, bias_ref, o_ref, *, rows):
    n_groups = rows // NA_ROWS_PER_GROUP
    lane = lax.broadcasted_iota(I32, (NA_Q, LANES), 1)
    first = lane < NA_HEAD_DIM

    def group(gi, carry):
        ustart = jnp.clip(gi * NA_ROWS_PER_GROUP - NA_KH // 2, 0, rows - NA_WIN_ROWS)
        pat = jnp.where(gi == 0, 0, jnp.where(gi == n_groups - 1, 2, 1))
        kstart = pl.multiple_of(ustart * GRID_W, GRID_W)
        qstart = pl.multiple_of(gi * NA_Q, NA_Q)
        kw = k_ref[pl.ds(kstart, NA_K), :]
        vw = v_ref[pl.ds(kstart, NA_K), :]
        q = q_ref[pl.ds(qstart, NA_Q), :].astype(F32)
        outs = []
        for hh in range(2):
            qm = jnp.where(first if hh == 0 else jnp.logical_not(first), q, 0.0).astype(BF16)
            s = _dot_nt(qm, kw) + bias_ref[pat, hh]
            m = jnp.max(s, axis=-1, keepdims=True)
            e = jnp.exp(s - m)
            l = jnp.sum(e, axis=-1, keepdims=True)
            outs.append(_dot(e.astype(BF16), vw) / l)
        o_ref[pl.ds(qstart, NA_Q), :] = jnp.where(first, outs[0], outs[1]).astype(o_ref.dtype)
        return carry

    lax.fori_loop(0, n_groups, group, 0)


def _na_attention(qkv, bias):
    B, T, _ = qkv.shape
    rows = T // GRID_W
    assert rows % NA_ROWS_PER_GROUP == 0 and rows >= NA_WIN_ROWS + 1
    n_pairs = D_MODEL // LANES
    blk = lambda off: pl.BlockSpec((None, T, LANES), lambda p, b: (b, 0, off + p))
    return pl.pallas_call(
        functools.partial(_na_kernel, rows=rows),
        grid=(n_pairs, B),
        in_specs=[
            blk(0), blk(n_pairs), blk(2 * n_pairs),
            pl.BlockSpec((3, 2, NA_Q, NA_K), lambda p, b: (0, p, 0, 0)),
        ],
        out_specs=pl.BlockSpec((None, T, LANES), lambda p, b: (b, 0, p)),
        out_shape=jax.ShapeDtypeStruct((B, T, D_MODEL), BF16),
        compiler_params=_cparams(("parallel", "parallel")),
        name="na_attention",
    )(qkv, qkv, qkv, bias)


def _na_bias_table(rpb):
    H = rpb.shape[0]
    r = rpb.astype(F32)
    pad = GRID_W
    rp = jnp.concatenate([jnp.repeat(r[..., :1], pad, -1), r, jnp.repeat(r[..., -1:], pad, -1)], -1)
    base = NA_KW - 1 + pad
    t1 = jnp.stack([rp[..., base - qc: base - qc + GRID_W] for qc in range(GRID_W)], axis=2)
    qc = np.arange(GRID_W)[:, None]
    kc = np.arange(GRID_W)[None, :]
    ws = np.clip(qc - NA_KW // 2, 0, GRID_W - NA_KW)
    t1 = jnp.where(jnp.asarray((kc >= ws) & (kc < ws + NA_KW)), t1, NEG_INF)
    masked = jnp.full((H, GRID_W, GRID_W), NEG_INF, F32)
    pats = []
    for p in range(3):
        per_row = []
        for i in range(NA_ROWS_PER_GROUP):
            lo = (0, i, NA_ROWS_PER_GROUP - 1)[p]
            qrow = (i, i + NA_KH // 2, i + NA_KH - 1)[p]
            us = [t1[:, u - qrow + NA_KH - 1] if lo <= u < lo + NA_KH else masked for u in range(NA_WIN_ROWS)]
            per_row.append(jnp.stack(us, axis=2))
        pats.append(jnp.stack(per_row, axis=1))
    return jnp.stack(pats, axis=0).reshape(3, H, NA_Q, NA_K)


def _ffn_pre_kernel(x_ref, g_ref, sh_ref, sc_ref, r_ref, h_ref, aff_ref):
    h = _norm_mod(x_ref[...], g_ref[...], sh_ref[...], sc_ref[...])
    h_ref[...] = h.astype(BF16)
    logits = jnp.dot(h, r_ref[...], preferred_element_type=F32, precision=HIGHEST)
    m = jnp.max(logits, axis=-1, keepdims=True)
    e = jnp.exp(logits - m)
    aff_ref[...] = e / jnp.sum(e, axis=-1, keepdims=True)


def _ffn_pre(x, g, sh, sc, router):
    B, T, D = x.shape
    tm = _tile(T, 1024)
    nt = T // tm
    E = router.shape[1]
    return pl.pallas_call(
        _ffn_pre_kernel,
        grid=(B, nt),
        in_specs=[
            pl.BlockSpec((None, tm, D), lambda b, i: (b, i, 0)),
            pl.BlockSpec((1, D), lambda b, i: (0, 0)),
            pl.BlockSpec((None, 1, D), lambda b, i: (b, 0, 0)),
            pl.BlockSpec((None, 1, D), lambda b, i: (b, 0, 0)),
            pl.BlockSpec((D, E), lambda b, i: (0, 0)),
        ],
        out_specs=[
            pl.BlockSpec((tm, D), lambda b, i: (b * nt + i, 0)),
            pl.BlockSpec((tm, E), lambda b, i: (b * nt + i, 0)),
        ],
        out_shape=[jax.ShapeDtypeStruct((B * T, D), BF16), jax.ShapeDtypeStruct((B * T, E), F32)],
        compiler_params=_cparams(("parallel", "parallel")),
        name="ffn_pre",
    )(x, g, sh, sc, router)


def _select_kernel(aff_ref, pos_ref, pre_ref, *, cap, n_tok, rb):
    E = N_EXPERTS
    per_row = LANES // E
    rows = n_tok // per_row
    bits = pltpu.bitcast(aff_ref[...], I32)
    lane = lax.broadcasted_iota(I32, (rows, LANES), 1)
    tok = lax.broadcasted_iota(I32, (rows, LANES), 0) * per_row + lane // E

    def count(pred):
        c = jnp.sum(pred.astype(I32), axis=0, keepdims=True)
        c = jnp.broadcast_to(c, (8, LANES))
        sh = E
        while sh < LANES:
            c = c + pltpu.roll(c, sh, 1)
            sh *= 2
        return c[0:1]

    def thr_step(i, thr):
        cand = thr | jnp.left_shift(jnp.int32(1), 30 - i)
        return jnp.where(count(bits >= cand) >= cap, cand, thr)

    thr = lax.fori_loop(0, 31, thr_step, jnp.zeros((1, LANES), I32))
    gt = bits > thr
    eq = bits == thr
    need = cap - count(gt)

    nbits = max(1, int(math.ceil(math.log2(n_tok))))

    def tie_step(i, jmax):
        cand = jmax + jnp.left_shift(jnp.int32(1), nbits - 1 - i)
        return jnp.where(count(jnp.logical_and(eq, tok < cand)) < need, cand, jmax)

    jmax = lax.fori_loop(0, nbits, tie_step, jnp.zeros((1, LANES), I32))

    li = lax.broadcasted_iota(I32, (LANES, LANES), 0)
    lj = lax.broadcasted_iota(I32, (LANES, LANES), 1)
    same_e = (li % E) == (lj % E)
    a_before = _ones_where(jnp.logical_and(same_e, li // E < lj // E))
    a_all = _ones_where(same_e)
    ri = lax.broadcasted_iota(I32, (rb, rb), 0)
    rj = lax.broadcasted_iota(I32, (rb, rb), 1)
    lower = _ones_where(rj < ri)

    running = jnp.zeros((1, LANES), F32)
    for blk in range(rows // rb):
        sl = slice(blk * rb, (blk + 1) * rb)
        b_blk = pltpu.bitcast(aff_ref[sl, :], I32)
        t_blk = ((lax.broadcasted_iota(I32, (rb, LANES), 0) + blk * rb) * per_row
                 + lax.broadcasted_iota(I32, (rb, LANES), 1) // E)
        keep = jnp.logical_or(b_blk > thr, jnp.logical_and(b_blk == thr, t_blk <= jmax))
        kb = _ones_where(keep)
        row_tot = _dot(kb, a_all)
        within = _dot(lower, row_tot.astype(BF16))
        pre = running + within + _dot(kb, a_before)
        running = running + jnp.sum(row_tot, axis=0, keepdims=True)
        pre_i = pre.astype(I32)
        pre_ref[sl, :] = pre_i
        pos_ref[sl, :] = jnp.where(keep, pre_i, -1)


def _select(aff, cap):
    n_tok, E = aff.shape
    rows = n_tok * E // LANES
    rb = min(256, rows)
    aff8 = aff.reshape(rows, LANES)
    pos, pre = pl.pallas_call(
        functools.partial(_select_kernel, cap=cap, n_tok=n_tok, rb=rb),
        out_shape=[jax.ShapeDtypeStruct((rows, LANES), I32)] * 2,
        compiler_params=pltpu.CompilerParams(vmem_limit_bytes=VMEM_LIMIT),
        name="expert_select",
    )(aff8)
    return pos.reshape(n_tok, E), pre.reshape(n_tok, E)


def _dispatch_kernel(ws_ref, pos_ref, aff_ref, h_ref, xe_ref, gs_ref, *, n_tiles, win, sub):
    e = pl.program_id(0)
    tb = pl.program_id(1)
    tn = pos_ref.shape[-1]

    @pl.when(tb == 0)
    def _():
        xe_ref[...] = jnp.zeros_like(xe_ref)
        gs_ref[...] = jnp.zeros_like(gs_ref)

    def tile(s, carry):
        ws = pl.multiple_of(ws_ref[e * n_tiles + tb * sub + s], BF16_SUBLANES)
        prel = pos_ref[pl.ds(s, 1), :] - ws
        hit = lax.broadcasted_iota(I32, (win, tn), 0) == prel
        rows = _dot(_ones_where(hit), h_ref[pl.ds(pl.multiple_of(s * tn, tn), tn), :])
        xe_ref[pl.ds(ws, win), :] = (xe_ref[pl.ds(ws, win), :].astype(F32) + rows).astype(xe_ref.dtype)
        g = jnp.sum(jnp.where(hit, aff_ref[pl.ds(s, 1), :], 0.0), axis=1, keepdims=True)
        gs_ref[pl.ds(ws, win), :] = gs_ref[pl.ds(ws, win), :] + g
        return carry

    lax.fori_loop(0, sub, tile, 0)


def _dispatch(wstart, pos_t, aff_t, h, cap, tn):
    E, n_tiles, _ = pos_t.shape
    N, D = h.shape
    win = tn + BF16_SUBLANES
    sub = 8 if n_tiles % 8 == 0 else n_tiles
    grid_spec = pltpu.PrefetchScalarGridSpec(
        num_scalar_prefetch=1,
        grid=(E, n_tiles // sub),
        in_specs=[
            pl.BlockSpec((None, sub, tn), lambda e, t, ws: (e, t, 0)),
            pl.BlockSpec((None, sub, tn), lambda e, t, ws: (e, t, 0)),
            pl.BlockSpec((sub * tn, D), lambda e, t, ws: (t, 0)),
        ],
        out_specs=[
            pl.BlockSpec((None, cap, D), lambda e, t, ws: (e, 0, 0)),
            pl.BlockSpec((None, cap, 1), lambda e, t, ws: (e, 0, 0)),
        ],
    )
    return pl.pallas_call(
        functools.partial(_dispatch_kernel, n_tiles=n_tiles, win=win, sub=sub),
        grid_spec=grid_spec,
        out_shape=[jax.ShapeDtypeStruct((E, cap, D), BF16), jax.ShapeDtypeStruct((E, cap, 1), F32)],
        compiler_params=_cparams(("parallel", "arbitrary")),
        name="moe_dispatch",
    )(wstart, pos_t, aff_t, h)


def _ffn_kernel(x_ref, gs_ref, wg_ref, wu_ref, wd_ref, o_ref, acc, *, n_f):
    f = pl.program_id(2)

    @pl.when(f == 0)
    def _():
        acc[...] = jnp.zeros_like(acc)

    x = x_ref[...]
    hid = _silu(_dot(x, wg_ref[...])) * _dot(x, wu_ref[...])
    acc[...] += _dot(hid.astype(BF16), wd_ref[...])

    @pl.when(f == n_f - 1)
    def _():
        o_ref[...] = (acc[...] * gs_ref[...]).astype(o_ref.dtype)


def _expert_ffn(xe, gs, wg, wu, wd):
    E, C, D = xe.shape
    F = wg.shape[-1]
    tm = _tile(C, 1024)
    tf = _tile(F, 512)
    n_f = F // tf
    return pl.pallas_call(
        functools.partial(_ffn_kernel, n_f=n_f),
        grid=(E, C // tm, n_f),
        in_specs=[
            pl.BlockSpec((None, tm, D), lambda e, m, f: (e, m, 0)),
            pl.BlockSpec((None, tm, 1), lambda e, m, f: (e, m, 0)),
            pl.BlockSpec((None, D, tf), lambda e, m, f: (e, 0, f)),
            pl.BlockSpec((None, D, tf), lambda e, m, f: (e, 0, f)),
            pl.BlockSpec((None, tf, D), lambda e, m, f: (e, f, 0)),
        ],
        out_specs=pl.BlockSpec((None, tm, D), lambda e, m, f: (e, m, 0)),
        out_shape=jax.ShapeDtypeStruct((E, C, D), BF16),
        scratch_shapes=[pltpu.VMEM((tm, D), F32)],
        compiler_params=_cparams(("parallel", "parallel", "arbitrary")),
        name="expert_ffn",
    )(xe, gs, wg, wu, wd)


def _combine_kernel(ws_ref, end_ref, pos_ref, *refs, n_tiles, n_exp):
    ye_refs = refs[:n_exp]
    x_ref, g_ref, o_ref, hit_scr, y_scr = refs[n_exp:]
    t = pl.program_id(0)
    tn = pos_ref.shape[0]
    win = ye_refs[0].shape[1]
    lane = lax.broadcasted_iota(I32, (tn, tn), 1)
    for e in range(n_exp):
        prel = pos_ref[:, e:e + 1] - ws_ref[e * n_tiles + t]
        hit_scr[:, e * tn:(e + 1) * tn] = _ones_where(lane == prel)
        y_scr[e * tn:(e + 1) * tn, :] = ye_refs[e][0, :tn, :]
    o_ref[...] = x_ref[...] + g_ref[...] * _dot(hit_scr[...], y_scr[...])

    for e in range(n_exp):
        ws = ws_ref[e * n_tiles + t]

        @pl.when(end_ref[e * n_tiles + t] - ws > tn)
        def _():
            prel = pos_ref[:, e:e + 1] - ws - tn
            hit = _ones_where(lax.broadcasted_iota(I32, (tn, win - tn), 1) == prel)
            o_ref[...] += g_ref[...] * _dot(hit, ye_refs[e][0, tn:, :])


def _combine(wstart, wend, pos, ye, x, gate, tn):
    E, C, D = ye.shape
    B, T, _ = x.shape
    N = B * T
    n_tiles = N // tn
    tpb = T // tn
    win = tn + BF16_SUBLANES

    def window(e):
        return pl.BlockSpec((pl.Element(1), pl.Element(win), pl.Element(D)),
                            lambda t, ws, we: (e, pl.multiple_of(ws[e * n_tiles + t], BF16_SUBLANES), 0))

    grid_spec = pltpu.PrefetchScalarGridSpec(
        num_scalar_prefetch=2,
        grid=(n_tiles,),
        in_specs=[pl.BlockSpec((tn, E), lambda t, ws, we: (t, 0))]
        + [window(e) for e in range(E)]
        + [pl.BlockSpec((None, tn, D), lambda t, ws, we: (t // tpb, t % tpb, 0)),
           pl.BlockSpec((None, 1, D), lambda t, ws, we: (t // tpb, 0, 0))],
        out_specs=pl.BlockSpec((None, tn, D), lambda t, ws, we: (t // tpb, t % tpb, 0)),
        scratch_shapes=[pltpu.VMEM((tn, E * tn), BF16), pltpu.VMEM((E * tn, D), BF16)],
    )
    return pl.pallas_call(
        functools.partial(_combine_kernel, n_tiles=n_tiles, n_exp=E),
        grid_spec=grid_spec,
        out_shape=jax.ShapeDtypeStruct((B, T, D), F32),
        compiler_params=_cparams(("parallel",)),
        name="moe_combine",
    )(wstart, wend, pos, *([ye] * E), x, gate)


def _moe_tile(n_tok, cap):
    tn = 256
    while tn + BF16_SUBLANES > cap or n_tok % tn:
        tn //= 2
    return tn


def _moe_layer(x, g, sh, sc, gate, router, wg, wu, wd):
    B, T, D = x.shape
    N = B * T
    E = router.shape[1]
    cap = EC_CAPACITY_FACTOR * N // E
    tn = _moe_tile(T, cap)
    win = tn + BF16_SUBLANES
    h, aff = _ffn_pre(x, g, sh, sc, router)
    pos, pre = _select(aff, cap)
    start = pre[::tn].T
    wstart = jnp.minimum(start // BF16_SUBLANES * BF16_SUBLANES, cap - win).reshape(-1).astype(I32)
    wend = jnp.concatenate([start[:, 1:], jnp.full((E, 1), cap, I32)], axis=1).reshape(-1).astype(I32)
    pos_t = pos.T.reshape(E, N // tn, tn)
    aff_t = aff.T.reshape(E, N // tn, tn)
    xe, gs = _dispatch(wstart, pos_t, aff_t, h, cap, tn)
    ye = _expert_ffn(xe, gs, wg, wu, wd)
    return _combine(wstart, wend, pos, ye, x, gate, tn)


def _trunk(x, mod, p):
    B, T, D = x.shape
    rc = min(RET_BLOCK, T)
    cos, sin = _rotary_tables(T)
    for i in range(DEPTH):
        sh1, sc1, g1, sh2, sc2, g2 = [mod[i, :, k][:, None, :] for k in range(6)]
        j = i // 2
        if i % 2 == 0:
            proj = _proj(x, p["norm_mix_g"][i][None], sh1, sc1, p["ret_w_in"][j])
            tabs = _retention_tables(p["ret_decay_logit"][j], rc)
            y_f = _retention_dir(proj, tabs, cos, sin, 0, None)
            y = _retention_dir(proj, tabs, cos, sin, 1, y_f)
            x = _out_proj(y, p["ret_w_out"][j], x, g1)
        else:
            qkv = _proj(x, p["norm_mix_g"][i][None], sh1, sc1, p["na_w_in"][j],
                        qk=(p["na_bd"], p["na_gain"][j], 2 * D))
            o = _na_attention(qkv, p["na_bias"][j])
            x = _out_proj(o, p["na_w_out"][j], x, g1)
        x = _moe_layer(x, p["norm_ffn_g"][i][None], sh2, sc2, g2, p["moe_router"][i],
                       p["moe_w_gate"][i], p["moe_w_up"][i], p["moe_w_down"][i])
    return x


def kernel(x_prompt, x_sample, c_prompt, c_sample, norm_mix_g, norm_ffn_g, ada_w, ada_b, ret_w_in, ret_decay_logit, ret_w_out, na_w_in, na_q_gain, na_k_gain, na_rpb, na_w_out, moe_router, moe_w_gate, moe_w_up, moe_w_down):
    D = D_MODEL
    bp, bs = c_prompt.shape[0], c_sample.shape[0]
    c_all = jnp.concatenate([c_prompt, c_sample], axis=0)
    pad = (-c_all.shape[0]) % 8
    if pad:
        c_all = jnp.pad(c_all, ((0, pad), (0, 0)))
    mod = _ada_mod(c_all, ada_w, ada_b).reshape(DEPTH, c_all.shape[0], 6, D)

    heads_row = lambda v: jnp.tile(v.astype(F32), (1, NA_HEADS))
    na_gain = jnp.concatenate(
        [heads_row(na_q_gain) * (NA_HEAD_DIM ** -0.5), heads_row(na_k_gain),
         jnp.ones((na_q_gain.shape[0], D), F32)], axis=-1)[:, None, :]
    bd = np.kron(np.eye(256 // NA_HEAD_DIM), np.ones((NA_HEAD_DIM, NA_HEAD_DIM)))
    p = dict(
        norm_mix_g=norm_mix_g, norm_ffn_g=norm_ffn_g,
        ret_w_in=ret_w_in.astype(BF16), ret_decay_logit=ret_decay_logit, ret_w_out=ret_w_out.astype(BF16),
        na_w_in=na_w_in.astype(BF16), na_w_out=na_w_out.astype(BF16),
        na_gain=na_gain, na_bd=jnp.asarray(bd, BF16),
        na_bias=jnp.stack([_na_bias_table(na_rpb[l]) for l in range(na_rpb.shape[0])]),
        moe_router=moe_router,
        moe_w_gate=moe_w_gate.astype(BF16), moe_w_up=moe_w_up.astype(BF16), moe_w_down=moe_w_down.astype(BF16),
    )
    y_prompt = _trunk(x_prompt, mod[:, :bp], p)
    y_sample = _trunk(x_sample, mod[:, bp:bp + bs], p)
    return (y_prompt, y_sample)
```

```python
import functools
import math

import numpy as np
import jax
import jax.numpy as jnp
from jax import lax
from jax.experimental import pallas as pl
from jax.experimental.pallas import tpu as pltpu

F32 = jnp.float32
BF16 = jnp.bfloat16
I32 = jnp.int32

D_MODEL = 1024
DEPTH = 4

RET_HEADS = 4
RET_QK_DIM = 256
RET_V_DIM = 512
RET_QK_WIDTH = RET_HEADS * RET_QK_DIM
RET_V_WIDTH = RET_HEADS * RET_V_DIM
RET_IN_COLS = 2 * RET_QK_WIDTH + 3 * RET_V_WIDTH
ROPE_BASE = 10000.0
RET_BLOCK = 256

NA_HEADS = 16
NA_HEAD_DIM = 64
NA_KH = 8
NA_KW = 16
GRID_W = 64
NA_ROWS_PER_GROUP = 4
NA_WIN_ROWS = NA_ROWS_PER_GROUP + NA_KH - 1
NA_Q = NA_ROWS_PER_GROUP * GRID_W
NA_K = NA_WIN_ROWS * GRID_W

N_EXPERTS = 16
EC_CAPACITY_FACTOR = 2
D_FF = 2 * D_MODEL

EPS = 1e-6
NEG_INF = -1e30

LANES = 128
BF16_SUBLANES = 16
VMEM_LIMIT = 56 * 1024 * 1024

HIGHEST = lax.Precision.HIGHEST


def _cparams(sem):
    return pltpu.CompilerParams(dimension_semantics=sem, vmem_limit_bytes=VMEM_LIMIT)


def _tile(n, pref):
    t = min(pref, n)
    while n % t:
        t //= 2
    return t


def _silu(x):
    return x * (1.0 / (1.0 + jnp.exp(-x)))


def _ones_where(mask):
    return jnp.where(mask, 1.0, 0.0).astype(BF16)


def _dot(a, b):
    return jnp.dot(a, b, preferred_element_type=F32)


def _dot_nt(a, b):
    return lax.dot_general(a, b, (((1,), (1,)), ((), ())), preferred_element_type=F32)


def _dot_tn(a, b):
    return lax.dot_general(a, b, (((0,), (0,)), ((), ())), preferred_element_type=F32)


def _ada_kernel(c_ref, w_ref, b_ref, o_ref):
    ca = _silu(c_ref[...])
    o_ref[...] = jnp.dot(ca, w_ref[...], preferred_element_type=F32, precision=HIGHEST) + b_ref[...]


def _ada_mod(c_all, ada_w, ada_b):
    R, D = c_all.shape
    n_out = ada_w.shape[-1]
    tn = min(1536, n_out)
    return pl.pallas_call(
        _ada_kernel,
        grid=(DEPTH, n_out // tn),
        in_specs=[
            pl.BlockSpec((R, D), lambda l, j: (0, 0)),
            pl.BlockSpec((None, D, tn), lambda l, j: (l, 0, j)),
            pl.BlockSpec((None, 1, tn), lambda l, j: (l, 0, j)),
        ],
        out_specs=pl.BlockSpec((None, R, tn), lambda l, j: (l, 0, j)),
        out_shape=jax.ShapeDtypeStruct((DEPTH, R, n_out), F32),
        compiler_params=_cparams(("arbitrary", "arbitrary")),
        name="ada_mod",
    )(c_all, ada_w, ada_b.reshape(DEPTH, 1, n_out))


def _norm_mod(x, g, sh, sc):
    ms = jnp.mean(x * x, axis=-1, keepdims=True)
    y = x * lax.rsqrt(ms + EPS) * g
    return y * (1.0 + sc) + sh


def _proj_kernel(x_ref, g_ref, sh_ref, sc_ref, w_ref, o_ref, h_scr):
    @pl.when(pl.program_id(2) == 0)
    def _():
        h_scr[...] = _norm_mod(x_ref[...], g_ref[...], sh_ref[...], sc_ref[...]).astype(BF16)

    o_ref[...] = _dot(h_scr[...], w_ref[...]).astype(o_ref.dtype)


def _proj_qknorm_kernel(x_ref, g_ref, sh_ref, sc_ref, w_ref, bd_ref, gain_ref, o_ref, h_scr, *, qk_blocks, tn):
    j = pl.program_id(2)

    @pl.when(j == 0)
    def _():
        h_scr[...] = _norm_mod(x_ref[...], g_ref[...], sh_ref[...], sc_ref[...]).astype(BF16)

    acc = _dot(h_scr[...], w_ref[...])

    @pl.when(j < qk_blocks)
    def _():
        for c in range(tn // 256):
            a = acc[:, c * 256:(c + 1) * 256]
            ss = _dot((a * a).astype(BF16), bd_ref[...])
            r = lax.rsqrt(ss * (1.0 / NA_HEAD_DIM) + EPS)
            o_ref[:, c * 256:(c + 1) * 256] = (a * r * gain_ref[:, c * 256:(c + 1) * 256]).astype(o_ref.dtype)

    @pl.when(j >= qk_blocks)
    def _():
        o_ref[...] = acc.astype(o_ref.dtype)


def _proj_rotary_kernel(x_ref, g_ref, sh_ref, sc_ref, w_ref, cos_ref, sin_ref, o_ref, h_scr, *, tn):
    j = pl.program_id(2)

    @pl.when(j == 0)
    def _():
        h_scr[...] = _norm_mod(x_ref[...], g_ref[...], sh_ref[...], sc_ref[...]).astype(BF16)

    acc = _dot(h_scr[...], w_ref[...])
    qk_blocks = 2 * RET_QK_WIDTH // tn
    half = RET_QK_DIM // 2

    @pl.when(j < qk_blocks)
    def _():
        scale = jnp.where(j < qk_blocks // 2, 1.0, RET_QK_DIM ** -0.5)
        cos = cos_ref[...] * scale
        sin = sin_ref[...] * scale
        for c in range(tn // RET_QK_DIM):
            a1 = acc[:, c * RET_QK_DIM:c * RET_QK_DIM + half]
            a2 = acc[:, c * RET_QK_DIM + half:(c + 1) * RET_QK_DIM]
            o_ref[:, c * RET_QK_DIM:c * RET_QK_DIM + half] = (a1 * cos - a2 * sin).astype(o_ref.dtype)
            o_ref[:, c * RET_QK_DIM + half:(c + 1) * RET_QK_DIM] = (a1 * sin + a2 * cos).astype(o_ref.dtype)

    @pl.when(j >= qk_blocks)
    def _():
        o_ref[...] = acc.astype(o_ref.dtype)


def _proj(x, g, sh, sc, w, qk=None, rotary=None):
    B, T, D = x.shape
    n_out = w.shape[1]
    tm = _tile(T, 1024)
    tn = _tile(n_out, 1024)
    grid = (B, T // tm, n_out // tn)
    in_specs = [
        pl.BlockSpec((None, tm, D), lambda b, i, j: (b, i, 0)),
        pl.BlockSpec((1, D), lambda b, i, j: (0, 0)),
        pl.BlockSpec((None, 1, D), lambda b, i, j: (b, 0, 0)),
        pl.BlockSpec((None, 1, D), lambda b, i, j: (b, 0, 0)),
        pl.BlockSpec((D, tn), lambda b, i, j: (0, j)),
    ]
    args = [x, g, sh, sc, w]
    if rotary is not None:
        half = RET_QK_DIM // 2
        in_specs += [pl.BlockSpec((tm, half), lambda b, i, j: (i, 0))] * 2
        args += list(rotary)
        body = functools.partial(_proj_rotary_kernel, tn=tn)
        name = "proj_rotary"
    elif qk is None:
        body = _proj_kernel
        name = "proj"
    else:
        bd, gain, n_qk = qk
        in_specs += [
            pl.BlockSpec((256, 256), lambda b, i, j: (0, 0)),
            pl.BlockSpec((1, tn), lambda b, i, j: (0, j)),
        ]
        args += [bd, gain]
        body = functools.partial(_proj_qknorm_kernel, qk_blocks=n_qk // tn, tn=tn)
        name = "proj_qknorm"
    return pl.pallas_call(
        body,
        grid=grid,
        in_specs=in_specs,
        out_specs=pl.BlockSpec((None, tm, tn), lambda b, i, j: (b, i, j)),
        out_shape=jax.ShapeDtypeStruct((B, T, n_out), BF16),
        scratch_shapes=[pltpu.VMEM((tm, D), BF16)],
        compiler_params=_cparams(("parallel", "parallel", "arbitrary")),
        name=name,
    )(*args)


def _outproj_kernel(y_ref, w_ref, x_ref, g_ref, o_ref):
    o_ref[...] = x_ref[...] + g_ref[...] * _dot(y_ref[...], w_ref[...])


def _out_proj(y, w, x, gate):
    B, T, K = y.shape
    D = w.shape[1]
    tm = _tile(T, 512)
    return pl.pallas_call(
        _outproj_kernel,
        grid=(B, T // tm),
        in_specs=[
            pl.BlockSpec((None, tm, K), lambda b, i: (b, i, 0)),
            pl.BlockSpec((K, D), lambda b, i: (0, 0)),
            pl.BlockSpec((None, tm, D), lambda b, i: (b, i, 0)),
            pl.BlockSpec((None, 1, D), lambda b, i: (b, 0, 0)),
        ],
        out_specs=pl.BlockSpec((None, tm, D), lambda b, i: (b, i, 0)),
        out_shape=jax.ShapeDtypeStruct((B, T, D), F32),
        compiler_params=_cparams(("parallel", "parallel")),
        name="out_proj",
    )(y, w, x, gate)


def _retention_kernel(*refs, has_prev):
    if has_prev:
        q_ref, k_ref, v_ref, gate_ref, intra_ref, qdec_ref, kdec_ref, cdec_ref, prev_ref, o_ref, state = refs
    else:
        q_ref, k_ref, v_ref, gate_ref, intra_ref, qdec_ref, kdec_ref, cdec_ref, o_ref, state = refs
        prev_ref = None

    @pl.when(pl.program_id(1) == 0)
    def _():
        state[...] = jnp.zeros_like(state)

    for h in range(RET_HEADS):
        qk = slice(h * RET_QK_DIM, (h + 1) * RET_QK_DIM)
        vs = slice(h * RET_V_DIM, (h + 1) * RET_V_DIM)
        qb = q_ref[:, qk]
        kb = k_ref[:, qk]
        v = v_ref[:, vs]
        scores = _dot_nt(qb, kb) * intra_ref[h]
        st = state[h]
        o = _dot(scores.astype(BF16), v) + _dot(qb, st.astype(BF16)) * qdec_ref[h]
        state[h] = st * cdec_ref[h] + _dot_tn((kb.astype(F32) * kdec_ref[h]).astype(BF16), v)

        mu = jnp.mean(o, axis=-1, keepdims=True)
        oc = o - mu
        var = jnp.mean(oc * oc, axis=-1, keepdims=True)
        y = oc * lax.rsqrt(var + EPS) * _silu(gate_ref[:, vs].astype(F32))
        if has_prev:
            y = y + prev_ref[:, vs].astype(F32)
        o_ref[:, vs] = y.astype(o_ref.dtype)


def _retention_dir(proj, tabs, direction, prev):
    B, T, _ = proj.shape
    rc = min(RET_BLOCK, T)
    nc = T // rc
    intra, qdec, kdec, cdec = tabs
    if direction == 0:
        cidx = lambda c: c
    else:
        cidx = lambda c: nc - 1 - c
    H = RET_HEADS
    v_blk = 2 * RET_QK_WIDTH // RET_V_WIDTH
    in_specs = [
        pl.BlockSpec((None, rc, RET_QK_WIDTH), lambda b, c: (b, cidx(c), 0)),
        pl.BlockSpec((None, rc, RET_QK_WIDTH), lambda b, c: (b, cidx(c), 1)),
        pl.BlockSpec((None, rc, RET_V_WIDTH), lambda b, c: (b, cidx(c), v_blk)),
        pl.BlockSpec((None, rc, RET_V_WIDTH), lambda b, c: (b, cidx(c), v_blk + 1 + direction)),
        pl.BlockSpec((H, rc, rc), lambda b, c: (direction, 0, 0)),
        pl.BlockSpec((H, rc, 1), lambda b, c: (direction, 0, 0)),
        pl.BlockSpec((H, rc, 1), lambda b, c: (direction, 0, 0)),
        pl.BlockSpec((H, 1, 1), lambda b, c: (direction, 0, 0)),
    ]
    args = [proj, proj, proj, proj, intra, qdec, kdec, cdec]
    if prev is not None:
        in_specs.append(pl.BlockSpec((None, rc, RET_V_WIDTH), lambda b, c: (b, cidx(c), 0)))
        args.append(prev)
    return pl.pallas_call(
        functools.partial(_retention_kernel, has_prev=prev is not None),
        grid=(B, nc),
        in_specs=in_specs,
        out_specs=pl.BlockSpec((None, rc, RET_V_WIDTH), lambda b, c: (b, cidx(c), 0)),
        out_shape=jax.ShapeDtypeStruct((B, T, RET_V_WIDTH), BF16),
        scratch_shapes=[pltpu.VMEM((H, RET_QK_DIM, RET_V_DIM), F32)],
        compiler_params=_cparams(("parallel", "arbitrary")),
        name="retention_bwd" if direction else "retention_fwd",
    )(*args)


def _retention_tables(decay_logit, rc):
    lg = jax.nn.log_sigmoid(decay_logit.astype(F32).reshape(-1))
    pos = jnp.arange(rc, dtype=F32)
    rel = pos[:, None] - pos[None, :]
    l3 = lg[:, None, None]
    fwd = jnp.where(rel >= 0, jnp.exp(jnp.maximum(rel, 0.0) * l3), 0.0)
    bwd = jnp.where(rel <= 0, jnp.exp(jnp.maximum(-rel, 0.0) * l3), 0.0)
    is_bwd = (jnp.arange(2 * RET_HEADS) >= RET_HEADS)
    intra = jnp.where(is_bwd[:, None, None], bwd, fwd)
    q_pow = jnp.where(is_bwd[:, None], rc - pos[None, :], pos[None, :] + 1.0)
    k_pow = jnp.where(is_bwd[:, None], pos[None, :], rc - 1.0 - pos[None, :])
    qdec = jnp.exp(q_pow * lg[:, None])[..., None]
    kdec = jnp.exp(k_pow * lg[:, None])[..., None]
    cdec = jnp.exp(rc * lg)[:, None, None]
    return intra, qdec, kdec, cdec


def _rotary_tables(T):
    d = RET_QK_DIM
    inv = 1.0 / (ROPE_BASE ** (jnp.arange(0, d, 2, dtype=F32) / d))
    ang = jnp.arange(T, dtype=F32)[:, None] * inv[None, :]
    return jnp.cos(ang), jnp.sin(ang)


def _na_kernel(q_ref, k_ref, v_ref, bias_ref, o_ref, *, rows):
    n_groups = rows // NA_ROWS_PER_GROUP
    lane = lax.broadcasted_iota(I32, (NA_Q, LANES), 1)
    first = lane < NA_HEAD_DIM

    def group(gi, carry):
        ustart = jnp.clip(gi * NA_ROWS_PER_GROUP - NA_KH // 2, 0, rows - NA_WIN_ROWS)
        pat = jnp.where(gi == 0, 0, jnp.where(gi == n_groups - 1, 2, 1))
        kstart = pl.multiple_of(ustart * GRID_W, GRID_W)
        qstart = pl.multiple_of(gi * NA_Q, NA_Q)
        kw = k_ref[pl.ds(kstart, NA_K), :]
        vw = v_ref[pl.ds(kstart, NA_K), :]
        q = q_ref[pl.ds(qstart, NA_Q), :].astype(F32)
        q2 = jnp.concatenate([jnp.where(first, q, 0.0), jnp.where(first, 0.0, q)], axis=0).astype(BF16)
        s = _dot_nt(q2, kw) + bias_ref[pat].reshape(2 * NA_Q, NA_K)
        m = jnp.max(s, axis=-1, keepdims=True)
        e = jnp.exp(s - m)
        l = jnp.sum(e, axis=-1, keepdims=True)
        o2 = _dot(e.astype(BF16), vw) / l
        o_ref[pl.ds(qstart, NA_Q), :] = jnp.where(first, o2[:NA_Q], o2[NA_Q:]).astype(o_ref.dtype)
        return carry

    lax.fori_loop(0, n_groups, group, 0, unroll=2)


def _na_attention(qkv, bias):
    B, T, _ = qkv.shape
    rows = T // GRID_W
    assert rows % NA_ROWS_PER_GROUP == 0 and rows >= NA_WIN_ROWS + 1
    n_pairs = D_MODEL // LANES
    blk = lambda off: pl.BlockSpec((None, T, LANES), lambda p, b: (b, 0, off + p))
    return pl.pallas_call(
        functools.partial(_na_kernel, rows=rows),
        grid=(n_pairs, B),
        in_specs=[
            blk(0), blk(n_pairs), blk(2 * n_pairs),
            pl.BlockSpec((3, 2, NA_Q, NA_K), lambda p, b: (0, p, 0, 0)),
        ],
        out_specs=pl.BlockSpec((None, T, LANES), lambda p, b: (b, 0, p)),
        out_shape=jax.ShapeDtypeStruct((B, T, D_MODEL), BF16),
        compiler_params=_cparams(("parallel", "parallel")),
        name="na_attention",
    )(qkv, qkv, qkv, bias)


def _na_bias_table(rpb):
    H = rpb.shape[0]
    r = rpb.astype(F32)
    pad = GRID_W
    rp = jnp.concatenate([jnp.repeat(r[..., :1], pad, -1), r, jnp.repeat(r[..., -1:], pad, -1)], -1)
    base = NA_KW - 1 + pad
    t1 = jnp.stack([rp[..., base - qc: base - qc + GRID_W] for qc in range(GRID_W)], axis=2)
    qc = np.arange(GRID_W)[:, None]
    kc = np.arange(GRID_W)[None, :]
    ws = np.clip(qc - NA_KW // 2, 0, GRID_W - NA_KW)
    t1 = jnp.where(jnp.asarray((kc >= ws) & (kc < ws + NA_KW)), t1, NEG_INF)
    masked = jnp.full((H, GRID_W, GRID_W), NEG_INF, F32)
    pats = []
    for p in range(3):
        per_row = []
        for i in range(NA_ROWS_PER_GROUP):
            lo = (0, i, NA_ROWS_PER_GROUP - 1)[p]
            qrow = (i, i + NA_KH // 2, i + NA_KH - 1)[p]
            us = [t1[:, u - qrow + NA_KH - 1] if lo <= u < lo + NA_KH else masked for u in range(NA_WIN_ROWS)]
            per_row.append(jnp.stack(us, axis=2))
        pats.append(jnp.stack(per_row, axis=1))
    return jnp.stack(pats, axis=0).reshape(3, H, NA_Q, NA_K)


def _ffn_pre_kernel(x_ref, g_ref, sh_ref, sc_ref, r_ref, h_ref, aff_ref):
    h = _norm_mod(x_ref[...], g_ref[...], sh_ref[...], sc_ref[...])
    h_ref[...] = h.astype(BF16)
    logits = jnp.dot(h, r_ref[...], preferred_element_type=F32, precision=HIGHEST)
    m = jnp.max(logits, axis=-1, keepdims=True)
    e = jnp.exp(logits - m)
    aff_ref[...] = e / jnp.sum(e, axis=-1, keepdims=True)


def _ffn_pre(x, g, sh, sc, router):
    B, T, D = x.shape
    tm = _tile(T, 1024)
    nt = T // tm
    E = router.shape[1]
    return pl.pallas_call(
        _ffn_pre_kernel,
        grid=(B, nt),
        in_specs=[
            pl.BlockSpec((None, tm, D), lambda b, i: (b, i, 0)),
            pl.BlockSpec((1, D), lambda b, i: (0, 0)),
            pl.BlockSpec((None, 1, D), lambda b, i: (b, 0, 0)),
            pl.BlockSpec((None, 1, D), lambda b, i: (b, 0, 0)),
            pl.BlockSpec((D, E), lambda b, i: (0, 0)),
        ],
        out_specs=[
            pl.BlockSpec((tm, D), lambda b, i: (b * nt + i, 0)),
            pl.BlockSpec((tm, E), lambda b, i: (b * nt + i, 0)),
        ],
        out_shape=[jax.ShapeDtypeStruct((B * T, D), BF16), jax.ShapeDtypeStruct((B * T, E), F32)],
        compiler_params=_cparams(("parallel", "parallel")),
        name="ffn_pre",
    )(x, g, sh, sc, router)


def _select_kernel(aff_ref, pos_ref, pre_ref, *, cap, n_tok, rb):
    E = N_EXPERTS
    per_row = LANES // E
    rows = n_tok // per_row
    bits = pltpu.bitcast(aff_ref[...], I32)
    lane = lax.broadcasted_iota(I32, (rows, LANES), 1)
    tok = lax.broadcasted_iota(I32, (rows, LANES), 0) * per_row + lane // E

    def count(pred):
        c = jnp.sum(pred.astype(I32), axis=0, keepdims=True)
        c = jnp.broadcast_to(c, (8, LANES))
        sh = E
        while sh < LANES:
            c = c + pltpu.roll(c, sh, 1)
            sh *= 2
        return c[0:1]

    def thr_step(i, thr):
        cand = thr | jnp.left_shift(jnp.int32(1), 30 - i)
        return jnp.where(count(bits >= cand) >= cap, cand, thr)

    thr = lax.fori_loop(0, 31, thr_step, jnp.zeros((1, LANES), I32))
    gt = bits > thr
    eq = bits == thr
    need = cap - count(gt)

    nbits = max(1, int(math.ceil(math.log2(n_tok))))

    def tie_step(i, jmax):
        cand = jmax + jnp.left_shift(jnp.int32(1), nbits - 1 - i)
        return jnp.where(count(jnp.logical_and(eq, tok < cand)) < need, cand, jmax)

    jmax = lax.fori_loop(0, nbits, tie_step, jnp.zeros((1, LANES), I32))

    li = lax.broadcasted_iota(I32, (LANES, LANES), 0)
    lj = lax.broadcasted_iota(I32, (LANES, LANES), 1)
    same_e = (li % E) == (lj % E)
    a_before = _ones_where(jnp.logical_and(same_e, li // E < lj // E))
    a_all = _ones_where(same_e)
    ri = lax.broadcasted_iota(I32, (rb, rb), 0)
    rj = lax.broadcasted_iota(I32, (rb, rb), 1)
    lower = _ones_where(rj < ri)

    running = jnp.zeros((1, LANES), F32)
    for blk in range(rows // rb):
        sl = slice(blk * rb, (blk + 1) * rb)
        b_blk = pltpu.bitcast(aff_ref[sl, :], I32)
        t_blk = ((lax.broadcasted_iota(I32, (rb, LANES), 0) + blk * rb) * per_row
                 + lax.broadcasted_iota(I32, (rb, LANES), 1) // E)
        keep = jnp.logical_or(b_blk > thr, jnp.logical_and(b_blk == thr, t_blk <= jmax))
        kb = _ones_where(keep)
        row_tot = _dot(kb, a_all)
        within = _dot(lower, row_tot.astype(BF16))
        pre = running + within + _dot(kb, a_before)
        running = running + jnp.sum(row_tot, axis=0, keepdims=True)
        pre_i = pre.astype(I32)
        pre_ref[sl, :] = pre_i
        pos_ref[sl, :] = jnp.where(keep, pre_i, -1)


def _select(aff, cap):
    n_tok, E = aff.shape
    rows = n_tok * E // LANES
    rb = min(256, rows)
    aff8 = aff.reshape(rows, LANES)
    pos, pre = pl.pallas_call(
        functools.partial(_select_kernel, cap=cap, n_tok=n_tok, rb=rb),
        out_shape=[jax.ShapeDtypeStruct((rows, LANES), I32)] * 2,
        compiler_params=pltpu.CompilerParams(vmem_limit_bytes=VMEM_LIMIT),
        name="expert_select",
    )(aff8)
    return pos.reshape(n_tok, E), pre.reshape(n_tok, E)


def _dispatch_kernel(ws_ref, pos_ref, aff_ref, h_ref, xe_ref, gs_ref, xacc, gacc, *, n_tiles, n_blocks, win, sub):
    e = pl.program_id(0)
    tb = pl.program_id(1)
    tn = pos_ref.shape[-1]
    cap = xe_ref.shape[0]
    head = BF16_SUBLANES

    @pl.when(tb == 0)
    def _():
        xacc[0:head, :] = jnp.zeros((head, xacc.shape[1]), xacc.dtype)
        gacc[0:head, :] = jnp.zeros((head, 1), gacc.dtype)

    def tile(s, carry):
        ws = pl.multiple_of(ws_ref[e * n_tiles + tb * sub + s], head)
        prel = pos_ref[pl.ds(s, 1), :] - ws
        hit = lax.broadcasted_iota(I32, (win, tn), 0) == prel
        rows = _dot(_ones_where(hit), h_ref[pl.ds(pl.multiple_of(s * tn, tn), tn), :])
        g = jnp.sum(jnp.where(hit, aff_ref[pl.ds(s, 1), :], 0.0), axis=1, keepdims=True)
        xacc[pl.ds(ws, head), :] = (xacc[pl.ds(ws, head), :].astype(F32) + rows[:head]).astype(xacc.dtype)
        gacc[pl.ds(ws, head), :] = gacc[pl.ds(ws, head), :] + g[:head]
        rest = pl.multiple_of(ws + head, head)
        xacc[pl.ds(rest, win - head), :] = rows[head:].astype(xacc.dtype)
        gacc[pl.ds(rest, win - head), :] = g[head:]
        return carry

    lax.fori_loop(0, sub, tile, 0, unroll=4)

    @pl.when(tb == n_blocks - 1)
    def _():
        xe_ref[...] = xacc[0:cap, :]
        gs_ref[...] = gacc[0:cap, :]


def _dispatch(wstart, pos_t, aff_t, h, cap, tn):
    E, n_tiles, _ = pos_t.shape
    N, D = h.shape
    win = tn + BF16_SUBLANES
    sub = 8 if n_tiles % 8 == 0 else n_tiles
    grid_spec = pltpu.PrefetchScalarGridSpec(
        num_scalar_prefetch=1,
        grid=(E, n_tiles // sub),
        in_specs=[
            pl.BlockSpec((None, sub, tn), lambda e, t, ws: (e, t, 0)),
            pl.BlockSpec((None, sub, tn), lambda e, t, ws: (e, t, 0)),
            pl.BlockSpec((sub * tn, D), lambda e, t, ws: (t, 0)),
        ],
        out_specs=[
            pl.BlockSpec((None, cap, D), lambda e, t, ws: (e, 0, 0)),
            pl.BlockSpec((None, cap, 1), lambda e, t, ws: (e, 0, 0)),
        ],
        scratch_shapes=[pltpu.VMEM((cap + win, D), BF16), pltpu.VMEM((cap + win, 1), F32)],
    )
    return pl.pallas_call(
        functools.partial(_dispatch_kernel, n_tiles=n_tiles, n_blocks=n_tiles // sub, win=win, sub=sub),
        grid_spec=grid_spec,
        out_shape=[jax.ShapeDtypeStruct((E, cap, D), BF16), jax.ShapeDtypeStruct((E, cap, 1), F32)],
        compiler_params=_cparams(("parallel", "arbitrary")),
        name="moe_dispatch",
    )(wstart, pos_t, aff_t, h)


def _ffn_kernel(x_ref, gs_ref, wg_ref, wu_ref, wd_ref, o_ref, acc, *, n_f):
    f = pl.program_id(2)

    @pl.when(f == 0)
    def _():
        acc[...] = jnp.zeros_like(acc)

    x = x_ref[...]
    hid = _silu(_dot(x, wg_ref[...])) * _dot(x, wu_ref[...])
    acc[...] += _dot(hid.astype(BF16), wd_ref[...])

    @pl.when(f == n_f - 1)
    def _():
        o_ref[...] = (acc[...] * gs_ref[...]).astype(o_ref.dtype)


def _expert_ffn(xe, gs, wg, wu, wd):
    E, C, D = xe.shape
    F = wg.shape[-1]
    tm = _tile(C, 1024)
    tf = _tile(F, 512)
    n_f = F // tf
    return pl.pallas_call(
        functools.partial(_ffn_kernel, n_f=n_f),
        grid=(E, C // tm, n_f),
        in_specs=[
            pl.BlockSpec((None, tm, D), lambda e, m, f: (e, m, 0)),
            pl.BlockSpec((None, tm, 1), lambda e, m, f: (e, m, 0)),
            pl.BlockSpec((None, D, tf), lambda e, m, f: (e, 0, f)),
            pl.BlockSpec((None, D, tf), lambda e, m, f: (e, 0, f)),
            pl.BlockSpec((None, tf, D), lambda e, m, f: (e, f, 0)),
        ],
        out_specs=pl.BlockSpec((None, tm, D), lambda e, m, f: (e, m, 0)),
        out_shape=jax.ShapeDtypeStruct((E, C, D), BF16),
        scratch_shapes=[pltpu.VMEM((tm, D), F32)],
        compiler_params=_cparams(("parallel", "parallel", "arbitrary")),
        name="expert_ffn",
    )(xe, gs, wg, wu, wd)


def _combine_kernel(ws_ref, end_ref, pos_ref, *refs, n_tiles, n_exp):
    ye_refs = refs[:n_exp]
    x_ref, g_ref, o_ref, hit_scr, y_scr = refs[n_exp:]
    t = pl.program_id(0)
    tn = pos_ref.shape[0]
    win = ye_refs[0].shape[1]
    lane = lax.broadcasted_iota(I32, (tn, tn), 1)
    for e in range(n_exp):
        prel = pos_ref[:, e:e + 1] - ws_ref[e * n_tiles + t]
        hit_scr[:, e * tn:(e + 1) * tn] = _ones_where(lane == prel)
        y_scr[e * tn:(e + 1) * tn, :] = ye_refs[e][0, :tn, :]
    o_ref[...] = x_ref[...] + g_ref[...] * _dot(hit_scr[...], y_scr[...])

    for e in range(n_exp):
        ws = ws_ref[e * n_tiles + t]

        @pl.when(end_ref[e * n_tiles + t] - ws > tn)
        def _():
            prel = pos_ref[:, e:e + 1] - ws - tn
            hit = _ones_where(lax.broadcasted_iota(I32, (tn, win - tn), 1) == prel)
            o_ref[...] += g_ref[...] * _dot(hit, ye_refs[e][0, tn:, :])


def _combine(wstart, wend, pos, ye, x, gate, tn):
    E, C, D = ye.shape
    B, T, _ = x.shape
    N = B * T
    n_tiles = N // tn
    tpb = T // tn
    win = tn + BF16_SUBLANES

    def window(e):
        return pl.BlockSpec((pl.Element(1), pl.Element(win), pl.Element(D)),
                            lambda t, ws, we: (e, pl.multiple_of(ws[e * n_tiles + t], BF16_SUBLANES), 0))

    grid_spec = pltpu.PrefetchScalarGridSpec(
        num_scalar_prefetch=2,
        grid=(n_tiles,),
        in_specs=[pl.BlockSpec((tn, E), lambda t, ws, we: (t, 0))]
        + [window(e) for e in range(E)]
        + [pl.BlockSpec((None, tn, D), lambda t, ws, we: (t // tpb, t % tpb, 0)),
           pl.BlockSpec((None, 1, D), lambda t, ws, we: (t // tpb, 0, 0))],
        out_specs=pl.BlockSpec((None, tn, D), lambda t, ws, we: (t // tpb, t % tpb, 0)),
        scratch_shapes=[pltpu.VMEM((tn, E * tn), BF16), pltpu.VMEM((E * tn, D), BF16)],
    )
    return pl.pallas_call(
        functools.partial(_combine_kernel, n_tiles=n_tiles, n_exp=E),
        grid_spec=grid_spec,
        out_shape=jax.ShapeDtypeStruct((B, T, D), F32),
        compiler_params=_cparams(("parallel",)),
        name="moe_combine",
    )(wstart, wend, pos, *([ye] * E), x, gate)


def _moe_tile(n_tok, cap):
    tn = 256
    while tn + BF16_SUBLANES > cap or n_tok % tn:
        tn //= 2
    return tn


def _moe_layer(x, g, sh, sc, gate, router, wg, wu, wd):
    B, T, D = x.shape
    N = B * T
    E = router.shape[1]
    cap = EC_CAPACITY_FACTOR * N // E
    tn = _moe_tile(T, cap)
    win = tn + BF16_SUBLANES
    h, aff = _ffn_pre(x, g, sh, sc, router)
    pos, pre = _select(aff, cap)
    start = pre[::tn].T
    aligned = (start // BF16_SUBLANES * BF16_SUBLANES).astype(I32)
    wdisp = aligned.reshape(-1)
    wstart = jnp.minimum(aligned, cap - win).reshape(-1)
    wend = jnp.concatenate([start[:, 1:], jnp.full((E, 1), cap, I32)], axis=1).reshape(-1).astype(I32)
    pos_t = pos.T.reshape(E, N // tn, tn)
    aff_t = aff.T.reshape(E, N // tn, tn)
    xe, gs = _dispatch(wdisp, pos_t, aff_t, h, cap, tn)
    ye = _expert_ffn(xe, gs, wg, wu, wd)
    return _combine(wstart, wend, pos, ye, x, gate, tn)


def _trunk(x, mod, p):
    B, T, D = x.shape
    rc = min(RET_BLOCK, T)
    cos, sin = _rotary_tables(T)
    for i in range(DEPTH):
        sh1, sc1, g1, sh2, sc2, g2 = [mod[i, :, k][:, None, :] for k in range(6)]
        j = i // 2
        if i % 2 == 0:
            proj = _proj(x, p["norm_mix_g"][i][None], sh1, sc1, p["ret_w_in"][j], rotary=(cos, sin))
            tabs = _retention_tables(p["ret_decay_logit"][j], rc)
            y_f = _retention_dir(proj, tabs, 0, None)
            y = _retention_dir(proj, tabs, 1, y_f)
            x = _out_proj(y, p["ret_w_out"][j], x, g1)
        else:
            qkv = _proj(x, p["norm_mix_g"][i][None], sh1, sc1, p["na_w_in"][j],
                        qk=(p["na_bd"], p["na_gain"][j], 2 * D))
            o = _na_attention(qkv, p["na_bias"][j])
            x = _out_proj(o, p["na_w_out"][j], x, g1)
        x = _moe_layer(x, p["norm_ffn_g"][i][None], sh2, sc2, g2, p["moe_router"][i],
                       p["moe_w_gate"][i], p["moe_w_up"][i], p["moe_w_down"][i])
    return x


def kernel(x_prompt, x_sample, c_prompt, c_sample, norm_mix_g, norm_ffn_g, ada_w, ada_b, ret_w_in, ret_decay_logit, ret_w_out, na_w_in, na_q_gain, na_k_gain, na_rpb, na_w_out, moe_router, moe_w_gate, moe_w_up, moe_w_down):
    D = D_MODEL
    bp, bs = c_prompt.shape[0], c_sample.shape[0]
    c_all = jnp.concatenate([c_prompt, c_sample], axis=0)
    pad = (-c_all.shape[0]) % 8
    if pad:
        c_all = jnp.pad(c_all, ((0, pad), (0, 0)))
    mod = _ada_mod(c_all, ada_w, ada_b).reshape(DEPTH, c_all.shape[0], 6, D)

    heads_row = lambda v: jnp.tile(v.astype(F32), (1, NA_HEADS))
    na_gain = jnp.concatenate(
        [heads_row(na_q_gain) * (NA_HEAD_DIM ** -0.5), heads_row(na_k_gain),
         jnp.ones((na_q_gain.shape[0], D), F32)], axis=-1)[:, None, :]
    bd = np.kron(np.eye(256 // NA_HEAD_DIM), np.ones((NA_HEAD_DIM, NA_HEAD_DIM)))
    p = dict(
        norm_mix_g=norm_mix_g, norm_ffn_g=norm_ffn_g,
        ret_w_in=ret_w_in.astype(BF16), ret_decay_logit=ret_decay_logit, ret_w_out=ret_w_out.astype(BF16),
        na_w_in=na_w_in.astype(BF16), na_w_out=na_w_out.astype(BF16),
        na_gain=na_gain, na_bd=jnp.asarray(bd, BF16),
        na_bias=jnp.stack([_na_bias_table(na_rpb[l]) for l in range(na_rpb.shape[0])]),
        moe_router=moe_router,
        moe_w_gate=moe_w_gate.astype(BF16), moe_w_up=moe_w_up.astype(BF16), moe_w_down=moe_w_down.astype(BF16),
    )
    y_prompt = _trunk(x_prompt, mod[:, :bp], p)
    y_sample = _trunk(x_sample, mod[:, bp:bp + bs], p)
    return (y_prompt, y_sample)
```

```python
import functools
import math

import numpy as np
import jax
import jax.numpy as jnp
from jax import lax
from jax.experimental import pallas as pl
from jax.experimental.pallas import tpu as pltpu

F32 = jnp.float32
BF16 = jnp.bfloat16
I32 = jnp.int32

D_MODEL = 1024
DEPTH = 4

RET_HEADS = 4
RET_QK_DIM = 256
RET_V_DIM = 512
RET_QK_WIDTH = RET_HEADS * RET_QK_DIM
RET_V_WIDTH = RET_HEADS * RET_V_DIM
RET_IN_COLS = 2 * RET_QK_WIDTH + 3 * RET_V_WIDTH
ROPE_BASE = 10000.0
RET_BLOCK = 256

NA_HEADS = 16
NA_HEAD_DIM = 64
NA_KH = 8
NA_KW = 16
GRID_W = 64
NA_ROWS_PER_GROUP = 4
NA_WIN_ROWS = NA_ROWS_PER_GROUP + NA_KH - 1
NA_Q = NA_ROWS_PER_GROUP * GRID_W
NA_K = NA_WIN_ROWS * GRID_W

N_EXPERTS = 16
EC_CAPACITY_FACTOR = 2
D_FF = 2 * D_MODEL

EPS = 1e-6
NEG_INF = -1e30

LANES = 128
BF16_SUBLANES = 16
VMEM_LIMIT = 56 * 1024 * 1024

HIGHEST = lax.Precision.HIGHEST


def _cparams(sem):
    return pltpu.CompilerParams(dimension_semantics=sem, vmem_limit_bytes=VMEM_LIMIT)


def _tile(n, pref):
    t = min(pref, n)
    while n % t:
        t //= 2
    return t


def _silu(x):
    return x * (1.0 / (1.0 + jnp.exp(-x)))


def _ones_where(mask):
    return jnp.where(mask, 1.0, 0.0).astype(BF16)


def _dot(a, b):
    return jnp.dot(a, b, preferred_element_type=F32)


def _dot_nt(a, b):
    return lax.dot_general(a, b, (((1,), (1,)), ((), ())), preferred_element_type=F32)


def _dot_tn(a, b):
    return lax.dot_general(a, b, (((0,), (0,)), ((), ())), preferred_element_type=F32)


def _ada_kernel(c_ref, w_ref, b_ref, o_ref):
    ca = _silu(c_ref[...])
    o_ref[...] = jnp.dot(ca, w_ref[...], preferred_element_type=F32, precision=HIGHEST) + b_ref[...]


def _ada_mod(c_all, ada_w, ada_b):
    R, D = c_all.shape
    n_out = ada_w.shape[-1]
    tn = min(1536, n_out)
    return pl.pallas_call(
        _ada_kernel,
        grid=(DEPTH, n_out // tn),
        in_specs=[
            pl.BlockSpec((R, D), lambda l, j: (0, 0)),
            pl.BlockSpec((None, D, tn), lambda l, j: (l, 0, j)),
            pl.BlockSpec((None, 1, tn), lambda l, j: (l, 0, j)),
        ],
        out_specs=pl.BlockSpec((None, R, tn), lambda l, j: (l, 0, j)),
        out_shape=jax.ShapeDtypeStruct((DEPTH, R, n_out), F32),
        compiler_params=_cparams(("arbitrary", "arbitrary")),
        name="ada_mod",
    )(c_all, ada_w, ada_b.reshape(DEPTH, 1, n_out))


def _norm_mod(x, g, sh, sc):
    ms = jnp.mean(x * x, axis=-1, keepdims=True)
    y = x * lax.rsqrt(ms + EPS) * g
    return y * (1.0 + sc) + sh


def _proj_kernel(x_ref, g_ref, sh_ref, sc_ref, w_ref, o_ref, h_scr):
    @pl.when(pl.program_id(2) == 0)
    def _():
        h_scr[...] = _norm_mod(x_ref[...], g_ref[...], sh_ref[...], sc_ref[...]).astype(BF16)

    o_ref[...] = _dot(h_scr[...], w_ref[...]).astype(o_ref.dtype)


def _proj_qknorm_kernel(x_ref, g_ref, sh_ref, sc_ref, w_ref, bd_ref, gain_ref, o_ref, h_scr, *, qk_blocks, tn):
    j = pl.program_id(2)

    @pl.when(j == 0)
    def _():
        h_scr[...] = _norm_mod(x_ref[...], g_ref[...], sh_ref[...], sc_ref[...]).astype(BF16)

    acc = _dot(h_scr[...], w_ref[...])

    @pl.when(j < qk_blocks)
    def _():
        for c in range(tn // 256):
            a = acc[:, c * 256:(c + 1) * 256]
            ss = _dot((a * a).astype(BF16), bd_ref[...])
            r = lax.rsqrt(ss * (1.0 / NA_HEAD_DIM) + EPS)
            o_ref[:, c * 256:(c + 1) * 256] = (a * r * gain_ref[:, c * 256:(c + 1) * 256]).astype(o_ref.dtype)

    @pl.when(j >= qk_blocks)
    def _():
        o_ref[...] = acc.astype(o_ref.dtype)


def _proj_rotary_kernel(x_ref, g_ref, sh_ref, sc_ref, w_ref, cos_ref, sin_ref, o_ref, h_scr, *, tn):
    j = pl.program_id(2)

    @pl.when(j == 0)
    def _():
        h_scr[...] = _norm_mod(x_ref[...], g_ref[...], sh_ref[...], sc_ref[...]).astype(BF16)

    acc = _dot(h_scr[...], w_ref[...])
    qk_blocks = 2 * RET_QK_WIDTH // tn
    half = RET_QK_DIM // 2

    @pl.when(j < qk_blocks)
    def _():
        scale = jnp.where(j < qk_blocks // 2, 1.0, RET_QK_DIM ** -0.5)
        cos = cos_ref[...] * scale
        sin = sin_ref[...] * scale
        for c in range(tn // RET_QK_DIM):
            a1 = acc[:, c * RET_QK_DIM:c * RET_QK_DIM + half]
            a2 = acc[:, c * RET_QK_DIM + half:(c + 1) * RET_QK_DIM]
            o_ref[:, c * RET_QK_DIM:c * RET_QK_DIM + half] = (a1 * cos - a2 * sin).astype(o_ref.dtype)
            o_ref[:, c * RET_QK_DIM + half:(c + 1) * RET_QK_DIM] = (a1 * sin + a2 * cos).astype(o_ref.dtype)

    @pl.when(j >= qk_blocks)
    def _():
        o_ref[...] = acc.astype(o_ref.dtype)


def _proj(x, g, sh, sc, w, qk=None, rotary=None):
    B, T, D = x.shape
    n_out = w.shape[1]
    tm = _tile(T, 1024)
    tn = _tile(n_out, 1024)
    grid = (B, T // tm, n_out // tn)
    in_specs = [
        pl.BlockSpec((None, tm, D), lambda b, i, j: (b, i, 0)),
        pl.BlockSpec((1, D), lambda b, i, j: (0, 0)),
        pl.BlockSpec((None, 1, D), lambda b, i, j: (b, 0, 0)),
        pl.BlockSpec((None, 1, D), lambda b, i, j: (b, 0, 0)),
        pl.BlockSpec((D, tn), lambda b, i, j: (0, j)),
    ]
    args = [x, g, sh, sc, w]
    if rotary is not None:
        half = RET_QK_DIM // 2
        in_specs += [pl.BlockSpec((tm, half), lambda b, i, j: (i, 0))] * 2
        args += list(rotary)
        body = functools.partial(_proj_rotary_kernel, tn=tn)
        name = "proj_rotary"
    elif qk is None:
        body = _proj_kernel
        name = "proj"
    else:
        bd, gain, n_qk = qk
        in_specs += [
            pl.BlockSpec((256, 256), lambda b, i, j: (0, 0)),
            pl.BlockSpec((1, tn), lambda b, i, j: (0, j)),
        ]
        args += [bd, gain]
        body = functools.partial(_proj_qknorm_kernel, qk_blocks=n_qk // tn, tn=tn)
        name = "proj_qknorm"
    return pl.pallas_call(
        body,
        grid=grid,
        in_specs=in_specs,
        out_specs=pl.BlockSpec((None, tm, tn), lambda b, i, j: (b, i, j)),
        out_shape=jax.ShapeDtypeStruct((B, T, n_out), BF16),
        scratch_shapes=[pltpu.VMEM((tm, D), BF16)],
        compiler_params=_cparams(("parallel", "parallel", "arbitrary")),
        name=name,
    )(*args)


def _outproj_kernel(y_ref, w_ref, x_ref, g_ref, o_ref):
    o_ref[...] = x_ref[...] + g_ref[...] * _dot(y_ref[...], w_ref[...])


def _out_proj(y, w, x, gate):
    B, T, K = y.shape
    D = w.shape[1]
    tm = _tile(T, 512)
    return pl.pallas_call(
        _outproj_kernel,
        grid=(B, T // tm),
        in_specs=[
            pl.BlockSpec((None, tm, K), lambda b, i: (b, i, 0)),
            pl.BlockSpec((K, D), lambda b, i: (0, 0)),
            pl.BlockSpec((None, tm, D), lambda b, i: (b, i, 0)),
            pl.BlockSpec((None, 1, D), lambda b, i: (b, 0, 0)),
        ],
        out_specs=pl.BlockSpec((None, tm, D), lambda b, i: (b, i, 0)),
        out_shape=jax.ShapeDtypeStruct((B, T, D), F32),
        compiler_params=_cparams(("parallel", "parallel")),
        name="out_proj",
    )(y, w, x, gate)


def _retention_kernel(*refs, has_prev):
    if has_prev:
        q_ref, k_ref, v_ref, gate_ref, intra_ref, qdec_ref, kdec_ref, cdec_ref, prev_ref, o_ref, state = refs
    else:
        q_ref, k_ref, v_ref, gate_ref, intra_ref, qdec_ref, kdec_ref, cdec_ref, o_ref, state = refs
        prev_ref = None

    @pl.when(pl.program_id(1) == 0)
    def _():
        state[...] = jnp.zeros_like(state)

    for h in range(RET_HEADS):
        qk = slice(h * RET_QK_DIM, (h + 1) * RET_QK_DIM)
        vs = slice(h * RET_V_DIM, (h + 1) * RET_V_DIM)
        qb = q_ref[:, qk]
        kb = k_ref[:, qk]
        v = v_ref[:, vs]
        scores = _dot_nt(qb, kb) * intra_ref[h]
        st = state[h]
        o = _dot(scores.astype(BF16), v) + _dot(qb, st.astype(BF16)) * qdec_ref[h]
        state[h] = st * cdec_ref[h] + _dot_tn((kb.astype(F32) * kdec_ref[h]).astype(BF16), v)

        mu = jnp.mean(o, axis=-1, keepdims=True)
        oc = o - mu
        var = jnp.mean(oc * oc, axis=-1, keepdims=True)
        y = oc * lax.rsqrt(var + EPS) * _silu(gate_ref[:, vs].astype(F32))
        if has_prev:
            y = y + prev_ref[:, vs].astype(F32)
        o_ref[:, vs] = y.astype(o_ref.dtype)


def _retention_dir(proj, tabs, direction, prev):
    B, T, _ = proj.shape
    rc = min(RET_BLOCK, T)
    nc = T // rc
    intra, qdec, kdec, cdec = tabs
    if direction == 0:
        cidx = lambda c: c
    else:
        cidx = lambda c: nc - 1 - c
    H = RET_HEADS
    v_blk = 2 * RET_QK_WIDTH // RET_V_WIDTH
    in_specs = [
        pl.BlockSpec((None, rc, RET_QK_WIDTH), lambda b, c: (b, cidx(c), 0)),
        pl.BlockSpec((None, rc, RET_QK_WIDTH), lambda b, c: (b, cidx(c), 1)),
        pl.BlockSpec((None, rc, RET_V_WIDTH), lambda b, c: (b, cidx(c), v_blk)),
        pl.BlockSpec((None, rc, RET_V_WIDTH), lambda b, c: (b, cidx(c), v_blk + 1 + direction)),
        pl.BlockSpec((H, rc, rc), lambda b, c: (direction, 0, 0)),
        pl.BlockSpec((H, rc, 1), lambda b, c: (direction, 0, 0)),
        pl.BlockSpec((H, rc, 1), lambda b, c: (direction, 0, 0)),
        pl.BlockSpec((H, 1, 1), lambda b, c: (direction, 0, 0)),
    ]
    args = [proj, proj, proj, proj, intra, qdec, kdec, cdec]
    if prev is not None:
        in_specs.append(pl.BlockSpec((None, rc, RET_V_WIDTH), lambda b, c: (b, cidx(c), 0)))
        args.append(prev)
    return pl.pallas_call(
        functools.partial(_retention_kernel, has_prev=prev is not None),
        grid=(B, nc),
        in_specs=in_specs,
        out_specs=pl.BlockSpec((None, rc, RET_V_WIDTH), lambda b, c: (b, cidx(c), 0)),
        out_shape=jax.ShapeDtypeStruct((B, T, RET_V_WIDTH), BF16),
        scratch_shapes=[pltpu.VMEM((H, RET_QK_DIM, RET_V_DIM), F32)],
        compiler_params=_cparams(("parallel", "arbitrary")),
        name="retention_bwd" if direction else "retention_fwd",
    )(*args)


def _retention_tables(decay_logit, rc):
    lg = jax.nn.log_sigmoid(decay_logit.astype(F32).reshape(-1))
    pos = jnp.arange(rc, dtype=F32)
    rel = pos[:, None] - pos[None, :]
    l3 = lg[:, None, None]
    fwd = jnp.where(rel >= 0, jnp.exp(jnp.maximum(rel, 0.0) * l3), 0.0)
    bwd = jnp.where(rel <= 0, jnp.exp(jnp.maximum(-rel, 0.0) * l3), 0.0)
    is_bwd = (jnp.arange(2 * RET_HEADS) >= RET_HEADS)
    intra = jnp.where(is_bwd[:, None, None], bwd, fwd)
    q_pow = jnp.where(is_bwd[:, None], rc - pos[None, :], pos[None, :] + 1.0)
    k_pow = jnp.where(is_bwd[:, None], pos[None, :], rc - 1.0 - pos[None, :])
    qdec = jnp.exp(q_pow * lg[:, None])[..., None]
    kdec = jnp.exp(k_pow * lg[:, None])[..., None]
    cdec = jnp.exp(rc * lg)[:, None, None]
    return intra, qdec, kdec, cdec


def _rotary_tables(T):
    d = RET_QK_DIM
    inv = 1.0 / (ROPE_BASE ** (jnp.arange(0, d, 2, dtype=F32) / d))
    ang = jnp.arange(T, dtype=F32)[:, None] * inv[None, :]
    return jnp.cos(ang), jnp.sin(ang)


def _na_kernel(q_ref, k_ref, v_ref, bias_ref, o_ref, *, rows):
    n_groups = rows // NA_ROWS_PER_GROUP
    lane = lax.broadcasted_iota(I32, (NA_Q, LANES), 1)
    first = lane < NA_HEAD_DIM

    def group(gi, carry):
        ustart = jnp.clip(gi * NA_ROWS_PER_GROUP - NA_KH // 2, 0, rows - NA_WIN_ROWS)
        pat = jnp.where(gi == 0, 0, jnp.where(gi == n_groups - 1, 2, 1))
        kstart = pl.multiple_of(ustart * GRID_W, GRID_W)
        qstart = pl.multiple_of(gi * NA_Q, NA_Q)
        kw = k_ref[pl.ds(kstart, NA_K), :]
        vw = v_ref[pl.ds(kstart, NA_K), :]
        q = q_ref[pl.ds(qstart, NA_Q), :].astype(F32)
        q2 = jnp.concatenate([jnp.where(first, q, 0.0), jnp.where(first, 0.0, q)], axis=0).astype(BF16)
        s = _dot_nt(q2, kw) + bias_ref[pat].reshape(2 * NA_Q, NA_K)
        m = jnp.max(s, axis=-1, keepdims=True)
        e = jnp.exp(s - m)
        l = jnp.sum(e, axis=-1, keepdims=True)
        o2 = _dot(e.astype(BF16), vw) / l
        o_ref[pl.ds(qstart, NA_Q), :] = jnp.where(first, o2[:NA_Q], o2[NA_Q:]).astype(o_ref.dtype)
        return carry

    lax.fori_loop(0, n_groups, group, 0, unroll=2)


def _na_attention(qkv, bias):
    B, T, _ = qkv.shape
    rows = T // GRID_W
    assert rows % NA_ROWS_PER_GROUP == 0 and rows >= NA_WIN_ROWS + 1
    n_pairs = D_MODEL // LANES
    blk = lambda off: pl.BlockSpec((None, T, LANES), lambda p, b: (b, 0, off + p))
    return pl.pallas_call(
        functools.partial(_na_kernel, rows=rows),
        grid=(n_pairs, B),
        in_specs=[
            blk(0), blk(n_pairs), blk(2 * n_pairs),
            pl.BlockSpec((3, 2, NA_Q, NA_K), lambda p, b: (0, p, 0, 0)),
        ],
        out_specs=pl.BlockSpec((None, T, LANES), lambda p, b: (b, 0, p)),
        out_shape=jax.ShapeDtypeStruct((B, T, D_MODEL), BF16),
        compiler_params=_cparams(("parallel", "parallel")),
        name="na_attention",
    )(qkv, qkv, qkv, bias)


def _na_bias_table(rpb):
    H = rpb.shape[0]
    r = rpb.astype(F32)
    pad = GRID_W
    rp = jnp.concatenate([jnp.repeat(r[..., :1], pad, -1), r, jnp.repeat(r[..., -1:], pad, -1)], -1)
    base = NA_KW - 1 + pad
    t1 = jnp.stack([rp[..., base - qc: base - qc + GRID_W] for qc in range(GRID_W)], axis=2)
    qc = np.arange(GRID_W)[:, None]
    kc = np.arange(GRID_W)[None, :]
    ws = np.clip(qc - NA_KW // 2, 0, GRID_W - NA_KW)
    t1 = jnp.where(jnp.asarray((kc >= ws) & (kc < ws + NA_KW)), t1, NEG_INF)
    masked = jnp.full((H, GRID_W, GRID_W), NEG_INF, F32)
    pats = []
    for p in range(3):
        per_row = []
        for i in range(NA_ROWS_PER_GROUP):
            lo = (0, i, NA_ROWS_PER_GROUP - 1)[p]
            qrow = (i, i + NA_KH // 2, i + NA_KH - 1)[p]
            us = [t1[:, u - qrow + NA_KH - 1] if lo <= u < lo + NA_KH else masked for u in range(NA_WIN_ROWS)]
            per_row.append(jnp.stack(us, axis=2))
        pats.append(jnp.stack(per_row, axis=1))
    return jnp.stack(pats, axis=0).reshape(3, H, NA_Q, NA_K)


def _ffn_pre_kernel(x_ref, g_ref, sh_ref, sc_ref, r_ref, h_ref, aff_ref):
    h = _norm_mod(x_ref[...], g_ref[...], sh_ref[...], sc_ref[...])
    h_ref[...] = h.astype(BF16)
    logits = jnp.dot(h, r_ref[...], preferred_element_type=F32, precision=HIGHEST)
    m = jnp.max(logits, axis=-1, keepdims=True)
    e = jnp.exp(logits - m)
    aff_ref[...] = e / jnp.sum(e, axis=-1, keepdims=True)


def _ffn_pre(x, g, sh, sc, router):
    B, T, D = x.shape
    tm = _tile(T, 1024)
    nt = T // tm
    E = router.shape[1]
    return pl.pallas_call(
        _ffn_pre_kernel,
        grid=(B, nt),
        in_specs=[
            pl.BlockSpec((None, tm, D), lambda b, i: (b, i, 0)),
            pl.BlockSpec((1, D), lambda b, i: (0, 0)),
            pl.BlockSpec((None, 1, D), lambda b, i: (b, 0, 0)),
            pl.BlockSpec((None, 1, D), lambda b, i: (b, 0, 0)),
            pl.BlockSpec((D, E), lambda b, i: (0, 0)),
        ],
        out_specs=[
            pl.BlockSpec((tm, D), lambda b, i: (b * nt + i, 0)),
            pl.BlockSpec((tm, E), lambda b, i: (b * nt + i, 0)),
        ],
        out_shape=[jax.ShapeDtypeStruct((B * T, D), BF16), jax.ShapeDtypeStruct((B * T, E), F32)],
        compiler_params=_cparams(("parallel", "parallel")),
        name="ffn_pre",
    )(x, g, sh, sc, router)


def _select_kernel(aff_ref, pos_ref, pre_ref, *, cap, n_tok, rb):
    E = N_EXPERTS
    per_row = LANES // E
    rows = n_tok // per_row
    bits = pltpu.bitcast(aff_ref[...], I32)
    lane = lax.broadcasted_iota(I32, (rows, LANES), 1)
    tok = lax.broadcasted_iota(I32, (rows, LANES), 0) * per_row + lane // E

    def count(pred):
        c = jnp.sum(pred.astype(I32), axis=0, keepdims=True)
        c = jnp.broadcast_to(c, (8, LANES))
        sh = E
        while sh < LANES:
            c = c + pltpu.roll(c, sh, 1)
            sh *= 2
        return c[0:1]

    def thr_step(i, thr):
        cand = thr | jnp.left_shift(jnp.int32(1), 30 - i)
        return jnp.where(count(bits >= cand) >= cap, cand, thr)

    thr = lax.fori_loop(0, 31, thr_step, jnp.zeros((1, LANES), I32))
    gt = bits > thr
    eq = bits == thr
    need = cap - count(gt)

    nbits = max(1, int(math.ceil(math.log2(n_tok))))

    def tie_step(i, jmax):
        cand = jmax + jnp.left_shift(jnp.int32(1), nbits - 1 - i)
        return jnp.where(count(jnp.logical_and(eq, tok < cand)) < need, cand, jmax)

    jmax = lax.fori_loop(0, nbits, tie_step, jnp.zeros((1, LANES), I32))

    li = lax.broadcasted_iota(I32, (LANES, LANES), 0)
    lj = lax.broadcasted_iota(I32, (LANES, LANES), 1)
    same_e = (li % E) == (lj % E)
    a_before = _ones_where(jnp.logical_and(same_e, li // E < lj // E))
    a_all = _ones_where(same_e)
    ri = lax.broadcasted_iota(I32, (rb, rb), 0)
    rj = lax.broadcasted_iota(I32, (rb, rb), 1)
    lower = _ones_where(rj < ri)

    running = jnp.zeros((1, LANES), F32)
    for blk in range(rows // rb):
        sl = slice(blk * rb, (blk + 1) * rb)
        b_blk = pltpu.bitcast(aff_ref[sl, :], I32)
        t_blk = ((lax.broadcasted_iota(I32, (rb, LANES), 0) + blk * rb) * per_row
                 + lax.broadcasted_iota(I32, (rb, LANES), 1) // E)
        keep = jnp.logical_or(b_blk > thr, jnp.logical_and(b_blk == thr, t_blk <= jmax))
        kb = _ones_where(keep)
        row_tot = _dot(kb, a_all)
        within = _dot(lower, row_tot.astype(BF16))
        pre = running + within + _dot(kb, a_before)
        running = running + jnp.sum(row_tot, axis=0, keepdims=True)
        pre_i = pre.astype(I32)
        pre_ref[sl, :] = pre_i
        pos_ref[sl, :] = jnp.where(keep, pre_i, -1)


def _select(aff, cap):
    n_tok, E = aff.shape
    rows = n_tok * E // LANES
    rb = min(256, rows)
    aff8 = aff.reshape(rows, LANES)
    pos, pre = pl.pallas_call(
        functools.partial(_select_kernel, cap=cap, n_tok=n_tok, rb=rb),
        out_shape=[jax.ShapeDtypeStruct((rows, LANES), I32)] * 2,
        compiler_params=pltpu.CompilerParams(vmem_limit_bytes=VMEM_LIMIT),
        name="expert_select",
    )(aff8)
    return pos.reshape(n_tok, E), pre.reshape(n_tok, E)


DISPATCH_GROUP = 2


def _dispatch_kernel(ws_ref, pos_ref, aff_ref, h_ref, xe_ref, gs_ref, *, n_tiles, win, sub, cap):
    eg = pl.program_id(0)
    tb = pl.program_id(1)
    group, _, tn = pos_ref.shape
    head = BF16_SUBLANES

    @pl.when(tb == 0)
    def _():
        for k in range(group):
            xe_ref[k, 0:head, :] = jnp.zeros((head, xe_ref.shape[2]), xe_ref.dtype)
            gs_ref[k, 0:head, :] = jnp.zeros((head, 1), gs_ref.dtype)
            xe_ref[k, cap:, :] = jnp.zeros((win, xe_ref.shape[2]), xe_ref.dtype)
            gs_ref[k, cap:, :] = jnp.zeros((win, 1), gs_ref.dtype)

    def tile(s, carry):
        hs = h_ref[pl.ds(pl.multiple_of(s * tn, tn), tn), :]
        for k in range(group):
            ws = pl.multiple_of(ws_ref[(eg * group + k) * n_tiles + tb * sub + s], head)
            prel = pos_ref[k, pl.ds(s, 1), :] - ws
            hit = lax.broadcasted_iota(I32, (win, tn), 0) == prel
            rows = _dot(_ones_where(hit), hs)
            g = jnp.sum(jnp.where(hit, aff_ref[k, pl.ds(s, 1), :], 0.0), axis=1, keepdims=True)
            xe_ref[k, pl.ds(ws, head), :] = (
                xe_ref[k, pl.ds(ws, head), :].astype(F32) + rows[:head]).astype(xe_ref.dtype)
            gs_ref[k, pl.ds(ws, head), :] = gs_ref[k, pl.ds(ws, head), :] + g[:head]
            rest = pl.multiple_of(ws + head, head)
            xe_ref[k, pl.ds(rest, win - head), :] = rows[head:].astype(xe_ref.dtype)
            gs_ref[k, pl.ds(rest, win - head), :] = g[head:]
        return carry

    lax.fori_loop(0, sub, tile, 0, unroll=2)


def _dispatch(wstart, pos_t, aff_t, h, cap, tn):
    E, n_tiles, _ = pos_t.shape
    N, D = h.shape
    win = tn + BF16_SUBLANES
    sub = 8 if n_tiles % 8 == 0 else n_tiles
    group = DISPATCH_GROUP
    resident = dict(pipeline_mode=pl.Buffered(1))
    grid_spec = pltpu.PrefetchScalarGridSpec(
        num_scalar_prefetch=1,
        grid=(E // group, n_tiles // sub),
        in_specs=[
            pl.BlockSpec((group, sub, tn), lambda e, t, ws: (e, t, 0)),
            pl.BlockSpec((group, sub, tn), lambda e, t, ws: (e, t, 0)),
            pl.BlockSpec((sub * tn, D), lambda e, t, ws: (t, 0)),
        ],
        out_specs=[
            pl.BlockSpec((group, cap + win, D), lambda e, t, ws: (e, 0, 0), **resident),
            pl.BlockSpec((group, cap + win, 1), lambda e, t, ws: (e, 0, 0), **resident),
        ],
    )
    return pl.pallas_call(
        functools.partial(_dispatch_kernel, n_tiles=n_tiles, win=win, sub=sub, cap=cap),
        grid_spec=grid_spec,
        out_shape=[jax.ShapeDtypeStruct((E, cap + win, D), BF16), jax.ShapeDtypeStruct((E, cap + win, 1), F32)],
        compiler_params=_cparams(("parallel", "arbitrary")),
        name="moe_dispatch",
    )(wstart, pos_t, aff_t, h)


def _ffn_kernel(x_ref, gs_ref, wg_ref, wu_ref, wd_ref, o_ref, acc, *, n_f):
    f = pl.program_id(2)

    @pl.when(f == 0)
    def _():
        acc[...] = jnp.zeros_like(acc)

    x = x_ref[...]
    hid = _silu(_dot(x, wg_ref[...].astype(BF16))) * _dot(x, wu_ref[...].astype(BF16))
    acc[...] += _dot(hid.astype(BF16), wd_ref[...].astype(BF16))

    @pl.when(f == n_f - 1)
    def _():
        o_ref[...] = (acc[...] * gs_ref[...]).astype(o_ref.dtype)


def _expert_ffn(xe, gs, wg, wu, wd, layer, C):
    E, _, D = xe.shape
    F = wg.shape[-1]
    tm = _tile(C, 2048)
    tf = _tile(F, 512)
    n_f = F // tf
    return pl.pallas_call(
        functools.partial(_ffn_kernel, n_f=n_f),
        grid=(E, C // tm, n_f),
        in_specs=[
            pl.BlockSpec((None, tm, D), lambda e, m, f: (e, m, 0)),
            pl.BlockSpec((None, tm, 1), lambda e, m, f: (e, m, 0)),
            pl.BlockSpec((None, None, D, tf), lambda e, m, f: (layer, e, 0, f)),
            pl.BlockSpec((None, None, D, tf), lambda e, m, f: (layer, e, 0, f)),
            pl.BlockSpec((None, None, tf, D), lambda e, m, f: (layer, e, f, 0)),
        ],
        out_specs=pl.BlockSpec((None, tm, D), lambda e, m, f: (e, m, 0)),
        out_shape=jax.ShapeDtypeStruct((E, C, D), BF16),
        scratch_shapes=[pltpu.VMEM((tm, D), F32)],
        compiler_params=_cparams(("parallel", "parallel", "arbitrary")),
        name="expert_ffn",
    )(xe, gs, wg, wu, wd)


def _combine_kernel(ws_ref, end_ref, pos_ref, *refs, n_tiles, n_exp):
    ye_refs = refs[:n_exp]
    x_ref, g_ref, o_ref, hit_scr, y_scr = refs[n_exp:]
    t = pl.program_id(0)
    tn = pos_ref.shape[0]
    win = ye_refs[0].shape[1]
    lane = lax.broadcasted_iota(I32, (tn, tn), 1)
    for e in range(n_exp):
        prel = pos_ref[:, e:e + 1] - ws_ref[e * n_tiles + t]
        hit_scr[:, e * tn:(e + 1) * tn] = _ones_where(lane == prel)
        y_scr[e * tn:(e + 1) * tn, :] = ye_refs[e][0, :tn, :]
    o_ref[...] = x_ref[...] + g_ref[...] * _dot(hit_scr[...], y_scr[...])

    for e in range(n_exp):
        ws = ws_ref[e * n_tiles + t]

        @pl.when(end_ref[e * n_tiles + t] - ws > tn)
        def _():
            prel = pos_ref[:, e:e + 1] - ws - tn
            hit = _ones_where(lax.broadcasted_iota(I32, (tn, win - tn), 1) == prel)
            o_ref[...] += g_ref[...] * _dot(hit, ye_refs[e][0, tn:, :])


def _combine(wstart, wend, pos, ye, x, gate, tn):
    E, C, D = ye.shape
    B, T, _ = x.shape
    N = B * T
    n_tiles = N // tn
    tpb = T // tn
    win = tn + BF16_SUBLANES

    def window(e):
        return pl.BlockSpec((pl.Element(1), pl.Element(win), pl.Element(D)),
                            lambda t, ws, we: (e, pl.multiple_of(ws[e * n_tiles + t], BF16_SUBLANES), 0))

    grid_spec = pltpu.PrefetchScalarGridSpec(
        num_scalar_prefetch=2,
        grid=(n_tiles,),
        in_specs=[pl.BlockSpec((tn, E), lambda t, ws, we: (t, 0))]
        + [window(e) for e in range(E)]
        + [pl.BlockSpec((None, tn, D), lambda t, ws, we: (t // tpb, t % tpb, 0)),
           pl.BlockSpec((None, 1, D), lambda t, ws, we: (t // tpb, 0, 0))],
        out_specs=pl.BlockSpec((None, tn, D), lambda t, ws, we: (t // tpb, t % tpb, 0)),
        scratch_shapes=[pltpu.VMEM((tn, E * tn), BF16), pltpu.VMEM((E * tn, D), BF16)],
    )
    return pl.pallas_call(
        functools.partial(_combine_kernel, n_tiles=n_tiles, n_exp=E),
        grid_spec=grid_spec,
        out_shape=jax.ShapeDtypeStruct((B, T, D), F32),
        compiler_params=_cparams(("parallel",)),
        name="moe_combine",
    )(wstart, wend, pos, *([ye] * E), x, gate)


def _moe_tile(n_tok, cap):
    tn = 256
    while tn + BF16_SUBLANES > cap or n_tok % tn:
        tn //= 2
    return tn


def _moe_layer(x, g, sh, sc, gate, router, wg, wu, wd, layer):
    B, T, D = x.shape
    N = B * T
    E = router.shape[1]
    cap = EC_CAPACITY_FACTOR * N // E
    tn = _moe_tile(T, cap)
    win = tn + BF16_SUBLANES
    h, aff = _ffn_pre(x, g, sh, sc, router)
    pos, pre = _select(aff, cap)
    start = pre[::tn].T
    aligned = (start // BF16_SUBLANES * BF16_SUBLANES).astype(I32)
    wdisp = aligned.reshape(-1)
    wstart = jnp.minimum(aligned, cap - win).reshape(-1)
    wend = jnp.concatenate([start[:, 1:], jnp.full((E, 1), cap, I32)], axis=1).reshape(-1).astype(I32)
    pos_t = pos.T.reshape(E, N // tn, tn)
    aff_t = aff.T.reshape(E, N // tn, tn)
    xe, gs = _dispatch(wdisp, pos_t, aff_t, h, cap, tn)
    ye = _expert_ffn(xe, gs, wg, wu, wd, layer, cap)
    return _combine(wstart, wend, pos, ye, x, gate, tn)


def _trunk(x, mod, p):
    B, T, D = x.shape
    rc = min(RET_BLOCK, T)
    cos, sin = _rotary_tables(T)
    for i in range(DEPTH):
        sh1, sc1, g1, sh2, sc2, g2 = [mod[i, :, k][:, None, :] for k in range(6)]
        j = i // 2
        if i % 2 == 0:
            proj = _proj(x, p["norm_mix_g"][i][None], sh1, sc1, p["ret_w_in"][j], rotary=(cos, sin))
            tabs = _retention_tables(p["ret_decay_logit"][j], rc)
            y_f = _retention_dir(proj, tabs, 0, None)
            y = _retention_dir(proj, tabs, 1, y_f)
            x = _out_proj(y, p["ret_w_out"][j], x, g1)
        else:
            qkv = _proj(x, p["norm_mix_g"][i][None], sh1, sc1, p["na_w_in"][j],
                        qk=(p["na_bd"], p["na_gain"][j], 2 * D))
            o = _na_attention(qkv, p["na_bias"][j])
            x = _out_proj(o, p["na_w_out"][j], x, g1)
        x = _moe_layer(x, p["norm_ffn_g"][i][None], sh2, sc2, g2, p["moe_router"][i],
                       p["moe_w_gate"], p["moe_w_up"], p["moe_w_down"], i)
    return x


def kernel(x_prompt, x_sample, c_prompt, c_sample, norm_mix_g, norm_ffn_g, ada_w, ada_b, ret_w_in, ret_decay_logit, ret_w_out, na_w_in, na_q_gain, na_k_gain, na_rpb, na_w_out, moe_router, moe_w_gate, moe_w_up, moe_w_down):
    D = D_MODEL
    bp, bs = c_prompt.shape[0], c_sample.shape[0]
    c_all = jnp.concatenate([c_prompt, c_sample], axis=0)
    pad = (-c_all.shape[0]) % 8
    if pad:
        c_all = jnp.pad(c_all, ((0, pad), (0, 0)))
    mod = _ada_mod(c_all, ada_w, ada_b).reshape(DEPTH, c_all.shape[0], 6, D)

    heads_row = lambda v: jnp.tile(v.astype(F32), (1, NA_HEADS))
    na_gain = jnp.concatenate(
        [heads_row(na_q_gain) * (NA_HEAD_DIM ** -0.5), heads_row(na_k_gain),
         jnp.ones((na_q_gain.shape[0], D), F32)], axis=-1)[:, None, :]
    bd = np.kron(np.eye(256 // NA_HEAD_DIM), np.ones((NA_HEAD_DIM, NA_HEAD_DIM)))
    p = dict(
        norm_mix_g=norm_mix_g, norm_ffn_g=norm_ffn_g,
        ret_w_in=ret_w_in.astype(BF16), ret_decay_logit=ret_decay_logit, ret_w_out=ret_w_out.astype(BF16),
        na_w_in=na_w_in.astype(BF16), na_w_out=na_w_out.astype(BF16),
        na_gain=na_gain, na_bd=jnp.asarray(bd, BF16),
        na_bias=jnp.stack([_na_bias_table(na_rpb[l]) for l in range(na_rpb.shape[0])]),
        moe_router=moe_router,
        moe_w_gate=moe_w_gate, moe_w_up=moe_w_up, moe_w_down=moe_w_down,
    )
    y_prompt = _trunk(x_prompt, mod[:, :bp], p)
    y_sample = _trunk(x_sample, mod[:, bp:bp + bs], p)
    return (y_prompt, y_sample)
```

```python
import functools
import math

import numpy as np
import jax
import jax.numpy as jnp
from jax import lax
from jax.experimental import pallas as pl
from jax.experimental.pallas import tpu as pltpu

F32 = jnp.float32
BF16 = jnp.bfloat16
I32 = jnp.int32

D_MODEL = 1024
DEPTH = 4

RET_HEADS = 4
RET_QK_DIM = 256
RET_V_DIM = 512
RET_QK_WIDTH = RET_HEADS * RET_QK_DIM
RET_V_WIDTH = RET_HEADS * RET_V_DIM
RET_IN_COLS = 2 * RET_QK_WIDTH + 3 * RET_V_WIDTH
ROPE_BASE = 10000.0
RET_BLOCK = 256

NA_HEADS = 16
NA_HEAD_DIM = 64
NA_KH = 8
NA_KW = 16
GRID_W = 64
NA_ROWS_PER_GROUP = 4
NA_WIN_ROWS = NA_ROWS_PER_GROUP + NA_KH - 1
NA_Q = NA_ROWS_PER_GROUP * GRID_W
NA_K = NA_WIN_ROWS * GRID_W

N_EXPERTS = 16
EC_CAPACITY_FACTOR = 2
D_FF = 2 * D_MODEL

EPS = 1e-6
NEG_INF = -1e30

LANES = 128
BF16_SUBLANES = 16
VMEM_LIMIT = 56 * 1024 * 1024

HIGHEST = lax.Precision.HIGHEST


def _cparams(sem):
    return pltpu.CompilerParams(dimension_semantics=sem, vmem_limit_bytes=VMEM_LIMIT)


def _tile(n, pref):
    t = min(pref, n)
    while n % t:
        t //= 2
    return t


def _silu(x):
    return x * (1.0 / (1.0 + jnp.exp(-x)))


def _ones_where(mask):
    return jnp.where(mask, 1.0, 0.0).astype(BF16)


def _dot(a, b):
    return jnp.dot(a, b, preferred_element_type=F32)


def _dot_nt(a, b):
    return lax.dot_general(a, b, (((1,), (1,)), ((), ())), preferred_element_type=F32)


def _dot_tn(a, b):
    return lax.dot_general(a, b, (((0,), (0,)), ((), ())), preferred_element_type=F32)


def _ada_kernel(c_ref, w_ref, b_ref, o_ref):
    ca = _silu(c_ref[...])
    o_ref[...] = jnp.dot(ca, w_ref[...], preferred_element_type=F32, precision=HIGHEST) + b_ref[...]


def _ada_mod(c_all, ada_w, ada_b):
    R, D = c_all.shape
    n_out = ada_w.shape[-1]
    tn = min(1536, n_out)
    return pl.pallas_call(
        _ada_kernel,
        grid=(DEPTH, n_out // tn),
        in_specs=[
            pl.BlockSpec((R, D), lambda l, j: (0, 0)),
            pl.BlockSpec((None, D, tn), lambda l, j: (l, 0, j)),
            pl.BlockSpec((None, 1, tn), lambda l, j: (l, 0, j)),
        ],
        out_specs=pl.BlockSpec((None, R, tn), lambda l, j: (l, 0, j)),
        out_shape=jax.ShapeDtypeStruct((DEPTH, R, n_out), F32),
        compiler_params=_cparams(("arbitrary", "arbitrary")),
        name="ada_mod",
    )(c_all, ada_w, ada_b.reshape(DEPTH, 1, n_out))


def _norm_mod(x, g, sh, sc):
    ms = jnp.mean(x * x, axis=-1, keepdims=True)
    y = x * lax.rsqrt(ms + EPS) * g
    return y * (1.0 + sc) + sh


def _proj_kernel(x_ref, g_ref, sh_ref, sc_ref, w_ref, o_ref, h_scr):
    @pl.when(pl.program_id(2) == 0)
    def _():
        h_scr[...] = _norm_mod(x_ref[...], g_ref[...], sh_ref[...], sc_ref[...]).astype(BF16)

    o_ref[...] = _dot(h_scr[...], w_ref[...]).astype(o_ref.dtype)


def _proj_qknorm_kernel(x_ref, g_ref, sh_ref, sc_ref, w_ref, bd_ref, gain_ref, o_ref, h_scr, *, qk_blocks, tn):
    j = pl.program_id(2)

    @pl.when(j == 0)
    def _():
        h_scr[...] = _norm_mod(x_ref[...], g_ref[...], sh_ref[...], sc_ref[...]).astype(BF16)

    acc = _dot(h_scr[...], w_ref[...])

    @pl.when(j < qk_blocks)
    def _():
        for c in range(tn // 256):
            a = acc[:, c * 256:(c + 1) * 256]
            ss = _dot((a * a).astype(BF16), bd_ref[...])
            r = lax.rsqrt(ss * (1.0 / NA_HEAD_DIM) + EPS)
            o_ref[:, c * 256:(c + 1) * 256] = (a * r * gain_ref[:, c * 256:(c + 1) * 256]).astype(o_ref.dtype)

    @pl.when(j >= qk_blocks)
    def _():
        o_ref[...] = acc.astype(o_ref.dtype)


def _proj_rotary_kernel(x_ref, g_ref, sh_ref, sc_ref, w_ref, cos_ref, sin_ref, o_ref, h_scr, *, tn):
    j = pl.program_id(2)

    @pl.when(j == 0)
    def _():
        h_scr[...] = _norm_mod(x_ref[...], g_ref[...], sh_ref[...], sc_ref[...]).astype(BF16)

    acc = _dot(h_scr[...], w_ref[...])
    qk_blocks = 2 * RET_QK_WIDTH // tn
    half = RET_QK_DIM // 2

    @pl.when(j < qk_blocks)
    def _():
        scale = jnp.where(j < qk_blocks // 2, 1.0, RET_QK_DIM ** -0.5)
        cos = cos_ref[...] * scale
        sin = sin_ref[...] * scale
        for c in range(tn // RET_QK_DIM):
            a1 = acc[:, c * RET_QK_DIM:c * RET_QK_DIM + half]
            a2 = acc[:, c * RET_QK_DIM + half:(c + 1) * RET_QK_DIM]
            o_ref[:, c * RET_QK_DIM:c * RET_QK_DIM + half] = (a1 * cos - a2 * sin).astype(o_ref.dtype)
            o_ref[:, c * RET_QK_DIM + half:(c + 1) * RET_QK_DIM] = (a1 * sin + a2 * cos).astype(o_ref.dtype)

    @pl.when(j >= qk_blocks)
    def _():
        o_ref[...] = acc.astype(o_ref.dtype)


def _proj(x, g, sh, sc, w, qk=None, rotary=None):
    B, T, D = x.shape
    n_out = w.shape[1]
    tm = _tile(T, 1024)
    tn = _tile(n_out, 1024)
    grid = (B, T // tm, n_out // tn)
    in_specs = [
        pl.BlockSpec((None, tm, D), lambda b, i, j: (b, i, 0)),
        pl.BlockSpec((1, D), lambda b, i, j: (0, 0)),
        pl.BlockSpec((None, 1, D), lambda b, i, j: (b, 0, 0)),
        pl.BlockSpec((None, 1, D), lambda b, i, j: (b, 0, 0)),
        pl.BlockSpec((D, tn), lambda b, i, j: (0, j)),
    ]
    args = [x, g, sh, sc, w]
    if rotary is not None:
        half = RET_QK_DIM // 2
        in_specs += [pl.BlockSpec((tm, half), lambda b, i, j: (i, 0))] * 2
        args += list(rotary)
        body = functools.partial(_proj_rotary_kernel, tn=tn)
        name = "proj_rotary"
    elif qk is None:
        body = _proj_kernel
        name = "proj"
    else:
        bd, gain, n_qk = qk
        in_specs += [
            pl.BlockSpec((256, 256), lambda b, i, j: (0, 0)),
            pl.BlockSpec((1, tn), lambda b, i, j: (0, j)),
        ]
        args += [bd, gain]
        body = functools.partial(_proj_qknorm_kernel, qk_blocks=n_qk // tn, tn=tn)
        name = "proj_qknorm"
    return pl.pallas_call(
        body,
        grid=grid,
        in_specs=in_specs,
        out_specs=pl.BlockSpec((None, tm, tn), lambda b, i, j: (b, i, j)),
        out_shape=jax.ShapeDtypeStruct((B, T, n_out), BF16),
        scratch_shapes=[pltpu.VMEM((tm, D), BF16)],
        compiler_params=_cparams(("parallel", "parallel", "arbitrary")),
        name=name,
    )(*args)


def _outproj_kernel(y_ref, w_ref, x_ref, g_ref, ng_ref, sh_ref, sc_ref, r_ref, o_ref, h_ref, aff_ref):
    xn = x_ref[...] + g_ref[...] * _dot(y_ref[...], w_ref[...])
    o_ref[...] = xn
    h = _norm_mod(xn, ng_ref[...], sh_ref[...], sc_ref[...])
    h_ref[...] = h.astype(BF16)
    logits = jnp.dot(h, r_ref[...], preferred_element_type=F32, precision=HIGHEST)
    m = jnp.max(logits, axis=-1, keepdims=True)
    e = jnp.exp(logits - m)
    aff_ref[...] = e / jnp.sum(e, axis=-1, keepdims=True)


def _out_proj(y, w, x, gate, ng, sh, sc, router):
    B, T, K = y.shape
    D = w.shape[1]
    E = router.shape[1]
    tm = _tile(T, 512)
    nt = T // tm
    return pl.pallas_call(
        _outproj_kernel,
        grid=(B, nt),
        in_specs=[
            pl.BlockSpec((None, tm, K), lambda b, i: (b, i, 0)),
            pl.BlockSpec((K, D), lambda b, i: (0, 0)),
            pl.BlockSpec((None, tm, D), lambda b, i: (b, i, 0)),
            pl.BlockSpec((None, 1, D), lambda b, i: (b, 0, 0)),
            pl.BlockSpec((1, D), lambda b, i: (0, 0)),
            pl.BlockSpec((None, 1, D), lambda b, i: (b, 0, 0)),
            pl.BlockSpec((None, 1, D), lambda b, i: (b, 0, 0)),
            pl.BlockSpec((D, E), lambda b, i: (0, 0)),
        ],
        out_specs=[
            pl.BlockSpec((None, tm, D), lambda b, i: (b, i, 0)),
            pl.BlockSpec((tm, D), lambda b, i: (b * nt + i, 0)),
            pl.BlockSpec((tm, E), lambda b, i: (b * nt + i, 0)),
        ],
        out_shape=[jax.ShapeDtypeStruct((B, T, D), F32), jax.ShapeDtypeStruct((B * T, D), BF16),
                   jax.ShapeDtypeStruct((B * T, E), F32)],
        compiler_params=_cparams(("parallel", "parallel")),
        name="out_proj",
    )(y, w, x, gate, ng, sh, sc, router)


def _retention_kernel(*refs, has_prev):
    if has_prev:
        q_ref, k_ref, v_ref, gate_ref, intra_ref, qdec_ref, kdec_ref, cdec_ref, prev_ref, o_ref, state = refs
    else:
        q_ref, k_ref, v_ref, gate_ref, intra_ref, qdec_ref, kdec_ref, cdec_ref, o_ref, state = refs
        prev_ref = None

    @pl.when(pl.program_id(1) == 0)
    def _():
        state[...] = jnp.zeros_like(state)

    for h in range(RET_HEADS):
        qk = slice(h * RET_QK_DIM, (h + 1) * RET_QK_DIM)
        vs = slice(h * RET_V_DIM, (h + 1) * RET_V_DIM)
        qb = q_ref[:, qk]
        kb = k_ref[:, qk]
        v = v_ref[:, vs]
        scores = _dot_nt(qb, kb) * intra_ref[h]
        st = state[h]
        o = _dot(scores.astype(BF16), v) + _dot(qb, st.astype(BF16)) * qdec_ref[h]
        state[h] = st * cdec_ref[h] + _dot_tn((kb.astype(F32) * kdec_ref[h]).astype(BF16), v)

        mu = jnp.mean(o, axis=-1, keepdims=True)
        oc = o - mu
        var = jnp.mean(oc * oc, axis=-1, keepdims=True)
        y = oc * lax.rsqrt(var + EPS) * _silu(gate_ref[:, vs].astype(F32))
        if has_prev:
            y = y + prev_ref[:, vs].astype(F32)
        o_ref[:, vs] = y.astype(o_ref.dtype)


def _retention_dir(proj, tabs, direction, prev):
    B, T, _ = proj.shape
    rc = min(RET_BLOCK, T)
    nc = T // rc
    intra, qdec, kdec, cdec = tabs
    if direction == 0:
        cidx = lambda c: c
    else:
        cidx = lambda c: nc - 1 - c
    H = RET_HEADS
    v_blk = 2 * RET_QK_WIDTH // RET_V_WIDTH
    in_specs = [
        pl.BlockSpec((None, rc, RET_QK_WIDTH), lambda b, c: (b, cidx(c), 0)),
        pl.BlockSpec((None, rc, RET_QK_WIDTH), lambda b, c: (b, cidx(c), 1)),
        pl.BlockSpec((None, rc, RET_V_WIDTH), lambda b, c: (b, cidx(c), v_blk)),
        pl.BlockSpec((None, rc, RET_V_WIDTH), lambda b, c: (b, cidx(c), v_blk + 1 + direction)),
        pl.BlockSpec((H, rc, rc), lambda b, c: (direction, 0, 0)),
        pl.BlockSpec((H, rc, 1), lambda b, c: (direction, 0, 0)),
        pl.BlockSpec((H, rc, 1), lambda b, c: (direction, 0, 0)),
        pl.BlockSpec((H, 1, 1), lambda b, c: (direction, 0, 0)),
    ]
    args = [proj, proj, proj, proj, intra, qdec, kdec, cdec]
    if prev is not None:
        in_specs.append(pl.BlockSpec((None, rc, RET_V_WIDTH), lambda b, c: (b, cidx(c), 0)))
        args.append(prev)
    return pl.pallas_call(
        functools.partial(_retention_kernel, has_prev=prev is not None),
        grid=(B, nc),
        in_specs=in_specs,
        out_specs=pl.BlockSpec((None, rc, RET_V_WIDTH), lambda b, c: (b, cidx(c), 0)),
        out_shape=jax.ShapeDtypeStruct((B, T, RET_V_WIDTH), BF16),
        scratch_shapes=[pltpu.VMEM((H, RET_QK_DIM, RET_V_DIM), F32)],
        compiler_params=_cparams(("parallel", "arbitrary")),
        name="retention_bwd" if direction else "retention_fwd",
    )(*args)


def _retention_tables(decay_logit, rc):
    lg = jax.nn.log_sigmoid(decay_logit.astype(F32).reshape(-1))
    pos = jnp.arange(rc, dtype=F32)
    rel = pos[:, None] - pos[None, :]
    l3 = lg[:, None, None]
    fwd = jnp.where(rel >= 0, jnp.exp(jnp.maximum(rel, 0.0) * l3), 0.0)
    bwd = jnp.where(rel <= 0, jnp.exp(jnp.maximum(-rel, 0.0) * l3), 0.0)
    is_bwd = (jnp.arange(2 * RET_HEADS) >= RET_HEADS)
    intra = jnp.where(is_bwd[:, None, None], bwd, fwd)
    q_pow = jnp.where(is_bwd[:, None], rc - pos[None, :], pos[None, :] + 1.0)
    k_pow = jnp.where(is_bwd[:, None], pos[None, :], rc - 1.0 - pos[None, :])
    qdec = jnp.exp(q_pow * lg[:, None])[..., None]
    kdec = jnp.exp(k_pow * lg[:, None])[..., None]
    cdec = jnp.exp(rc * lg)[:, None, None]
    return intra, qdec, kdec, cdec


def _rotary_tables(T):
    d = RET_QK_DIM
    inv = 1.0 / (ROPE_BASE ** (jnp.arange(0, d, 2, dtype=F32) / d))
    ang = jnp.arange(T, dtype=F32)[:, None] * inv[None, :]
    return jnp.cos(ang), jnp.sin(ang)


def _na_kernel(q_ref, k_ref, v_ref, bias_ref, o_ref, *, rows):
    n_groups = rows // NA_ROWS_PER_GROUP
    lane = lax.broadcasted_iota(I32, (NA_Q, LANES), 1)
    first = lane < NA_HEAD_DIM

    def group(gi, carry):
        ustart = jnp.clip(gi * NA_ROWS_PER_GROUP - NA_KH // 2, 0, rows - NA_WIN_ROWS)
        pat = jnp.where(gi == 0, 0, jnp.where(gi == n_groups - 1, 2, 1))
        kstart = pl.multiple_of(ustart * GRID_W, GRID_W)
        qstart = pl.multiple_of(gi * NA_Q, NA_Q)
        kw = k_ref[pl.ds(kstart, NA_K), :]
        vw = v_ref[pl.ds(kstart, NA_K), :]
        q = q_ref[pl.ds(qstart, NA_Q), :].astype(F32)
        q2 = jnp.concatenate([jnp.where(first, q, 0.0), jnp.where(first, 0.0, q)], axis=0).astype(BF16)
        s = _dot_nt(q2, kw) + bias_ref[pat].reshape(2 * NA_Q, NA_K)
        m = jnp.max(s, axis=-1, keepdims=True)
        e = jnp.exp(s - m)
        l = jnp.sum(e, axis=-1, keepdims=True)
        o2 = _dot(e.astype(BF16), vw) / l
        o_ref[pl.ds(qstart, NA_Q), :] = jnp.where(first, o2[:NA_Q], o2[NA_Q:]).astype(o_ref.dtype)
        return carry

    lax.fori_loop(0, n_groups, group, 0, unroll=2)


def _na_attention(qkv, bias):
    B, T, _ = qkv.shape
    rows = T // GRID_W
    assert rows % NA_ROWS_PER_GROUP == 0 and rows >= NA_WIN_ROWS + 1
    n_pairs = D_MODEL // LANES
    blk = lambda off: pl.BlockSpec((None, T, LANES), lambda p, b: (b, 0, off + p))
    return pl.pallas_call(
        functools.partial(_na_kernel, rows=rows),
        grid=(n_pairs, B),
        in_specs=[
            blk(0), blk(n_pairs), blk(2 * n_pairs),
            pl.BlockSpec((3, 2, NA_Q, NA_K), lambda p, b: (0, p, 0, 0)),
        ],
        out_specs=pl.BlockSpec((None, T, LANES), lambda p, b: (b, 0, p)),
        out_shape=jax.ShapeDtypeStruct((B, T, D_MODEL), BF16),
        compiler_params=_cparams(("parallel", "parallel")),
        name="na_attention",
    )(qkv, qkv, qkv, bias)


def _na_bias_table(rpb):
    H = rpb.shape[0]
    r = rpb.astype(F32)
    pad = GRID_W
    rp = jnp.concatenate([jnp.repeat(r[..., :1], pad, -1), r, jnp.repeat(r[..., -1:], pad, -1)], -1)
    base = NA_KW - 1 + pad
    t1 = jnp.stack([rp[..., base - qc: base - qc + GRID_W] for qc in range(GRID_W)], axis=2)
    qc = np.arange(GRID_W)[:, None]
    kc = np.arange(GRID_W)[None, :]
    ws = np.clip(qc - NA_KW // 2, 0, GRID_W - NA_KW)
    t1 = jnp.where(jnp.asarray((kc >= ws) & (kc < ws + NA_KW)), t1, NEG_INF)
    masked = jnp.full((H, GRID_W, GRID_W), NEG_INF, F32)
    pats = []
    for p in range(3):
        per_row = []
        for i in range(NA_ROWS_PER_GROUP):
            lo = (0, i, NA_ROWS_PER_GROUP - 1)[p]
            qrow = (i, i + NA_KH // 2, i + NA_KH - 1)[p]
            us = [t1[:, u - qrow + NA_KH - 1] if lo <= u < lo + NA_KH else masked for u in range(NA_WIN_ROWS)]
            per_row.append(jnp.stack(us, axis=2))
        pats.append(jnp.stack(per_row, axis=1))
    return jnp.stack(pats, axis=0).reshape(3, H, NA_Q, NA_K)


def _select_kernel(aff_ref, pos_ref, pre_ref, *, cap, n_tok, rb):
    E = N_EXPERTS
    per_row = LANES // E
    rows = n_tok // per_row
    bits = pltpu.bitcast(aff_ref[...], I32)
    lane = lax.broadcasted_iota(I32, (rows, LANES), 1)
    tok = lax.broadcasted_iota(I32, (rows, LANES), 0) * per_row + lane // E

    def count(pred):
        c = jnp.sum(pred.astype(I32), axis=0, keepdims=True)
        c = jnp.broadcast_to(c, (8, LANES))
        sh = E
        while sh < LANES:
            c = c + pltpu.roll(c, sh, 1)
            sh *= 2
        return c[0:1]

    def thr_step(i, thr):
        cand = thr | jnp.left_shift(jnp.int32(1), 30 - i)
        return jnp.where(count(bits >= cand) >= cap, cand, thr)

    thr = lax.fori_loop(0, 31, thr_step, jnp.zeros((1, LANES), I32))
    gt = bits > thr
    eq = bits == thr
    need = cap - count(gt)

    nbits = max(1, int(math.ceil(math.log2(n_tok))))

    def tie_step(i, jmax):
        cand = jmax + jnp.left_shift(jnp.int32(1), nbits - 1 - i)
        return jnp.where(count(jnp.logical_and(eq, tok < cand)) < need, cand, jmax)

    jmax = lax.fori_loop(0, nbits, tie_step, jnp.zeros((1, LANES), I32))

    li = lax.broadcasted_iota(I32, (LANES, LANES), 0)
    lj = lax.broadcasted_iota(I32, (LANES, LANES), 1)
    same_e = (li % E) == (lj % E)
    a_before = _ones_where(jnp.logical_and(same_e, li // E < lj // E))
    a_all = _ones_where(same_e)
    ri = lax.broadcasted_iota(I32, (rb, rb), 0)
    rj = lax.broadcasted_iota(I32, (rb, rb), 1)
    lower = _ones_where(rj < ri)

    running = jnp.zeros((1, LANES), F32)
    for blk in range(rows // rb):
        sl = slice(blk * rb, (blk + 1) * rb)
        b_blk = pltpu.bitcast(aff_ref[sl, :], I32)
        t_blk = ((lax.broadcasted_iota(I32, (rb, LANES), 0) + blk * rb) * per_row
                 + lax.broadcasted_iota(I32, (rb, LANES), 1) // E)
        keep = jnp.logical_or(b_blk > thr, jnp.logical_and(b_blk == thr, t_blk <= jmax))
        kb = _ones_where(keep)
        row_tot = _dot(kb, a_all)
        within = _dot(lower, row_tot.astype(BF16))
        pre = running + within + _dot(kb, a_before)
        running = running + jnp.sum(row_tot, axis=0, keepdims=True)
        pre_i = pre.astype(I32)
        pre_ref[sl, :] = pre_i
        pos_ref[sl, :] = jnp.where(keep, pre_i, -1)


def _select(aff, cap):
    n_tok, E = aff.shape
    rows = n_tok * E // LANES
    rb = min(256, rows)
    aff8 = aff.reshape(rows, LANES)
    pos, pre = pl.pallas_call(
        functools.partial(_select_kernel, cap=cap, n_tok=n_tok, rb=rb),
        out_shape=[jax.ShapeDtypeStruct((rows, LANES), I32)] * 2,
        compiler_params=pltpu.CompilerParams(vmem_limit_bytes=VMEM_LIMIT),
        name="expert_select",
    )(aff8)
    return pos.reshape(n_tok, E), pre.reshape(n_tok, E)


DISPATCH_GROUP = 2


def _dispatch_slab(tn):
    return tn // 4 + BF16_SUBLANES


def _dispatch_rounds(tn):
    return -(-(tn + BF16_SUBLANES) // _dispatch_slab(tn))


def _dispatch_kernel(ws_ref, need_ref, pos_ref, aff_ref, h_ref, xe_ref, gs_ref, *, n_tiles, sub, batch):
    eg = pl.program_id(0)
    tb = pl.program_id(1)
    group, _, tn = pos_ref.shape
    head = BF16_SUBLANES
    slab = _dispatch_slab(tn)
    n_rounds = _dispatch_rounds(tn)

    @pl.when(tb == 0)
    def _():
        xe_ref[...] = jnp.zeros_like(xe_ref)
        gs_ref[...] = jnp.zeros_like(gs_ref)

    def one_round(s, r):
        t = tb * sub + s
        hs = h_ref[pl.ds(pl.multiple_of(s * tn, tn), tn), :]
        row = lax.broadcasted_iota(I32, (slab, tn), 0) + r * slab
        ws = [pl.multiple_of(ws_ref[(eg * group + k) * n_tiles + t], head) for k in range(group)]
        hits = [row == pos_ref[k, pl.ds(s, 1), :] - ws[k] for k in range(group)]
        rows = _dot(jnp.concatenate([_ones_where(hk) for hk in hits], axis=0), hs)
        for k in range(group):
            rk = rows[k * slab:(k + 1) * slab]
            gk = jnp.sum(jnp.where(hits[k], aff_ref[k, pl.ds(s, 1), :], 0.0), axis=1, keepdims=True)
            base = pl.multiple_of(ws[k] + r * slab, head)
            acc_rows = head if r == 0 else slab
            xe_ref[k, pl.ds(base, acc_rows), :] = (
                xe_ref[k, pl.ds(base, acc_rows), :].astype(F32) + rk[:acc_rows]).astype(xe_ref.dtype)
            gs_ref[k, pl.ds(base, acc_rows), :] = gs_ref[k, pl.ds(base, acc_rows), :] + gk[:acc_rows]
            if r == 0:
                rest = pl.multiple_of(base + head, head)
                xe_ref[k, pl.ds(rest, slab - head), :] = rk[head:].astype(xe_ref.dtype)
                gs_ref[k, pl.ds(rest, slab - head), :] = gk[head:]

    def tile_batch(sb, carry):
        for u in range(batch):
            one_round(sb * batch + u, 0)
        for u in range(batch):
            s = sb * batch + u
            need = need_ref[eg * n_tiles + tb * sub + s]
            for r in range(1, n_rounds):
                pl.when(need > r * slab)(functools.partial(one_round, s, r))
        return carry

    lax.fori_loop(0, sub // batch, tile_batch, 0)


def _dispatch(wstart, need, pos_t, aff_t, h, cap, tn):
    E, n_tiles, _ = pos_t.shape
    N, D = h.shape
    rows = cap + _dispatch_rounds(tn) * _dispatch_slab(tn)
    sub = 8 if n_tiles % 8 == 0 else n_tiles
    group = DISPATCH_GROUP
    resident = dict(pipeline_mode=pl.Buffered(1))
    grid_spec = pltpu.PrefetchScalarGridSpec(
        num_scalar_prefetch=2,
        grid=(E // group, n_tiles // sub),
        in_specs=[
            pl.BlockSpec((group, sub, tn), lambda e, t, ws, nd: (e, t, 0)),
            pl.BlockSpec((group, sub, tn), lambda e, t, ws, nd: (e, t, 0)),
            pl.BlockSpec((sub * tn, D), lambda e, t, ws, nd: (t, 0)),
        ],
        out_specs=[
            pl.BlockSpec((group, rows, D), lambda e, t, ws, nd: (e, 0, 0), **resident),
            pl.BlockSpec((group, rows, 1), lambda e, t, ws, nd: (e, 0, 0), **resident),
        ],
    )
    return pl.pallas_call(
        functools.partial(_dispatch_kernel, n_tiles=n_tiles, sub=sub, batch=4 if sub % 4 == 0 else 1),
        grid_spec=grid_spec,
        out_shape=[jax.ShapeDtypeStruct((E, rows, D), BF16), jax.ShapeDtypeStruct((E, rows, 1), F32)],
        compiler_params=_cparams(("parallel", "arbitrary")),
        name="moe_dispatch",
    )(wstart, need, pos_t, aff_t, h)


def _ffn_kernel(x_ref, gs_ref, wg_ref, wu_ref, wd_ref, o_ref, acc, *, n_f):
    f = pl.program_id(2)

    @pl.when(f == 0)
    def _():
        acc[...] = jnp.zeros_like(acc)

    x = x_ref[...]
    hid = _silu(_dot(x, wg_ref[...].astype(BF16))) * _dot(x, wu_ref[...].astype(BF16))
    acc[...] += _dot(hid.astype(BF16), wd_ref[...].astype(BF16))

    @pl.when(f == n_f - 1)
    def _():
        o_ref[...] = (acc[...] * gs_ref[...]).astype(o_ref.dtype)


def _expert_ffn(xe, gs, wg, wu, wd, layer, C):
    E, _, D = xe.shape
    F = wg.shape[-1]
    tm = _tile(C, 2048)
    tf = _tile(F, 512)
    n_f = F // tf
    return pl.pallas_call(
        functools.partial(_ffn_kernel, n_f=n_f),
        grid=(E, C // tm, n_f),
        in_specs=[
            pl.BlockSpec((None, tm, D), lambda e, m, f: (e, m, 0)),
            pl.BlockSpec((None, tm, 1), lambda e, m, f: (e, m, 0)),
            pl.BlockSpec((None, None, D, tf), lambda e, m, f: (layer, e, 0, f)),
            pl.BlockSpec((None, None, D, tf), lambda e, m, f: (layer, e, 0, f)),
            pl.BlockSpec((None, None, tf, D), lambda e, m, f: (layer, e, f, 0)),
        ],
        out_specs=pl.BlockSpec((None, tm, D), lambda e, m, f: (e, m, 0)),
        out_shape=jax.ShapeDtypeStruct((E, C, D), BF16),
        scratch_shapes=[pltpu.VMEM((tm, D), F32)],
        compiler_params=_cparams(("parallel", "parallel", "arbitrary")),
        name="expert_ffn",
    )(xe, gs, wg, wu, wd)


COMBINE_SLAB = LANES


def _combine_kernel(ws_ref, end_ref, pos_ref, ye_any, *refs, n_tiles, n_exp, win):
    ye_refs = refs[:n_exp]
    x_ref, g_ref, o_ref, hit_scr, y_scr, late_scr = refs[n_exp:]
    t = pl.program_id(0)
    tn = pos_ref.shape[0]
    slab = ye_refs[0].shape[1]
    lane = lax.broadcasted_iota(I32, (tn, slab), 1)
    for e in range(n_exp):
        prel = pos_ref[:, e:e + 1] - ws_ref[e * n_tiles + t]
        hit_scr[:, e * slab:(e + 1) * slab] = _ones_where(lane == prel)
        y_scr[e * slab:(e + 1) * slab, :] = ye_refs[e][0]
    o_ref[...] = x_ref[...] + g_ref[...] * _dot(hit_scr[...], y_scr[...])

    if win > slab:
        for e in range(n_exp):
            ws = ws_ref[e * n_tiles + t]

            @pl.when(end_ref[e * n_tiles + t] - ws > slab)
            def _():
                late = pl.multiple_of(ws + slab, BF16_SUBLANES)
                pltpu.sync_copy(ye_any.at[e, pl.ds(late, win - slab), :], late_scr)
                prel = pos_ref[:, e:e + 1] - ws - slab
                hit = _ones_where(lax.broadcasted_iota(I32, (tn, win - slab), 1) == prel)
                o_ref[...] += g_ref[...] * _dot(hit, late_scr[...])


def _combine(wstart, wend, pos, ye, x, gate, tn):
    E, C, D = ye.shape
    B, T, _ = x.shape
    N = B * T
    n_tiles = N // tn
    tpb = T // tn
    win = tn + BF16_SUBLANES
    slab = min(COMBINE_SLAB, win)

    def window(e):
        return pl.BlockSpec((pl.Element(1), pl.Element(slab), pl.Element(D)),
                            lambda t, ws, we: (e, pl.multiple_of(ws[e * n_tiles + t], BF16_SUBLANES), 0))

    grid_spec = pltpu.PrefetchScalarGridSpec(
        num_scalar_prefetch=2,
        grid=(n_tiles,),
        in_specs=[pl.BlockSpec((tn, E), lambda t, ws, we: (t, 0)),
                  pl.BlockSpec(memory_space=pl.ANY)]
        + [window(e) for e in range(E)]
        + [pl.BlockSpec((None, tn, D), lambda t, ws, we: (t // tpb, t % tpb, 0)),
           pl.BlockSpec((None, 1, D), lambda t, ws, we: (t // tpb, 0, 0))],
        out_specs=pl.BlockSpec((None, tn, D), lambda t, ws, we: (t // tpb, t % tpb, 0)),
        scratch_shapes=[pltpu.VMEM((tn, E * slab), BF16), pltpu.VMEM((E * slab, D), BF16),
                        pltpu.VMEM((max(win - slab, BF16_SUBLANES), D), BF16)],
    )
    return pl.pallas_call(
        functools.partial(_combine_kernel, n_tiles=n_tiles, n_exp=E, win=win),
        grid_spec=grid_spec,
        out_shape=jax.ShapeDtypeStruct((B, T, D), F32),
        compiler_params=_cparams(("parallel",)),
        name="moe_combine",
    )(wstart, wend, pos, ye, *([ye] * E), x, gate)


def _moe_tile(n_tok, cap):
    tn = 256
    while tn + BF16_SUBLANES > cap or n_tok % tn:
        tn //= 2
    return tn


def _moe_layer(x, h, aff, gate, wg, wu, wd, layer):
    B, T, D = x.shape
    N = B * T
    E = aff.shape[1]
    cap = EC_CAPACITY_FACTOR * N // E
    tn = _moe_tile(T, cap)
    win = tn + BF16_SUBLANES
    pos, pre = _select(aff, cap)
    start = pre[::tn].T
    aligned = (start // BF16_SUBLANES * BF16_SUBLANES).astype(I32)
    wdisp = aligned.reshape(-1)
    wstart = jnp.minimum(aligned, cap - win).reshape(-1)
    end = jnp.concatenate([start[:, 1:], jnp.full((E, 1), cap, I32)], axis=1).astype(I32)
    wend = end.reshape(-1)
    need = (end - aligned).reshape(E // DISPATCH_GROUP, DISPATCH_GROUP, -1).max(axis=1).reshape(-1)
    pos_t = pos.T.reshape(E, N // tn, tn)
    aff_t = aff.T.reshape(E, N // tn, tn)
    xe, gs = _dispatch(wdisp, need, pos_t, aff_t, h, cap, tn)
    ye = _expert_ffn(xe, gs, wg, wu, wd, layer, cap)
    return _combine(wstart, wend, pos, ye, x, gate, tn)


def _trunk(x, mod, p):
    B, T, D = x.shape
    rc = min(RET_BLOCK, T)
    cos, sin = _rotary_tables(T)
    for i in range(DEPTH):
        sh1, sc1, g1, sh2, sc2, g2 = [mod[i, :, k][:, None, :] for k in range(6)]
        j = i // 2
        if i % 2 == 0:
            proj = _proj(x, p["norm_mix_g"][i][None], sh1, sc1, p["ret_w_in"][j], rotary=(cos, sin))
            tabs = _retention_tables(p["ret_decay_logit"][j], rc)
            y_f = _retention_dir(proj, tabs, 0, None)
            y = _retention_dir(proj, tabs, 1, y_f)
            w_out = p["ret_w_out"][j]
        else:
            qkv = _proj(x, p["norm_mix_g"][i][None], sh1, sc1, p["na_w_in"][j],
                        qk=(p["na_bd"], p["na_gain"][j], 2 * D))
            y = _na_attention(qkv, p["na_bias"][j])
            w_out = p["na_w_out"][j]
        x, h, aff = _out_proj(y, w_out, x, g1, p["norm_ffn_g"][i][None], sh2, sc2, p["moe_router"][i])
        x = _moe_layer(x, h, aff, g2, p["moe_w_gate"], p["moe_w_up"], p["moe_w_down"], i)
    return x


def kernel(x_prompt, x_sample, c_prompt, c_sample, norm_mix_g, norm_ffn_g, ada_w, ada_b, ret_w_in, ret_decay_logit, ret_w_out, na_w_in, na_q_gain, na_k_gain, na_rpb, na_w_out, moe_router, moe_w_gate, moe_w_up, moe_w_down):
    D = D_MODEL
    bp, bs = c_prompt.shape[0], c_sample.shape[0]
    c_all = jnp.concatenate([c_prompt, c_sample], axis=0)
    pad = (-c_all.shape[0]) % 8
    if pad:
        c_all = jnp.pad(c_all, ((0, pad), (0, 0)))
    mod = _ada_mod(c_all, ada_w, ada_b).reshape(DEPTH, c_all.shape[0], 6, D)

    heads_row = lambda v: jnp.tile(v.astype(F32), (1, NA_HEADS))
    na_gain = jnp.concatenate(
        [heads_row(na_q_gain) * (NA_HEAD_DIM ** -0.5), heads_row(na_k_gain),
         jnp.ones((na_q_gain.shape[0], D), F32)], axis=-1)[:, None, :]
    bd = np.kron(np.eye(256 // NA_HEAD_DIM), np.ones((NA_HEAD_DIM, NA_HEAD_DIM)))
    p = dict(
        norm_mix_g=norm_mix_g, norm_ffn_g=norm_ffn_g,
        ret_w_in=ret_w_in.astype(BF16), ret_decay_logit=ret_decay_logit, ret_w_out=ret_w_out.astype(BF16),
        na_w_in=na_w_in.astype(BF16), na_w_out=na_w_out.astype(BF16),
        na_gain=na_gain, na_bd=jnp.asarray(bd, BF16),
        na_bias=jnp.stack([_na_bias_table(na_rpb[l]) for l in range(na_rpb.shape[0])]),
        moe_router=moe_router,
        moe_w_gate=moe_w_gate, moe_w_up=moe_w_up, moe_w_down=moe_w_down,
    )
    y_prompt = _trunk(x_prompt, mod[:, :bp], p)
    y_sample = _trunk(x_sample, mod[:, bp:bp + bs], p)
    return (y_prompt, y_sample)
```

```python
import functools
import math

import numpy as np
import jax
import jax.numpy as jnp
from jax import lax
from jax.experimental import pallas as pl
from jax.experimental.pallas import tpu as pltpu

F32 = jnp.float32
BF16 = jnp.bfloat16
I32 = jnp.int32

D_MODEL = 1024
DEPTH = 4

RET_HEADS = 4
RET_QK_DIM = 256
RET_V_DIM = 512
RET_QK_WIDTH = RET_HEADS * RET_QK_DIM
RET_V_WIDTH = RET_HEADS * RET_V_DIM
RET_IN_COLS = 2 * RET_QK_WIDTH + 3 * RET_V_WIDTH
ROPE_BASE = 10000.0
RET_BLOCK = 256

NA_HEADS = 16
NA_HEAD_DIM = 64
NA_KH = 8
NA_KW = 16
GRID_W = 64
NA_ROWS_PER_GROUP = 4
NA_WIN_ROWS = NA_ROWS_PER_GROUP + NA_KH - 1
NA_Q = NA_ROWS_PER_GROUP * GRID_W
NA_K = NA_WIN_ROWS * GRID_W

N_EXPERTS = 16
EC_CAPACITY_FACTOR = 2
D_FF = 2 * D_MODEL

EPS = 1e-6
NEG_INF = -1e30

LANES = 128
BF16_SUBLANES = 16
VMEM_LIMIT = 56 * 1024 * 1024

HIGHEST = lax.Precision.HIGHEST


def _cparams(sem):
    return pltpu.CompilerParams(dimension_semantics=sem, vmem_limit_bytes=VMEM_LIMIT)


def _tile(n, pref):
    t = min(pref, n)
    while n % t:
        t //= 2
    return t


def _silu(x):
    return x * (1.0 / (1.0 + jnp.exp(-x)))


def _ones_where(mask):
    return jnp.where(mask, 1.0, 0.0).astype(BF16)


def _dot(a, b):
    return jnp.dot(a, b, preferred_element_type=F32)


def _dot_nt(a, b):
    return lax.dot_general(a, b, (((1,), (1,)), ((), ())), preferred_element_type=F32)


def _dot_tn(a, b):
    return lax.dot_general(a, b, (((0,), (0,)), ((), ())), preferred_element_type=F32)


def _ada_kernel(c_ref, w_ref, b_ref, o_ref):
    ca = _silu(c_ref[...])
    o_ref[...] = jnp.dot(ca, w_ref[...], preferred_element_type=F32, precision=HIGHEST) + b_ref[...]


def _ada_mod(c_all, ada_w, ada_b):
    R, D = c_all.shape
    n_out = ada_w.shape[-1]
    tn = min(1536, n_out)
    return pl.pallas_call(
        _ada_kernel,
        grid=(DEPTH, n_out // tn),
        in_specs=[
            pl.BlockSpec((R, D), lambda l, j: (0, 0)),
            pl.BlockSpec((None, D, tn), lambda l, j: (l, 0, j)),
            pl.BlockSpec((None, 1, tn), lambda l, j: (l, 0, j)),
        ],
        out_specs=pl.BlockSpec((None, R, tn), lambda l, j: (l, 0, j)),
        out_shape=jax.ShapeDtypeStruct((DEPTH, R, n_out), F32),
        compiler_params=_cparams(("arbitrary", "arbitrary")),
        name="ada_mod",
    )(c_all, ada_w, ada_b.reshape(DEPTH, 1, n_out))


def _norm_mod(x, g, sh, sc):
    ms = jnp.mean(x * x, axis=-1, keepdims=True)
    y = x * lax.rsqrt(ms + EPS) * g
    return y * (1.0 + sc) + sh


def _proj_kernel(x_ref, g_ref, sh_ref, sc_ref, w_ref, o_ref, h_scr):
    @pl.when(pl.program_id(2) == 0)
    def _():
        h_scr[...] = _norm_mod(x_ref[...], g_ref[...], sh_ref[...], sc_ref[...]).astype(BF16)

    o_ref[...] = _dot(h_scr[...], w_ref[...]).astype(o_ref.dtype)


def _proj_qknorm_kernel(x_ref, g_ref, sh_ref, sc_ref, w_ref, bd_ref, gain_ref, o_ref, h_scr, *, qk_blocks, tn):
    j = pl.program_id(2)

    @pl.when(j == 0)
    def _():
        h_scr[...] = _norm_mod(x_ref[...], g_ref[...], sh_ref[...], sc_ref[...]).astype(BF16)

    acc = _dot(h_scr[...], w_ref[...])

    @pl.when(j < qk_blocks)
    def _():
        for c in range(tn // 256):
            a = acc[:, c * 256:(c + 1) * 256]
            ss = _dot((a * a).astype(BF16), bd_ref[...])
            r = lax.rsqrt(ss * (1.0 / NA_HEAD_DIM) + EPS)
            o_ref[:, c * 256:(c + 1) * 256] = (a * r * gain_ref[:, c * 256:(c + 1) * 256]).astype(o_ref.dtype)

    @pl.when(j >= qk_blocks)
    def _():
        o_ref[...] = acc.astype(o_ref.dtype)


def _proj_rotary_kernel(x_ref, g_ref, sh_ref, sc_ref, w_ref, cos_ref, sin_ref, o_ref, h_scr, *, tn):
    j = pl.program_id(2)

    @pl.when(j == 0)
    def _():
        h_scr[...] = _norm_mod(x_ref[...], g_ref[...], sh_ref[...], sc_ref[...]).astype(BF16)

    acc = _dot(h_scr[...], w_ref[...])
    qk_blocks = 2 * RET_QK_WIDTH // tn
    half = RET_QK_DIM // 2

    @pl.when(j < qk_blocks)
    def _():
        scale = jnp.where(j < qk_blocks // 2, 1.0, RET_QK_DIM ** -0.5)
        cos = cos_ref[...] * scale
        sin = sin_ref[...] * scale
        for c in range(tn // RET_QK_DIM):
            a1 = acc[:, c * RET_QK_DIM:c * RET_QK_DIM + half]
            a2 = acc[:, c * RET_QK_DIM + half:(c + 1) * RET_QK_DIM]
            o_ref[:, c * RET_QK_DIM:c * RET_QK_DIM + half] = (a1 * cos - a2 * sin).astype(o_ref.dtype)
            o_ref[:, c * RET_QK_DIM + half:(c + 1) * RET_QK_DIM] = (a1 * sin + a2 * cos).astype(o_ref.dtype)

    @pl.when(j >= qk_blocks)
    def _():
        o_ref[...] = acc.astype(o_ref.dtype)


def _proj(x, g, sh, sc, w, qk=None, rotary=None):
    B, T, D = x.shape
    n_out = w.shape[1]
    tm = _tile(T, 1024)
    tn = _tile(n_out, 1024)
    grid = (B, T // tm, n_out // tn)
    in_specs = [
        pl.BlockSpec((None, tm, D), lambda b, i, j: (b, i, 0)),
        pl.BlockSpec((1, D), lambda b, i, j: (0, 0)),
        pl.BlockSpec((None, 1, D), lambda b, i, j: (b, 0, 0)),
        pl.BlockSpec((None, 1, D), lambda b, i, j: (b, 0, 0)),
        pl.BlockSpec((D, tn), lambda b, i, j: (0, j)),
    ]
    args = [x, g, sh, sc, w]
    if rotary is not None:
        half = RET_QK_DIM // 2
        in_specs += [pl.BlockSpec((tm, half), lambda b, i, j: (i, 0))] * 2
        args += list(rotary)
        body = functools.partial(_proj_rotary_kernel, tn=tn)
        name = "proj_rotary"
    elif qk is None:
        body = _proj_kernel
        name = "proj"
    else:
        bd, gain, n_qk = qk
        in_specs += [
            pl.BlockSpec((256, 256), lambda b, i, j: (0, 0)),
            pl.BlockSpec((1, tn), lambda b, i, j: (0, j)),
        ]
        args += [bd, gain]
        body = functools.partial(_proj_qknorm_kernel, qk_blocks=n_qk // tn, tn=tn)
        name = "proj_qknorm"
    return pl.pallas_call(
        body,
        grid=grid,
        in_specs=in_specs,
        out_specs=pl.BlockSpec((None, tm, tn), lambda b, i, j: (b, i, j)),
        out_shape=jax.ShapeDtypeStruct((B, T, n_out), BF16),
        scratch_shapes=[pltpu.VMEM((tm, D), BF16)],
        compiler_params=_cparams(("parallel", "parallel", "arbitrary")),
        name=name,
    )(*args)


def _outproj_kernel(y_ref, w_ref, x_ref, g_ref, o_ref):
    o_ref[...] = x_ref[...] + g_ref[...] * _dot(y_ref[...], w_ref[...])


def _ffn_pre_kernel(x_ref, g_ref, sh_ref, sc_ref, r_ref, h_ref, aff_ref):
    h = _norm_mod(x_ref[...], g_ref[...], sh_ref[...], sc_ref[...])
    h_hi = h.astype(BF16)
    h_ref[...] = h_hi
    h_lo = (h - h_hi.astype(F32)).astype(BF16)
    r = r_ref[...]
    r_hi = r.astype(BF16)
    r_lo = (r - r_hi.astype(F32)).astype(BF16)
    logits = _dot(h_hi, r_hi) + (_dot(h_hi, r_lo) + _dot(h_lo, r_hi))
    m = jnp.max(logits, axis=-1, keepdims=True)
    e = jnp.exp(logits - m)
    aff_ref[...] = e / jnp.sum(e, axis=-1, keepdims=True)


def _out_proj(y, w, x, gate, ng, sh, sc, router):
    B, T, K = y.shape
    D = w.shape[1]
    E = router.shape[1]
    tm = _tile(T, 512)
    xn = pl.pallas_call(
        _outproj_kernel,
        grid=(B, T // tm),
        in_specs=[
            pl.BlockSpec((None, tm, K), lambda b, i: (b, i, 0)),
            pl.BlockSpec((K, D), lambda b, i: (0, 0)),
            pl.BlockSpec((None, tm, D), lambda b, i: (b, i, 0)),
            pl.BlockSpec((None, 1, D), lambda b, i: (b, 0, 0)),
        ],
        out_specs=pl.BlockSpec((None, tm, D), lambda b, i: (b, i, 0)),
        out_shape=jax.ShapeDtypeStruct((B, T, D), F32),
        compiler_params=_cparams(("parallel", "parallel")),
        name="out_proj",
    )(y, w, x, gate)
    tp = _tile(T, 1024)
    nt = T // tp
    h, aff = pl.pallas_call(
        _ffn_pre_kernel,
        grid=(B, nt),
        in_specs=[
            pl.BlockSpec((None, tp, D), lambda b, i: (b, i, 0)),
            pl.BlockSpec((1, D), lambda b, i: (0, 0)),
            pl.BlockSpec((None, 1, D), lambda b, i: (b, 0, 0)),
            pl.BlockSpec((None, 1, D), lambda b, i: (b, 0, 0)),
            pl.BlockSpec((D, E), lambda b, i: (0, 0)),
        ],
        out_specs=[
            pl.BlockSpec((tp, D), lambda b, i: (b * nt + i, 0)),
            pl.BlockSpec((tp, E), lambda b, i: (b * nt + i, 0)),
        ],
        out_shape=[jax.ShapeDtypeStruct((B * T, D), BF16), jax.ShapeDtypeStruct((B * T, E), F32)],
        compiler_params=_cparams(("parallel", "parallel")),
        name="ffn_pre",
    )(xn, ng, sh, sc, router)
    return xn, h, aff


def _retention_kernel(*refs, has_prev):
    if has_prev:
        q_ref, k_ref, v_ref, gate_ref, intra_ref, qdec_ref, kdec_ref, cdec_ref, prev_ref, o_ref, state = refs
    else:
        q_ref, k_ref, v_ref, gate_ref, intra_ref, qdec_ref, kdec_ref, cdec_ref, o_ref, state = refs
        prev_ref = None

    @pl.when(pl.program_id(1) == 0)
    def _():
        state[...] = jnp.zeros_like(state)

    for h in range(RET_HEADS):
        qk = slice(h * RET_QK_DIM, (h + 1) * RET_QK_DIM)
        vs = slice(h * RET_V_DIM, (h + 1) * RET_V_DIM)
        qb = q_ref[:, qk]
        kb = k_ref[:, qk]
        v = v_ref[:, vs]
        scores = _dot_nt(qb, kb) * intra_ref[h]
        st = state[h]
        o = _dot(scores.astype(BF16), v) + _dot(qb, st.astype(BF16)) * qdec_ref[h]
        state[h] = st * cdec_ref[h] + _dot_tn((kb.astype(F32) * kdec_ref[h]).astype(BF16), v)

        mu = jnp.mean(o, axis=-1, keepdims=True)
        oc = o - mu
        var = jnp.mean(oc * oc, axis=-1, keepdims=True)
        y = oc * lax.rsqrt(var + EPS) * _silu(gate_ref[:, vs].astype(F32))
        if has_prev:
            y = y + prev_ref[:, vs].astype(F32)
        o_ref[:, vs] = y.astype(o_ref.dtype)


def _retention_dir(proj, tabs, direction, prev):
    B, T, _ = proj.shape
    rc = min(RET_BLOCK, T)
    nc = T // rc
    intra, qdec, kdec, cdec = tabs
    if direction == 0:
        cidx = lambda c: c
    else:
        cidx = lambda c: nc - 1 - c
    H = RET_HEADS
    v_blk = 2 * RET_QK_WIDTH // RET_V_WIDTH
    in_specs = [
        pl.BlockSpec((None, rc, RET_QK_WIDTH), lambda b, c: (b, cidx(c), 0)),
        pl.BlockSpec((None, rc, RET_QK_WIDTH), lambda b, c: (b, cidx(c), 1)),
        pl.BlockSpec((None, rc, RET_V_WIDTH), lambda b, c: (b, cidx(c), v_blk)),
        pl.BlockSpec((None, rc, RET_V_WIDTH), lambda b, c: (b, cidx(c), v_blk + 1 + direction)),
        pl.BlockSpec((H, rc, rc), lambda b, c: (direction, 0, 0)),
        pl.BlockSpec((H, rc, 1), lambda b, c: (direction, 0, 0)),
        pl.BlockSpec((H, rc, 1), lambda b, c: (direction, 0, 0)),
        pl.BlockSpec((H, 1, 1), lambda b, c: (direction, 0, 0)),
    ]
    args = [proj, proj, proj, proj, intra, qdec, kdec, cdec]
    if prev is not None:
        in_specs.append(pl.BlockSpec((None, rc, RET_V_WIDTH), lambda b, c: (b, cidx(c), 0)))
        args.append(prev)
    return pl.pallas_call(
        functools.partial(_retention_kernel, has_prev=prev is not None),
        grid=(B, nc),
        in_specs=in_specs,
        out_specs=pl.BlockSpec((None, rc, RET_V_WIDTH), lambda b, c: (b, cidx(c), 0)),
        out_shape=jax.ShapeDtypeStruct((B, T, RET_V_WIDTH), BF16),
        scratch_shapes=[pltpu.VMEM((H, RET_QK_DIM, RET_V_DIM), F32)],
        compiler_params=_cparams(("parallel", "arbitrary")),
        name="retention_bwd" if direction else "retention_fwd",
    )(*args)


def _retention_tables(decay_logit, rc):
    lg = jax.nn.log_sigmoid(decay_logit.astype(F32).reshape(-1))
    pos = jnp.arange(rc, dtype=F32)
    rel = pos[:, None] - pos[None, :]
    l3 = lg[:, None, None]
    fwd = jnp.where(rel >= 0, jnp.exp(jnp.maximum(rel, 0.0) * l3), 0.0)
    bwd = jnp.where(rel <= 0, jnp.exp(jnp.maximum(-rel, 0.0) * l3), 0.0)
    is_bwd = (jnp.arange(2 * RET_HEADS) >= RET_HEADS)
    intra = jnp.where(is_bwd[:, None, None], bwd, fwd)
    q_pow = jnp.where(is_bwd[:, None], rc - pos[None, :], pos[None, :] + 1.0)
    k_pow = jnp.where(is_bwd[:, None], pos[None, :], rc - 1.0 - pos[None, :])
    qdec = jnp.exp(q_pow * lg[:, None])[..., None]
    kdec = jnp.exp(k_pow * lg[:, None])[..., None]
    cdec = jnp.exp(rc * lg)[:, None, None]
    return intra, qdec, kdec, cdec


def _rotary_tables(T):
    d = RET_QK_DIM
    inv = 1.0 / (ROPE_BASE ** (jnp.arange(0, d, 2, dtype=F32) / d))
    ang = jnp.arange(T, dtype=F32)[:, None] * inv[None, :]
    return jnp.cos(ang), jnp.sin(ang)


def _na_kernel(q_ref, k_ref, v_ref, bias_ref, o_ref, *, rows):
    n_groups = rows // NA_ROWS_PER_GROUP
    lane = lax.broadcasted_iota(I32, (NA_Q, LANES), 1)
    first = lane < NA_HEAD_DIM

    def group(gi, carry):
        ustart = jnp.clip(gi * NA_ROWS_PER_GROUP - NA_KH // 2, 0, rows - NA_WIN_ROWS)
        pat = jnp.where(gi == 0, 0, jnp.where(gi == n_groups - 1, 2, 1))
        kstart = pl.multiple_of(ustart * GRID_W, GRID_W)
        qstart = pl.multiple_of(gi * NA_Q, NA_Q)
        kw = k_ref[pl.ds(kstart, NA_K), :]
        vw = v_ref[pl.ds(kstart, NA_K), :]
        q = q_ref[pl.ds(qstart, NA_Q), :].astype(F32)
        q2 = jnp.concatenate([jnp.where(first, q, 0.0), jnp.where(first, 0.0, q)], axis=0).astype(BF16)
        s = _dot_nt(q2, kw) + bias_ref[pat].reshape(2 * NA_Q, NA_K)
        m = jnp.max(s, axis=-1, keepdims=True)
        e = jnp.exp(s - m)
        l = jnp.sum(e, axis=-1, keepdims=True)
        o2 = _dot(e.astype(BF16), vw) / l
        o_ref[pl.ds(qstart, NA_Q), :] = jnp.where(first, o2[:NA_Q], o2[NA_Q:]).astype(o_ref.dtype)
        return carry

    lax.fori_loop(0, n_groups, group, 0, unroll=2)


def _na_attention(qkv, bias):
    B, T, _ = qkv.shape
    rows = T // GRID_W
    assert rows % NA_ROWS_PER_GROUP == 0 and rows >= NA_WIN_ROWS + 1
    n_pairs = D_MODEL // LANES
    blk = lambda off: pl.BlockSpec((None, T, LANES), lambda p, b: (b, 0, off + p))
    return pl.pallas_call(
        functools.partial(_na_kernel, rows=rows),
        grid=(n_pairs, B),
        in_specs=[
            blk(0), blk(n_pairs), blk(2 * n_pairs),
            pl.BlockSpec((3, 2, NA_Q, NA_K), lambda p, b: (0, p, 0, 0)),
        ],
        out_specs=pl.BlockSpec((None, T, LANES), lambda p, b: (b, 0, p)),
        out_shape=jax.ShapeDtypeStruct((B, T, D_MODEL), BF16),
        compiler_params=_cparams(("parallel", "parallel")),
        name="na_attention",
    )(qkv, qkv, qkv, bias)


def _na_bias_table(rpb):
    H = rpb.shape[0]
    r = rpb.astype(F32)
    pad = GRID_W
    rp = jnp.concatenate([jnp.repeat(r[..., :1], pad, -1), r, jnp.repeat(r[..., -1:], pad, -1)], -1)
    base = NA_KW - 1 + pad
    t1 = jnp.stack([rp[..., base - qc: base - qc + GRID_W] for qc in range(GRID_W)], axis=2)
    qc = np.arange(GRID_W)[:, None]
    kc = np.arange(GRID_W)[None, :]
    ws = np.clip(qc - NA_KW // 2, 0, GRID_W - NA_KW)
    t1 = jnp.where(jnp.asarray((kc >= ws) & (kc < ws + NA_KW)), t1, NEG_INF)
    masked = jnp.full((H, GRID_W, GRID_W), NEG_INF, F32)
    pats = []
    for p in range(3):
        per_row = []
        for i in range(NA_ROWS_PER_GROUP):
            lo = (0, i, NA_ROWS_PER_GROUP - 1)[p]
            qrow = (i, i + NA_KH // 2, i + NA_KH - 1)[p]
            us = [t1[:, u - qrow + NA_KH - 1] if lo <= u < lo + NA_KH else masked for u in range(NA_WIN_ROWS)]
            per_row.append(jnp.stack(us, axis=2))
        pats.append(jnp.stack(per_row, axis=1))
    return jnp.stack(pats, axis=0).reshape(3, H, NA_Q, NA_K)


def _select_kernel(aff_ref, pos_ref, pre_ref, *, cap, n_tok, rb):
    E = N_EXPERTS
    per_row = LANES // E
    rows = n_tok // per_row
    bits = pltpu.bitcast(aff_ref[...], I32)
    lane = lax.broadcasted_iota(I32, (rows, LANES), 1)
    tok = lax.broadcasted_iota(I32, (rows, LANES), 0) * per_row + lane // E

    def count(pred):
        c = jnp.sum(pred.astype(I32), axis=0, keepdims=True)
        c = jnp.broadcast_to(c, (8, LANES))
        sh = E
        while sh < LANES:
            c = c + pltpu.roll(c, sh, 1)
            sh *= 2
        return c[0:1]

    def thr_step(i, thr):
        cand = thr | jnp.left_shift(jnp.int32(1), 30 - i)
        return jnp.where(count(bits >= cand) >= cap, cand, thr)

    thr = lax.fori_loop(0, 31, thr_step, jnp.zeros((1, LANES), I32))
    gt = bits > thr
    eq = bits == thr
    need = cap - count(gt)

    nbits = max(1, int(math.ceil(math.log2(n_tok))))

    def tie_step(i, jmax):
        cand = jmax + jnp.left_shift(jnp.int32(1), nbits - 1 - i)
        return jnp.where(count(jnp.logical_and(eq, tok < cand)) < need, cand, jmax)

    jmax = lax.fori_loop(0, nbits, tie_step, jnp.zeros((1, LANES), I32))

    li = lax.broadcasted_iota(I32, (LANES, LANES), 0)
    lj = lax.broadcasted_iota(I32, (LANES, LANES), 1)
    same_e = (li % E) == (lj % E)
    a_before = _ones_where(jnp.logical_and(same_e, li // E < lj // E))
    a_all = _ones_where(same_e)
    ri = lax.broadcasted_iota(I32, (rb, rb), 0)
    rj = lax.broadcasted_iota(I32, (rb, rb), 1)
    lower = _ones_where(rj < ri)

    running = jnp.zeros((1, LANES), F32)
    for blk in range(rows // rb):
        sl = slice(blk * rb, (blk + 1) * rb)
        b_blk = pltpu.bitcast(aff_ref[sl, :], I32)
        t_blk = ((lax.broadcasted_iota(I32, (rb, LANES), 0) + blk * rb) * per_row
                 + lax.broadcasted_iota(I32, (rb, LANES), 1) // E)
        keep = jnp.logical_or(b_blk > thr, jnp.logical_and(b_blk == thr, t_blk <= jmax))
        kb = _ones_where(keep)
        row_tot = _dot(kb, a_all)
        within = _dot(lower, row_tot.astype(BF16))
        pre = running + within + _dot(kb, a_before)
        running = running + jnp.sum(row_tot, axis=0, keepdims=True)
        pre_i = pre.astype(I32)
        pre_ref[sl, :] = pre_i
        pos_ref[sl, :] = jnp.where(keep, pre_i, -1)


def _select(aff, cap):
    n_tok, E = aff.shape
    rows = n_tok * E // LANES
    rb = min(256, rows)
    aff8 = aff.reshape(rows, LANES)
    pos, pre = pl.pallas_call(
        functools.partial(_select_kernel, cap=cap, n_tok=n_tok, rb=rb),
        out_shape=[jax.ShapeDtypeStruct((rows, LANES), I32)] * 2,
        compiler_params=pltpu.CompilerParams(vmem_limit_bytes=VMEM_LIMIT),
        name="expert_select",
    )(aff8)
    return pos.reshape(n_tok, E), pre.reshape(n_tok, E)


DISPATCH_GROUP = 2


def _dispatch_slab(tn):
    return tn // 4 + BF16_SUBLANES


def _dispatch_rounds(tn):
    return -(-(tn + BF16_SUBLANES) // _dispatch_slab(tn))


def _dispatch_kernel(ws_ref, need_ref, pos_ref, aff_ref, h_ref, xe_ref, gs_ref, *, n_tiles, sub, batch):
    eg = pl.program_id(0)
    tb = pl.program_id(1)
    group, _, tn = pos_ref.shape
    head = BF16_SUBLANES
    slab = _dispatch_slab(tn)
    n_rounds = _dispatch_rounds(tn)

    @pl.when(tb == 0)
    def _():
        xe_ref[...] = jnp.zeros_like(xe_ref)
        gs_ref[...] = jnp.zeros_like(gs_ref)

    def one_round(s, r):
        t = tb * sub + s
        hs = h_ref[pl.ds(pl.multiple_of(s * tn, tn), tn), :]
        row = lax.broadcasted_iota(I32, (slab, tn), 0) + r * slab
        ws = [pl.multiple_of(ws_ref[(eg * group + k) * n_tiles + t], head) for k in range(group)]
        hits = [row == pos_ref[k, pl.ds(s, 1), :] - ws[k] for k in range(group)]
        rows = _dot(jnp.concatenate([_ones_where(hk) for hk in hits], axis=0), hs)
        for k in range(group):
            rk = rows[k * slab:(k + 1) * slab]
            gk = jnp.sum(jnp.where(hits[k], aff_ref[k, pl.ds(s, 1), :], 0.0), axis=1, keepdims=True)
            base = pl.multiple_of(ws[k] + r * slab, head)
            acc_rows = head if r == 0 else slab
            xe_ref[k, pl.ds(base, acc_rows), :] = (
                xe_ref[k, pl.ds(base, acc_rows), :].astype(F32) + rk[:acc_rows]).astype(xe_ref.dtype)
            gs_ref[k, pl.ds(base, acc_rows), :] = gs_ref[k, pl.ds(base, acc_rows), :] + gk[:acc_rows]
            if r == 0:
                rest = pl.multiple_of(base + head, head)
                xe_ref[k, pl.ds(rest, slab - head), :] = rk[head:].astype(xe_ref.dtype)
                gs_ref[k, pl.ds(rest, slab - head), :] = gk[head:]

    def tile_batch(sb, carry):
        for u in range(batch):
            one_round(sb * batch + u, 0)
        for u in range(batch):
            s = sb * batch + u
            need = need_ref[eg * n_tiles + tb * sub + s]
            for r in range(1, n_rounds):
                pl.when(need > r * slab)(functools.partial(one_round, s, r))
        return carry

    lax.fori_loop(0, sub // batch, tile_batch, 0)


def _dispatch(wstart, need, pos_t, aff_t, h, cap, tn):
    E, n_tiles, _ = pos_t.shape
    N, D = h.shape
    rows = cap + _dispatch_rounds(tn) * _dispatch_slab(tn)
    sub = 8 if n_tiles % 8 == 0 else n_tiles
    group = DISPATCH_GROUP
    resident = dict(pipeline_mode=pl.Buffered(1))
    grid_spec = pltpu.PrefetchScalarGridSpec(
        num_scalar_prefetch=2,
        grid=(E // group, n_tiles // sub),
        in_specs=[
            pl.BlockSpec((group, sub, tn), lambda e, t, ws, nd: (e, t, 0)),
            pl.BlockSpec((group, sub, tn), lambda e, t, ws, nd: (e, t, 0)),
            pl.BlockSpec((sub * tn, D), lambda e, t, ws, nd: (t, 0)),
        ],
        out_specs=[
            pl.BlockSpec((group, rows, D), lambda e, t, ws, nd: (e, 0, 0), **resident),
            pl.BlockSpec((group, rows, 1), lambda e, t, ws, nd: (e, 0, 0), **resident),
        ],
    )
    return pl.pallas_call(
        functools.partial(_dispatch_kernel, n_tiles=n_tiles, sub=sub, batch=4 if sub % 4 == 0 else 1),
        grid_spec=grid_spec,
        out_shape=[jax.ShapeDtypeStruct((E, rows, D), BF16), jax.ShapeDtypeStruct((E, rows, 1), F32)],
        compiler_params=_cparams(("parallel", "arbitrary")),
        name="moe_dispatch",
    )(wstart, need, pos_t, aff_t, h)


def _ffn_kernel(x_ref, gs_ref, wg_ref, wu_ref, wd_ref, o_ref, acc, *, n_f):
    f = pl.program_id(2)

    @pl.when(f == 0)
    def _():
        acc[...] = jnp.zeros_like(acc)

    x = x_ref[...]
    hid = _silu(_dot(x, wg_ref[...].astype(BF16))) * _dot(x, wu_ref[...].astype(BF16))
    acc[...] += _dot(hid.astype(BF16), wd_ref[...].astype(BF16))

    @pl.when(f == n_f - 1)
    def _():
        o_ref[...] = (acc[...] * gs_ref[...]).astype(o_ref.dtype)


def _expert_ffn(xe, gs, wg, wu, wd, layer, C):
    E, _, D = xe.shape
    F = wg.shape[-1]
    tm = _tile(C, 2048)
    tf = _tile(F, 512)
    n_f = F // tf
    return pl.pallas_call(
        functools.partial(_ffn_kernel, n_f=n_f),
        grid=(E, C // tm, n_f),
        in_specs=[
            pl.BlockSpec((None, tm, D), lambda e, m, f: (e, m, 0)),
            pl.BlockSpec((None, tm, 1), lambda e, m, f: (e, m, 0)),
            pl.BlockSpec((None, None, D, tf), lambda e, m, f: (layer, e, 0, f)),
            pl.BlockSpec((None, None, D, tf), lambda e, m, f: (layer, e, 0, f)),
            pl.BlockSpec((None, None, tf, D), lambda e, m, f: (layer, e, f, 0)),
        ],
        out_specs=pl.BlockSpec((None, tm, D), lambda e, m, f: (e, m, 0)),
        out_shape=jax.ShapeDtypeStruct((E, C, D), BF16),
        scratch_shapes=[pltpu.VMEM((tm, D), F32)],
        compiler_params=_cparams(("parallel", "parallel", "arbitrary")),
        name="expert_ffn",
    )(xe, gs, wg, wu, wd)


COMBINE_SLAB = LANES // 2
COMBINE_PARTS = 4


def _combine_kernel(ws_ref, end_ref, pos_ref, ye_any, *refs, n_tiles, n_exp, win):
    ye_refs = refs[:n_exp]
    x_ref, g_ref, o_ref = refs[n_exp:n_exp + 3]
    late_scr = refs[-1]
    parts = refs[n_exp + 3:-1]
    hit_scrs, y_scrs = parts[:len(parts) // 2], parts[len(parts) // 2:]
    per = n_exp // len(hit_scrs)
    t = pl.program_id(0)
    tn = pos_ref.shape[0]
    slab = ye_refs[0].shape[1]
    lane = lax.broadcasted_iota(I32, (tn, LANES), 1)
    low = lane < slab
    col = lane & (slab - 1)
    acc = None
    for gi, (hit_scr, y_scr) in enumerate(zip(hit_scrs, y_scrs)):
        for k in range(0, per, 2):
            e = gi * per + k
            prel = [pos_ref[:, e + i:e + i + 1] - ws_ref[(e + i) * n_tiles + t] for i in range(2)]
            hit_scr[:, k * slab:(k + 2) * slab] = _ones_where(col == jnp.where(low, prel[0], prel[1]))
            y_scr[k * slab:(k + 1) * slab, :] = ye_refs[e][0]
            y_scr[(k + 1) * slab:(k + 2) * slab, :] = ye_refs[e + 1][0]
        part = _dot(hit_scr[...], y_scr[...])
        acc = part if acc is None else acc + part
    o_ref[...] = x_ref[...] + g_ref[...] * acc

    if win > slab:
        for e in range(n_exp):
            ws = ws_ref[e * n_tiles + t]

            @pl.when(end_ref[e * n_tiles + t] - ws > slab)
            def _():
                late = pl.multiple_of(ws + slab, BF16_SUBLANES)
                pltpu.sync_copy(ye_any.at[e, pl.ds(late, win - slab), :], late_scr)
                prel = pos_ref[:, e:e + 1] - ws - slab
                hit = _ones_where(lax.broadcasted_iota(I32, (tn, win - slab), 1) == prel)
                o_ref[...] += g_ref[...] * _dot(hit, late_scr[...])


def _combine(wstart, wend, pos, ye, x, gate, tn):
    E, C, D = ye.shape
    B, T, _ = x.shape
    N = B * T
    n_tiles = N // tn
    tpb = T // tn
    win = tn + BF16_SUBLANES
    slab = min(COMBINE_SLAB, win)

    def window(e):
        return pl.BlockSpec((pl.Element(1), pl.Element(slab), pl.Element(D)),
                            lambda t, ws, we: (e, pl.multiple_of(ws[e * n_tiles + t], BF16_SUBLANES), 0))

    grid_spec = pltpu.PrefetchScalarGridSpec(
        num_scalar_prefetch=2,
        grid=(n_tiles,),
        in_specs=[pl.BlockSpec((tn, E), lambda t, ws, we: (t, 0)),
                  pl.BlockSpec(memory_space=pl.ANY)]
        + [window(e) for e in range(E)]
        + [pl.BlockSpec((None, tn, D), lambda t, ws, we: (t // tpb, t % tpb, 0)),
           pl.BlockSpec((None, 1, D), lambda t, ws, we: (t // tpb, 0, 0))],
        out_specs=pl.BlockSpec((None, tn, D), lambda t, ws, we: (t // tpb, t % tpb, 0)),
        scratch_shapes=[pltpu.VMEM((tn, E // COMBINE_PARTS * slab), BF16)] * COMBINE_PARTS
        + [pltpu.VMEM((E // COMBINE_PARTS * slab, D), BF16)] * COMBINE_PARTS
        + [pltpu.VMEM((max(win - slab, BF16_SUBLANES), D), BF16)],
    )
    return pl.pallas_call(
        functools.partial(_combine_kernel, n_tiles=n_tiles, n_exp=E, win=win),
        grid_spec=grid_spec,
        out_shape=jax.ShapeDtypeStruct((B, T, D), F32),
        compiler_params=_cparams(("parallel",)),
        name="moe_combine",
    )(wstart, wend, pos, ye, *([ye] * E), x, gate)


def _moe_tile(n_tok, cap):
    tn = 256
    while tn + BF16_SUBLANES > cap or n_tok % tn:
        tn //= 2
    return tn


def _moe_layer(x, h, aff, gate, wg, wu, wd, layer):
    B, T, D = x.shape
    N = B * T
    E = aff.shape[1]
    cap = EC_CAPACITY_FACTOR * N // E
    tn = _moe_tile(T, cap)
    win = tn + BF16_SUBLANES
    pos, pre = _select(aff, cap)
    start = pre[::tn].T
    aligned = (start // BF16_SUBLANES * BF16_SUBLANES).astype(I32)
    wdisp = aligned.reshape(-1)
    wstart = jnp.minimum(aligned, cap - win).reshape(-1)
    end = jnp.concatenate([start[:, 1:], jnp.full((E, 1), cap, I32)], axis=1).astype(I32)
    wend = end.reshape(-1)
    need = (end - aligned).reshape(E // DISPATCH_GROUP, DISPATCH_GROUP, -1).max(axis=1).reshape(-1)
    pos_t = pos.T.reshape(E, N // tn, tn)
    aff_t = aff.T.reshape(E, N // tn, tn)
    xe, gs = _dispatch(wdisp, need, pos_t, aff_t, h, cap, tn)
    ye = _expert_ffn(xe, gs, wg, wu, wd, layer, cap)
    return _combine(wstart, wend, pos, ye, x, gate, tn)


def _trunk(x, mod, p):
    B, T, D = x.shape
    rc = min(RET_BLOCK, T)
    cos, sin = _rotary_tables(T)
    for i in range(DEPTH):
        sh1, sc1, g1, sh2, sc2, g2 = [mod[i, :, k][:, None, :] for k in range(6)]
        j = i // 2
        if i % 2 == 0:
            proj = _proj(x, p["norm_mix_g"][i][None], sh1, sc1, p["ret_w_in"][j], rotary=(cos, sin))
            tabs = _retention_tables(p["ret_decay_logit"][j], rc)
            y_f = _retention_dir(proj, tabs, 0, None)
            y = _retention_dir(proj, tabs, 1, y_f)
            w_out = p["ret_w_out"][j]
        else:
            qkv = _proj(x, p["norm_mix_g"][i][None], sh1, sc1, p["na_w_in"][j],
                        qk=(p["na_bd"], p["na_gain"][j], 2 * D))
            y = _na_attention(qkv, p["na_bias"][j])
            w_out = p["na_w_out"][j]
        x, h, aff = _out_proj(y, w_out, x, g1, p["norm_ffn_g"][i][None], sh2, sc2, p["moe_router"][i])
        x = _moe_layer(x, h, aff, g2, p["moe_w_gate"], p["moe_w_up"], p["moe_w_down"], i)
    return x


def kernel(x_prompt, x_sample, c_prompt, c_sample, norm_mix_g, norm_ffn_g, ada_w, ada_b, ret_w_in, ret_decay_logit, ret_w_out, na_w_in, na_q_gain, na_k_gain, na_rpb, na_w_out, moe_router, moe_w_gate, moe_w_up, moe_w_down):
    D = D_MODEL
    bp, bs = c_prompt.shape[0], c_sample.shape[0]
    c_all = jnp.concatenate([c_prompt, c_sample], axis=0)
    pad = (-c_all.shape[0]) % 8
    if pad:
        c_all = jnp.pad(c_all, ((0, pad), (0, 0)))
    mod = _ada_mod(c_all, ada_w, ada_b).reshape(DEPTH, c_all.shape[0], 6, D)

    heads_row = lambda v: jnp.tile(v.astype(F32), (1, NA_HEADS))
    na_gain = jnp.concatenate(
        [heads_row(na_q_gain) * (NA_HEAD_DIM ** -0.5), heads_row(na_k_gain),
         jnp.ones((na_q_gain.shape[0], D), F32)], axis=-1)[:, None, :]
    bd = np.kron(np.eye(256 // NA_HEAD_DIM), np.ones((NA_HEAD_DIM, NA_HEAD_DIM)))
    p = dict(
        norm_mix_g=norm_mix_g, norm_ffn_g=norm_ffn_g,
        ret_w_in=ret_w_in.astype(BF16), ret_decay_logit=ret_decay_logit, ret_w_out=ret_w_out.astype(BF16),
        na_w_in=na_w_in.astype(BF16), na_w_out=na_w_out.astype(BF16),
        na_gain=na_gain, na_bd=jnp.asarray(bd, BF16),
        na_bias=jnp.stack([_na_bias_table(na_rpb[l]) for l in range(na_rpb.shape[0])]),
        moe_router=moe_router,
        moe_w_gate=moe_w_gate, moe_w_up=moe_w_up, moe_w_down=moe_w_down,
    )
    y_prompt = _trunk(x_prompt, mod[:, :bp], p)
    y_sample = _trunk(x_sample, mod[:, bp:bp + bs], p)
    return (y_prompt, y_sample)
```

```python
import functools
import math

import numpy as np
import jax
import jax.numpy as jnp
from jax import lax
from jax.experimental import pallas as pl
from jax.experimental.pallas import tpu as pltpu

F32 = jnp.float32
BF16 = jnp.bfloat16
I32 = jnp.int32

D_MODEL = 1024
DEPTH = 4

RET_HEADS = 4
RET_QK_DIM = 256
RET_V_DIM = 512
RET_QK_WIDTH = RET_HEADS * RET_QK_DIM
RET_V_WIDTH = RET_HEADS * RET_V_DIM
RET_IN_COLS = 2 * RET_QK_WIDTH + 3 * RET_V_WIDTH
ROPE_BASE = 10000.0
RET_BLOCK = 256

NA_HEADS = 16
NA_HEAD_DIM = 64
NA_KH = 8
NA_KW = 16
GRID_W = 64
NA_ROWS_PER_GROUP = 4
NA_WIN_ROWS = NA_ROWS_PER_GROUP + NA_KH - 1
NA_Q = NA_ROWS_PER_GROUP * GRID_W
NA_K = NA_WIN_ROWS * GRID_W

N_EXPERTS = 16
EC_CAPACITY_FACTOR = 2
D_FF = 2 * D_MODEL

EPS = 1e-6
NEG_INF = -1e30

LANES = 128
BF16_SUBLANES = 16
VMEM_LIMIT = 56 * 1024 * 1024

HIGHEST = lax.Precision.HIGHEST


def _cparams(sem):
    return pltpu.CompilerParams(dimension_semantics=sem, vmem_limit_bytes=VMEM_LIMIT)


def _tile(n, pref):
    t = min(pref, n)
    while n % t:
        t //= 2
    return t


def _silu(x):
    return x * (1.0 / (1.0 + jnp.exp(-x)))


def _ones_where(mask):
    return jnp.where(mask, 1.0, 0.0).astype(BF16)


def _dot(a, b):
    return jnp.dot(a, b, preferred_element_type=F32)


def _dot_nt(a, b):
    return lax.dot_general(a, b, (((1,), (1,)), ((), ())), preferred_element_type=F32)


def _dot_tn(a, b):
    return lax.dot_general(a, b, (((0,), (0,)), ((), ())), preferred_element_type=F32)


def _ada_kernel(c_ref, w_ref, b_ref, o_ref):
    ca = _silu(c_ref[...])
    o_ref[...] = jnp.dot(ca, w_ref[...], preferred_element_type=F32, precision=HIGHEST) + b_ref[...]


def _ada_mod(c_all, ada_w, ada_b):
    R, D = c_all.shape
    n_out = ada_w.shape[-1]
    tn = min(1536, n_out)
    return pl.pallas_call(
        _ada_kernel,
        grid=(DEPTH, n_out // tn),
        in_specs=[
            pl.BlockSpec((R, D), lambda l, j: (0, 0)),
            pl.BlockSpec((None, D, tn), lambda l, j: (l, 0, j)),
            pl.BlockSpec((None, 1, tn), lambda l, j: (l, 0, j)),
        ],
        out_specs=pl.BlockSpec((None, R, tn), lambda l, j: (l, 0, j)),
        out_shape=jax.ShapeDtypeStruct((DEPTH, R, n_out), F32),
        compiler_params=_cparams(("arbitrary", "arbitrary")),
        name="ada_mod",
    )(c_all, ada_w, ada_b.reshape(DEPTH, 1, n_out))


def _norm_mod(x, g, sh, sc):
    ms = jnp.mean(x * x, axis=-1, keepdims=True)
    y = x * lax.rsqrt(ms + EPS) * g
    return y * (1.0 + sc) + sh


def _proj_kernel(x_ref, g_ref, sh_ref, sc_ref, w_ref, o_ref, h_scr):
    @pl.when(pl.program_id(2) == 0)
    def _():
        h_scr[...] = _norm_mod(x_ref[...], g_ref[...], sh_ref[...], sc_ref[...]).astype(BF16)

    o_ref[...] = _dot(h_scr[...], w_ref[...]).astype(o_ref.dtype)


def _proj_qknorm_kernel(x_ref, g_ref, sh_ref, sc_ref, w_ref, bd_ref, gain_ref, o_ref, h_scr, *, qk_blocks, tn):
    j = pl.program_id(2)

    @pl.when(j == 0)
    def _():
        h_scr[...] = _norm_mod(x_ref[...], g_ref[...], sh_ref[...], sc_ref[...]).astype(BF16)

    acc = _dot(h_scr[...], w_ref[...])

    @pl.when(j < qk_blocks)
    def _():
        for c in range(tn // 256):
            a = acc[:, c * 256:(c + 1) * 256]
            ss = _dot((a * a).astype(BF16), bd_ref[...])
            r = lax.rsqrt(ss * (1.0 / NA_HEAD_DIM) + EPS)
            o_ref[:, c * 256:(c + 1) * 256] = (a * r * gain_ref[:, c * 256:(c + 1) * 256]).astype(o_ref.dtype)

    @pl.when(j >= qk_blocks)
    def _():
        o_ref[...] = acc.astype(o_ref.dtype)


def _proj_rotary_kernel(x_ref, g_ref, sh_ref, sc_ref, w_ref, cos_ref, sin_ref, o_ref, h_scr, *, tn):
    j = pl.program_id(2)

    @pl.when(j == 0)
    def _():
        h_scr[...] = _norm_mod(x_ref[...], g_ref[...], sh_ref[...], sc_ref[...]).astype(BF16)

    acc = _dot(h_scr[...], w_ref[...])
    qk_blocks = 2 * RET_QK_WIDTH // tn
    half = RET_QK_DIM // 2

    @pl.when(j < qk_blocks)
    def _():
        scale = jnp.where(j < qk_blocks // 2, 1.0, RET_QK_DIM ** -0.5)
        cos = cos_ref[...] * scale
        sin = sin_ref[...] * scale
        for c in range(tn // RET_QK_DIM):
            a1 = acc[:, c * RET_QK_DIM:c * RET_QK_DIM + half]
            a2 = acc[:, c * RET_QK_DIM + half:(c + 1) * RET_QK_DIM]
            o_ref[:, c * RET_QK_DIM:c * RET_QK_DIM + half] = (a1 * cos - a2 * sin).astype(o_ref.dtype)
            o_ref[:, c * RET_QK_DIM + half:(c + 1) * RET_QK_DIM] = (a1 * sin + a2 * cos).astype(o_ref.dtype)

    @pl.when(j >= qk_blocks)
    def _():
        o_ref[...] = acc.astype(o_ref.dtype)


def _proj(x, g, sh, sc, w, qk=None, rotary=None):
    B, T, D = x.shape
    n_out = w.shape[1]
    tm = _tile(T, 2048)
    tn = _tile(n_out, 1024)
    grid = (B, T // tm, n_out // tn)
    in_specs = [
        pl.BlockSpec((None, tm, D), lambda b, i, j: (b, i, 0)),
        pl.BlockSpec((1, D), lambda b, i, j: (0, 0)),
        pl.BlockSpec((None, 1, D), lambda b, i, j: (b, 0, 0)),
        pl.BlockSpec((None, 1, D), lambda b, i, j: (b, 0, 0)),
        pl.BlockSpec((D, tn), lambda b, i, j: (0, j)),
    ]
    args = [x, g, sh, sc, w]
    if rotary is not None:
        half = RET_QK_DIM // 2
        in_specs += [pl.BlockSpec((tm, half), lambda b, i, j: (i, 0))] * 2
        args += list(rotary)
        body = functools.partial(_proj_rotary_kernel, tn=tn)
        name = "proj_rotary"
    elif qk is None:
        body = _proj_kernel
        name = "proj"
    else:
        bd, gain, n_qk = qk
        in_specs += [
            pl.BlockSpec((256, 256), lambda b, i, j: (0, 0)),
            pl.BlockSpec((1, tn), lambda b, i, j: (0, j)),
        ]
        args += [bd, gain]
        body = functools.partial(_proj_qknorm_kernel, qk_blocks=n_qk // tn, tn=tn)
        name = "proj_qknorm"
    return pl.pallas_call(
        body,
        grid=grid,
        in_specs=in_specs,
        out_specs=pl.BlockSpec((None, tm, tn), lambda b, i, j: (b, i, j)),
        out_shape=jax.ShapeDtypeStruct((B, T, n_out), BF16),
        scratch_shapes=[pltpu.VMEM((tm, D), BF16)],
        compiler_params=_cparams(("parallel", "parallel", "arbitrary")),
        name=name,
    )(*args)


def _outproj_kernel(y_ref, w_ref, x_ref, g_ref, o_ref):
    o_ref[...] = x_ref[...] + g_ref[...] * _dot(y_ref[...], w_ref[...])


def _ffn_pre_kernel(x_ref, g_ref, sh_ref, sc_ref, r_ref, h_ref, aff_ref):
    h = _norm_mod(x_ref[...], g_ref[...], sh_ref[...], sc_ref[...])
    h_hi = h.astype(BF16)
    h_ref[...] = h_hi
    h_lo = (h - h_hi.astype(F32)).astype(BF16)
    r = r_ref[...]
    r_hi = r.astype(BF16)
    r_lo = (r - r_hi.astype(F32)).astype(BF16)
    logits = _dot(h_hi, r_hi) + (_dot(h_hi, r_lo) + _dot(h_lo, r_hi))
    m = jnp.max(logits, axis=-1, keepdims=True)
    e = jnp.exp(logits - m)
    aff_ref[...] = e / jnp.sum(e, axis=-1, keepdims=True)


def _out_proj(y, w, x, gate, ng, sh, sc, router):
    B, T, K = y.shape
    D = w.shape[1]
    E = router.shape[1]
    tm = _tile(T, 1024)
    xn = pl.pallas_call(
        _outproj_kernel,
        grid=(B, T // tm),
        in_specs=[
            pl.BlockSpec((None, tm, K), lambda b, i: (b, i, 0)),
            pl.BlockSpec((K, D), lambda b, i: (0, 0)),
            pl.BlockSpec((None, tm, D), lambda b, i: (b, i, 0)),
            pl.BlockSpec((None, 1, D), lambda b, i: (b, 0, 0)),
        ],
        out_specs=pl.BlockSpec((None, tm, D), lambda b, i: (b, i, 0)),
        out_shape=jax.ShapeDtypeStruct((B, T, D), F32),
        compiler_params=_cparams(("parallel", "parallel")),
        name="out_proj",
    )(y, w, x, gate)
    tp = _tile(T, 1024)
    nt = T // tp
    h, aff = pl.pallas_call(
        _ffn_pre_kernel,
        grid=(B, nt),
        in_specs=[
            pl.BlockSpec((None, tp, D), lambda b, i: (b, i, 0)),
            pl.BlockSpec((1, D), lambda b, i: (0, 0)),
            pl.BlockSpec((None, 1, D), lambda b, i: (b, 0, 0)),
            pl.BlockSpec((None, 1, D), lambda b, i: (b, 0, 0)),
            pl.BlockSpec((D, E), lambda b, i: (0, 0)),
        ],
        out_specs=[
            pl.BlockSpec((tp, D), lambda b, i: (b * nt + i, 0)),
            pl.BlockSpec((tp, E), lambda b, i: (b * nt + i, 0)),
        ],
        out_shape=[jax.ShapeDtypeStruct((B * T, D), BF16), jax.ShapeDtypeStruct((B * T, E), F32)],
        compiler_params=_cparams(("parallel", "parallel")),
        name="ffn_pre",
    )(xn, ng, sh, sc, router)
    return xn, h, aff


def _retention_kernel(*refs, has_prev):
    if has_prev:
        q_ref, k_ref, v_ref, gate_ref, intra_ref, qdec_ref, kdec_ref, cdec_ref, prev_ref, o_ref, state = refs
    else:
        q_ref, k_ref, v_ref, gate_ref, intra_ref, qdec_ref, kdec_ref, cdec_ref, o_ref, state = refs
        prev_ref = None

    @pl.when(pl.program_id(1) == 0)
    def _():
        state[...] = jnp.zeros_like(state)

    for h in range(RET_HEADS):
        qk = slice(h * RET_QK_DIM, (h + 1) * RET_QK_DIM)
        vs = slice(h * RET_V_DIM, (h + 1) * RET_V_DIM)
        qb = q_ref[:, qk]
        kb = k_ref[:, qk]
        v = v_ref[:, vs]
        scores = _dot_nt(qb, kb) * intra_ref[h]
        st = state[h]
        o = _dot(scores.astype(BF16), v) + _dot(qb, st.astype(BF16)) * qdec_ref[h]
        state[h] = st * cdec_ref[h] + _dot_tn((kb.astype(F32) * kdec_ref[h]).astype(BF16), v)

        mu = jnp.mean(o, axis=-1, keepdims=True)
        oc = o - mu
        var = jnp.mean(oc * oc, axis=-1, keepdims=True)
        y = oc * lax.rsqrt(var + EPS) * _silu(gate_ref[:, vs].astype(F32))
        if has_prev:
            y = y + prev_ref[:, vs].astype(F32)
        o_ref[:, vs] = y.astype(o_ref.dtype)


def _retention_dir(proj, tabs, direction, prev):
    B, T, _ = proj.shape
    rc = min(RET_BLOCK, T)
    nc = T // rc
    intra, qdec, kdec, cdec = tabs
    if direction == 0:
        cidx = lambda c: c
    else:
        cidx = lambda c: nc - 1 - c
    H = RET_HEADS
    v_blk = 2 * RET_QK_WIDTH // RET_V_WIDTH
    in_specs = [
        pl.BlockSpec((None, rc, RET_QK_WIDTH), lambda b, c: (b, cidx(c), 0)),
        pl.BlockSpec((None, rc, RET_QK_WIDTH), lambda b, c: (b, cidx(c), 1)),
        pl.BlockSpec((None, rc, RET_V_WIDTH), lambda b, c: (b, cidx(c), v_blk)),
        pl.BlockSpec((None, rc, RET_V_WIDTH), lambda b, c: (b, cidx(c), v_blk + 1 + direction)),
        pl.BlockSpec((H, rc, rc), lambda b, c: (direction, 0, 0)),
        pl.BlockSpec((H, rc, 1), lambda b, c: (direction, 0, 0)),
        pl.BlockSpec((H, rc, 1), lambda b, c: (direction, 0, 0)),
        pl.BlockSpec((H, 1, 1), lambda b, c: (direction, 0, 0)),
    ]
    args = [proj, proj, proj, proj, intra, qdec, kdec, cdec]
    if prev is not None:
        in_specs.append(pl.BlockSpec((None, rc, RET_V_WIDTH), lambda b, c: (b, cidx(c), 0)))
        args.append(prev)
    return pl.pallas_call(
        functools.partial(_retention_kernel, has_prev=prev is not None),
        grid=(B, nc),
        in_specs=in_specs,
        out_specs=pl.BlockSpec((None, rc, RET_V_WIDTH), lambda b, c: (b, cidx(c), 0)),
        out_shape=jax.ShapeDtypeStruct((B, T, RET_V_WIDTH), BF16),
        scratch_shapes=[pltpu.VMEM((H, RET_QK_DIM, RET_V_DIM), F32)],
        compiler_params=_cparams(("parallel", "arbitrary")),
        name="retention_bwd" if direction else "retention_fwd",
    )(*args)


def _retention_tables(decay_logit, rc):
    lg = jax.nn.log_sigmoid(decay_logit.astype(F32).reshape(-1))
    pos = jnp.arange(rc, dtype=F32)
    rel = pos[:, None] - pos[None, :]
    l3 = lg[:, None, None]
    fwd = jnp.where(rel >= 0, jnp.exp(jnp.maximum(rel, 0.0) * l3), 0.0)
    bwd = jnp.where(rel <= 0, jnp.exp(jnp.maximum(-rel, 0.0) * l3), 0.0)
    is_bwd = (jnp.arange(2 * RET_HEADS) >= RET_HEADS)
    intra = jnp.where(is_bwd[:, None, None], bwd, fwd)
    q_pow = jnp.where(is_bwd[:, None], rc - pos[None, :], pos[None, :] + 1.0)
    k_pow = jnp.where(is_bwd[:, None], pos[None, :], rc - 1.0 - pos[None, :])
    qdec = jnp.exp(q_pow * lg[:, None])[..., None]
    kdec = jnp.exp(k_pow * lg[:, None])[..., None]
    cdec = jnp.exp(rc * lg)[:, None, None]
    return intra, qdec, kdec, cdec


def _rotary_tables(T):
    d = RET_QK_DIM
    inv = 1.0 / (ROPE_BASE ** (jnp.arange(0, d, 2, dtype=F32) / d))
    ang = jnp.arange(T, dtype=F32)[:, None] * inv[None, :]
    return jnp.cos(ang), jnp.sin(ang)


def _na_kernel(q_ref, k_ref, v_ref, bias_ref, o_ref, *, rows):
    n_groups = rows // NA_ROWS_PER_GROUP
    lane = lax.broadcasted_iota(I32, (NA_Q, LANES), 1)
    first = lane < NA_HEAD_DIM

    def group(gi, carry):
        ustart = jnp.clip(gi * NA_ROWS_PER_GROUP - NA_KH // 2, 0, rows - NA_WIN_ROWS)
        pat = jnp.where(gi == 0, 0, jnp.where(gi == n_groups - 1, 2, 1))
        kstart = pl.multiple_of(ustart * GRID_W, GRID_W)
        qstart = pl.multiple_of(gi * NA_Q, NA_Q)
        kw = k_ref[pl.ds(kstart, NA_K), :]
        vw = v_ref[pl.ds(kstart, NA_K), :]
        q = q_ref[pl.ds(qstart, NA_Q), :].astype(F32)
        q2 = jnp.concatenate([jnp.where(first, q, 0.0), jnp.where(first, 0.0, q)], axis=0).astype(BF16)
        s = _dot_nt(q2, kw) + bias_ref[pat].reshape(2 * NA_Q, NA_K)
        m = jnp.max(s, axis=-1, keepdims=True)
        e = jnp.exp(s - m)
        l = jnp.sum(e, axis=-1, keepdims=True)
        o2 = _dot(e.astype(BF16), vw) / l
        o_ref[pl.ds(qstart, NA_Q), :] = jnp.where(first, o2[:NA_Q], o2[NA_Q:]).astype(o_ref.dtype)
        return carry

    lax.fori_loop(0, n_groups, group, 0, unroll=2)


def _na_attention(qkv, bias):
    B, T, _ = qkv.shape
    rows = T // GRID_W
    assert rows % NA_ROWS_PER_GROUP == 0 and rows >= NA_WIN_ROWS + 1
    n_pairs = D_MODEL // LANES
    blk = lambda off: pl.BlockSpec((None, T, LANES), lambda p, b: (b, 0, off + p))
    return pl.pallas_call(
        functools.partial(_na_kernel, rows=rows),
        grid=(n_pairs, B),
        in_specs=[
            blk(0), blk(n_pairs), blk(2 * n_pairs),
            pl.BlockSpec((3, 2, NA_Q, NA_K), lambda p, b: (0, p, 0, 0)),
        ],
        out_specs=pl.BlockSpec((None, T, LANES), lambda p, b: (b, 0, p)),
        out_shape=jax.ShapeDtypeStruct((B, T, D_MODEL), BF16),
        compiler_params=_cparams(("parallel", "parallel")),
        name="na_attention",
    )(qkv, qkv, qkv, bias)


def _na_bias_table(rpb):
    H = rpb.shape[0]
    r = rpb.astype(F32)
    pad = GRID_W
    rp = jnp.concatenate([jnp.repeat(r[..., :1], pad, -1), r, jnp.repeat(r[..., -1:], pad, -1)], -1)
    base = NA_KW - 1 + pad
    t1 = jnp.stack([rp[..., base - qc: base - qc + GRID_W] for qc in range(GRID_W)], axis=2)
    qc = np.arange(GRID_W)[:, None]
    kc = np.arange(GRID_W)[None, :]
    ws = np.clip(qc - NA_KW // 2, 0, GRID_W - NA_KW)
    t1 = jnp.where(jnp.asarray((kc >= ws) & (kc < ws + NA_KW)), t1, NEG_INF)
    masked = jnp.full((H, GRID_W, GRID_W), NEG_INF, F32)
    pats = []
    for p in range(3):
        per_row = []
        for i in range(NA_ROWS_PER_GROUP):
            lo = (0, i, NA_ROWS_PER_GROUP - 1)[p]
            qrow = (i, i + NA_KH // 2, i + NA_KH - 1)[p]
            us = [t1[:, u - qrow + NA_KH - 1] if lo <= u < lo + NA_KH else masked for u in range(NA_WIN_ROWS)]
            per_row.append(jnp.stack(us, axis=2))
        pats.append(jnp.stack(per_row, axis=1))
    return jnp.stack(pats, axis=0).reshape(3, H, NA_Q, NA_K)


def _select_kernel(aff_ref, pos_ref, pre_ref, *, cap, n_tok, rb):
    E = N_EXPERTS
    per_row = LANES // E
    rows = n_tok // per_row
    bits = pltpu.bitcast(aff_ref[...], I32)
    lane = lax.broadcasted_iota(I32, (rows, LANES), 1)
    tok = lax.broadcasted_iota(I32, (rows, LANES), 0) * per_row + lane // E

    def count(pred):
        c = jnp.sum(pred.astype(I32), axis=0, keepdims=True)
        c = jnp.broadcast_to(c, (8, LANES))
        sh = E
        while sh < LANES:
            c = c + pltpu.roll(c, sh, 1)
            sh *= 2
        return c[0:1]

    def thr_step(i, thr):
        cand = thr | jnp.left_shift(jnp.int32(1), 30 - i)
        return jnp.where(count(bits >= cand) >= cap, cand, thr)

    thr = lax.fori_loop(0, 31, thr_step, jnp.zeros((1, LANES), I32))
    gt = bits > thr
    eq = bits == thr
    need = cap - count(gt)

    nbits = max(1, int(math.ceil(math.log2(n_tok))))

    def tie_step(i, jmax):
        cand = jmax + jnp.left_shift(jnp.int32(1), nbits - 1 - i)
        return jnp.where(count(jnp.logical_and(eq, tok < cand)) < need, cand, jmax)

    jmax = lax.fori_loop(0, nbits, tie_step, jnp.zeros((1, LANES), I32))

    li = lax.broadcasted_iota(I32, (LANES, LANES), 0)
    lj = lax.broadcasted_iota(I32, (LANES, LANES), 1)
    same_e = (li % E) == (lj % E)
    a_before = _ones_where(jnp.logical_and(same_e, li // E < lj // E))
    a_all = _ones_where(same_e)
    ri = lax.broadcasted_iota(I32, (rb, rb), 0)
    rj = lax.broadcasted_iota(I32, (rb, rb), 1)
    lower = _ones_where(rj < ri)

    running = jnp.zeros((1, LANES), F32)
    for blk in range(rows // rb):
        sl = slice(blk * rb, (blk + 1) * rb)
        b_blk = pltpu.bitcast(aff_ref[sl, :], I32)
        t_blk = ((lax.broadcasted_iota(I32, (rb, LANES), 0) + blk * rb) * per_row
                 + lax.broadcasted_iota(I32, (rb, LANES), 1) // E)
        keep = jnp.logical_or(b_blk > thr, jnp.logical_and(b_blk == thr, t_blk <= jmax))
        kb = _ones_where(keep)
        row_tot = _dot(kb, a_all)
        within = _dot(lower, row_tot.astype(BF16))
        pre = running + within + _dot(kb, a_before)
        running = running + jnp.sum(row_tot, axis=0, keepdims=True)
        pre_i = pre.astype(I32)
        pre_ref[sl, :] = pre_i
        pos_ref[sl, :] = jnp.where(keep, pre_i, -1)


def _select(aff, cap):
    n_tok, E = aff.shape
    rows = n_tok * E // LANES
    rb = min(256, rows)
    aff8 = aff.reshape(rows, LANES)
    pos, pre = pl.pallas_call(
        functools.partial(_select_kernel, cap=cap, n_tok=n_tok, rb=rb),
        out_shape=[jax.ShapeDtypeStruct((rows, LANES), I32)] * 2,
        compiler_params=pltpu.CompilerParams(vmem_limit_bytes=VMEM_LIMIT),
        name="expert_select",
    )(aff8)
    return pos.reshape(n_tok, E), pre.reshape(n_tok, E)


DISPATCH_GROUP = 2


def _dispatch_slab(tn):
    return tn // 4 + BF16_SUBLANES


def _dispatch_rounds(tn):
    return -(-(tn + BF16_SUBLANES) // _dispatch_slab(tn))


def _dispatch_kernel(ws_ref, need_ref, pos_ref, aff_ref, h_ref, xe_ref, gs_ref, *, n_tiles, sub, batch):
    eg = pl.program_id(0)
    tb = pl.program_id(1)
    group, _, tn = pos_ref.shape
    head = BF16_SUBLANES
    slab = _dispatch_slab(tn)
    n_rounds = _dispatch_rounds(tn)

    @pl.when(tb == 0)
    def _():
        xe_ref[...] = jnp.zeros_like(xe_ref)
        gs_ref[...] = jnp.zeros_like(gs_ref)

    def one_round(s, r):
        t = tb * sub + s
        hs = h_ref[pl.ds(pl.multiple_of(s * tn, tn), tn), :]
        row = lax.broadcasted_iota(I32, (slab, tn), 0) + r * slab
        ws = [pl.multiple_of(ws_ref[(eg * group + k) * n_tiles + t], head) for k in range(group)]
        hits = [row == pos_ref[k, pl.ds(s, 1), :] - ws[k] for k in range(group)]
        rows = _dot(jnp.concatenate([_ones_where(hk) for hk in hits], axis=0), hs)
        for k in range(group):
            rk = rows[k * slab:(k + 1) * slab]
            gk = jnp.sum(jnp.where(hits[k], aff_ref[k, pl.ds(s, 1), :], 0.0), axis=1, keepdims=True)
            base = pl.multiple_of(ws[k] + r * slab, head)
            acc_rows = head if r == 0 else slab
            xe_ref[k, pl.ds(base, acc_rows), :] = (
                xe_ref[k, pl.ds(base, acc_rows), :].astype(F32) + rk[:acc_rows]).astype(xe_ref.dtype)
            gs_ref[k, pl.ds(base, acc_rows), :] = gs_ref[k, pl.ds(base, acc_rows), :] + gk[:acc_rows]
            if r == 0:
                rest = pl.multiple_of(base + head, head)
                xe_ref[k, pl.ds(rest, slab - head), :] = rk[head:].astype(xe_ref.dtype)
                gs_ref[k, pl.ds(rest, slab - head), :] = gk[head:]

    def tile_batch(sb, carry):
        for u in range(batch):
            one_round(sb * batch + u, 0)
        for u in range(batch):
            s = sb * batch + u
            need = need_ref[eg * n_tiles + tb * sub + s]
            for r in range(1, n_rounds):
                pl.when(need > r * slab)(functools.partial(one_round, s, r))
        return carry

    lax.fori_loop(0, sub // batch, tile_batch, 0)


def _dispatch(wstart, need, pos_t, aff_t, h, cap, tn):
    E, n_tiles, _ = pos_t.shape
    N, D = h.shape
    rows = cap + _dispatch_rounds(tn) * _dispatch_slab(tn)
    sub = 8 if n_tiles % 8 == 0 else n_tiles
    group = DISPATCH_GROUP
    resident = dict(pipeline_mode=pl.Buffered(1))
    grid_spec = pltpu.PrefetchScalarGridSpec(
        num_scalar_prefetch=2,
        grid=(E // group, n_tiles // sub),
        in_specs=[
            pl.BlockSpec((group, sub, tn), lambda e, t, ws, nd: (e, t, 0)),
            pl.BlockSpec((group, sub, tn), lambda e, t, ws, nd: (e, t, 0)),
            pl.BlockSpec((sub * tn, D), lambda e, t, ws, nd: (t, 0)),
        ],
        out_specs=[
            pl.BlockSpec((group, rows, D), lambda e, t, ws, nd: (e, 0, 0), **resident),
            pl.BlockSpec((group, rows, 1), lambda e, t, ws, nd: (e, 0, 0), **resident),
        ],
    )
    return pl.pallas_call(
        functools.partial(_dispatch_kernel, n_tiles=n_tiles, sub=sub, batch=4 if sub % 4 == 0 else 1),
        grid_spec=grid_spec,
        out_shape=[jax.ShapeDtypeStruct((E, rows, D), BF16), jax.ShapeDtypeStruct((E, rows, 1), F32)],
        compiler_params=_cparams(("parallel", "arbitrary")),
        name="moe_dispatch",
    )(wstart, need, pos_t, aff_t, h)


def _ffn_kernel(x_ref, gs_ref, wg_ref, wu_ref, wd_ref, o_ref, acc, *, n_f):
    f = pl.program_id(2)

    @pl.when(f == 0)
    def _():
        acc[...] = jnp.zeros_like(acc)

    x = x_ref[...]
    hid = _silu(_dot(x, wg_ref[...].astype(BF16))) * _dot(x, wu_ref[...].astype(BF16))
    acc[...] += _dot(hid.astype(BF16), wd_ref[...].astype(BF16))

    @pl.when(f == n_f - 1)
    def _():
        o_ref[...] = (acc[...] * gs_ref[...]).astype(o_ref.dtype)


def _expert_ffn(xe, gs, wg, wu, wd, layer, C):
    E, _, D = xe.shape
    F = wg.shape[-1]
    tm = _tile(C, 2048)
    tf = _tile(F, 512)
    n_f = F // tf
    return pl.pallas_call(
        functools.partial(_ffn_kernel, n_f=n_f),
        grid=(E, C // tm, n_f),
        in_specs=[
            pl.BlockSpec((None, tm, D), lambda e, m, f: (e, m, 0)),
            pl.BlockSpec((None, tm, 1), lambda e, m, f: (e, m, 0)),
            pl.BlockSpec((None, None, D, tf), lambda e, m, f: (layer, e, 0, f)),
            pl.BlockSpec((None, None, D, tf), lambda e, m, f: (layer, e, 0, f)),
            pl.BlockSpec((None, None, tf, D), lambda e, m, f: (layer, e, f, 0)),
        ],
        out_specs=pl.BlockSpec((None, tm, D), lambda e, m, f: (e, m, 0)),
        out_shape=jax.ShapeDtypeStruct((E, C, D), BF16),
        scratch_shapes=[pltpu.VMEM((tm, D), F32)],
        compiler_params=_cparams(("parallel", "parallel", "arbitrary")),
        name="expert_ffn",
    )(xe, gs, wg, wu, wd)


COMBINE_SLAB = LANES
COMBINE_PARTS = 4


def _combine_kernel(ws_ref, end_ref, pos_ref, ye_any, x_ref, g_ref, o_ref, *scratch, n_tiles, n_exp, win, slab):
    hit_scrs = scratch[:-3]
    y_buf, late_scr, sem = scratch[-3:]
    per = n_exp // len(hit_scrs)
    t = pl.program_id(0)
    tn = pos_ref.shape[0]
    slot = t % 2

    def window_copy(tile, buf, e):
        ws = pl.multiple_of(ws_ref[e * n_tiles + tile], BF16_SUBLANES)
        return pltpu.make_async_copy(ye_any.at[e, pl.ds(ws, slab), :],
                                     y_buf.at[buf, pl.ds(e * slab, slab), :], sem.at[buf])

    @pl.when(t == 0)
    def _():
        for e in range(n_exp):
            window_copy(0, 0, e).start()

    @pl.when(t + 1 < n_tiles)
    def _():
        for e in range(n_exp):
            window_copy(t + 1, 1 - slot, e).start()

    lane = lax.broadcasted_iota(I32, (tn, slab), 1)
    for gi, hit_scr in enumerate(hit_scrs):
        for k in range(per):
            e = gi * per + k
            prel = pos_ref[:, e:e + 1] - ws_ref[e * n_tiles + t]
            hit_scr[:, k * slab:(k + 1) * slab] = _ones_where(lane == prel)

    for e in range(n_exp):
        window_copy(t, slot, e).wait()

    acc = None
    for gi, hit_scr in enumerate(hit_scrs):
        part = _dot(hit_scr[...], y_buf[slot, gi * per * slab:(gi + 1) * per * slab, :])
        acc = part if acc is None else acc + part
    o_ref[...] = x_ref[...] + g_ref[...] * acc

    if win > slab:
        for e in range(n_exp):
            ws = ws_ref[e * n_tiles + t]

            @pl.when(end_ref[e * n_tiles + t] - ws > slab)
            def _():
                late = pl.multiple_of(ws + slab, BF16_SUBLANES)
                pltpu.sync_copy(ye_any.at[e, pl.ds(late, win - slab), :], late_scr)
                prel = pos_ref[:, e:e + 1] - ws - slab
                hit = _ones_where(lax.broadcasted_iota(I32, (tn, win - slab), 1) == prel)
                o_ref[...] += g_ref[...] * _dot(hit, late_scr[...])


def _combine(wstart, wend, pos, ye, x, gate, tn):
    E, C, D = ye.shape
    B, T, _ = x.shape
    N = B * T
    n_tiles = N // tn
    tpb = T // tn
    win = tn + BF16_SUBLANES
    slab = min(COMBINE_SLAB, win)
    grid_spec = pltpu.PrefetchScalarGridSpec(
        num_scalar_prefetch=2,
        grid=(n_tiles,),
        in_specs=[pl.BlockSpec((tn, E), lambda t, ws, we: (t, 0)),
                  pl.BlockSpec(memory_space=pl.ANY),
                  pl.BlockSpec((None, tn, D), lambda t, ws, we: (t // tpb, t % tpb, 0)),
                  pl.BlockSpec((None, 1, D), lambda t, ws, we: (t // tpb, 0, 0))],
        out_specs=pl.BlockSpec((None, tn, D), lambda t, ws, we: (t // tpb, t % tpb, 0)),
        scratch_shapes=[pltpu.VMEM((tn, E // COMBINE_PARTS * slab), BF16)] * COMBINE_PARTS
        + [pltpu.VMEM((2, E * slab, D), BF16),
           pltpu.VMEM((max(win - slab, BF16_SUBLANES), D), BF16),
           pltpu.SemaphoreType.DMA((2,))],
    )
    return pl.pallas_call(
        functools.partial(_combine_kernel, n_tiles=n_tiles, n_exp=E, win=win, slab=slab),
        grid_spec=grid_spec,
        out_shape=jax.ShapeDtypeStruct((B, T, D), F32),
        compiler_params=_cparams(("arbitrary",)),
        name="moe_combine",
    )(wstart, wend, pos, ye, x, gate)


def _moe_tile(n_tok, cap):
    tn = 256
    while tn + BF16_SUBLANES > cap or n_tok % tn:
        tn //= 2
    return tn


def _moe_layer(x, h, aff, gate, wg, wu, wd, layer):
    B, T, D = x.shape
    N = B * T
    E = aff.shape[1]
    cap = EC_CAPACITY_FACTOR * N // E
    tn = _moe_tile(T, cap)
    win = tn + BF16_SUBLANES
    pos, pre = _select(aff, cap)
    start = pre[::tn].T
    aligned = (start // BF16_SUBLANES * BF16_SUBLANES).astype(I32)
    wdisp = aligned.reshape(-1)
    wstart = jnp.minimum(aligned, cap - win).reshape(-1)
    end = jnp.concatenate([start[:, 1:], jnp.full((E, 1), cap, I32)], axis=1).astype(I32)
    wend = end.reshape(-1)
    need = (end - aligned).reshape(E // DISPATCH_GROUP, DISPATCH_GROUP, -1).max(axis=1).reshape(-1)
    pos_t = pos.T.reshape(E, N // tn, tn)
    aff_t = aff.T.reshape(E, N // tn, tn)
    xe, gs = _dispatch(wdisp, need, pos_t, aff_t, h, cap, tn)
    ye = _expert_ffn(xe, gs, wg, wu, wd, layer, cap)
    return _combine(wstart, wend, pos, ye, x, gate, tn)


def _trunk(x, mod, p):
    B, T, D = x.shape
    rc = min(RET_BLOCK, T)
    cos, sin = _rotary_tables(T)
    for i in range(DEPTH):
        sh1, sc1, g1, sh2, sc2, g2 = [mod[i, :, k][:, None, :] for k in range(6)]
        j = i // 2
        if i % 2 == 0:
            proj = _proj(x, p["norm_mix_g"][i][None], sh1, sc1, p["ret_w_in"][j], rotary=(cos, sin))
            tabs = _retention_tables(p["ret_decay_logit"][j], rc)
            y_f = _retention_dir(proj, tabs, 0, None)
            y = _retention_dir(proj, tabs, 1, y_f)
            w_out = p["ret_w_out"][j]
        else:
            qkv = _proj(x, p["norm_mix_g"][i][None], sh1, sc1, p["na_w_in"][j],
                        qk=(p["na_bd"], p["na_gain"][j], 2 * D))
            y = _na_attention(qkv, p["na_bias"][j])
            w_out = p["na_w_out"][j]
        x, h, aff = _out_proj(y, w_out, x, g1, p["norm_ffn_g"][i][None], sh2, sc2, p["moe_router"][i])
        x = _moe_layer(x, h, aff, g2, p["moe_w_gate"], p["moe_w_up"], p["moe_w_down"], i)
    return x


def kernel(x_prompt, x_sample, c_prompt, c_sample, norm_mix_g, norm_ffn_g, ada_w, ada_b, ret_w_in, ret_decay_logit, ret_w_out, na_w_in, na_q_gain, na_k_gain, na_rpb, na_w_out, moe_router, moe_w_gate, moe_w_up, moe_w_down):
    D = D_MODEL
    bp, bs = c_prompt.shape[0], c_sample.shape[0]
    c_all = jnp.concatenate([c_prompt, c_sample], axis=0)
    pad = (-c_all.shape[0]) % 8
    if pad:
        c_all = jnp.pad(c_all, ((0, pad), (0, 0)))
    mod = _ada_mod(c_all, ada_w, ada_b).reshape(DEPTH, c_all.shape[0], 6, D)

    heads_row = lambda v: jnp.tile(v.astype(F32), (1, NA_HEADS))
    na_gain = jnp.concatenate(
        [heads_row(na_q_gain) * (NA_HEAD_DIM ** -0.5), heads_row(na_k_gain),
         jnp.ones((na_q_gain.shape[0], D), F32)], axis=-1)[:, None, :]
    bd = np.kron(np.eye(256 // NA_HEAD_DIM), np.ones((NA_HEAD_DIM, NA_HEAD_DIM)))
    p = dict(
        norm_mix_g=norm_mix_g, norm_ffn_g=norm_ffn_g,
        ret_w_in=ret_w_in.astype(BF16), ret_decay_logit=ret_decay_logit, ret_w_out=ret_w_out.astype(BF16),
        na_w_in=na_w_in.astype(BF16), na_w_out=na_w_out.astype(BF16),
        na_gain=na_gain, na_bd=jnp.asarray(bd, BF16),
        na_bias=jnp.stack([_na_bias_table(na_rpb[l]) for l in range(na_rpb.shape[0])]),
        moe_router=moe_router,
        moe_w_gate=moe_w_gate, moe_w_up=moe_w_up, moe_w_down=moe_w_down,
    )
    y_prompt = _trunk(x_prompt, mod[:, :bp], p)
    y_sample = _trunk(x_sample, mod[:, bp:bp + bs], p)
    return (y_prompt, y_sample)
```

```python
import functools
import math

import numpy as np
import jax
import jax.numpy as jnp
from jax import lax
from jax.experimental import pallas as pl
from jax.experimental.pallas import tpu as pltpu

F32 = jnp.float32
BF16 = jnp.bfloat16
I32 = jnp.int32

D_MODEL = 1024
DEPTH = 4

RET_HEADS = 4
RET_QK_DIM = 256
RET_V_DIM = 512
RET_QK_WIDTH = RET_HEADS * RET_QK_DIM
RET_V_WIDTH = RET_HEADS * RET_V_DIM
RET_IN_COLS = 2 * RET_QK_WIDTH + 3 * RET_V_WIDTH
ROPE_BASE = 10000.0
RET_BLOCK = 256

NA_HEADS = 16
NA_HEAD_DIM = 64
NA_KH = 8
NA_KW = 16
GRID_W = 64
NA_ROWS_PER_GROUP = 4
NA_WIN_ROWS = NA_ROWS_PER_GROUP + NA_KH - 1
NA_Q = NA_ROWS_PER_GROUP * GRID_W
NA_K = NA_WIN_ROWS * GRID_W

N_EXPERTS = 16
EC_CAPACITY_FACTOR = 2
D_FF = 2 * D_MODEL

EPS = 1e-6
NEG_INF = -1e30

LANES = 128
BF16_SUBLANES = 16
VMEM_LIMIT = 56 * 1024 * 1024

HIGHEST = lax.Precision.HIGHEST


def _cparams(sem):
    return pltpu.CompilerParams(dimension_semantics=sem, vmem_limit_bytes=VMEM_LIMIT)


def _tile(n, pref):
    t = min(pref, n)
    while n % t:
        t //= 2
    return t


def _silu(x):
    return x * (1.0 / (1.0 + jnp.exp(-x)))


def _ones_where(mask):
    return jnp.where(mask, 1.0, 0.0).astype(BF16)


def _dot(a, b):
    return jnp.dot(a, b, preferred_element_type=F32)


def _dot_nt(a, b):
    return lax.dot_general(a, b, (((1,), (1,)), ((), ())), preferred_element_type=F32)


def _dot_tn(a, b):
    return lax.dot_general(a, b, (((0,), (0,)), ((), ())), preferred_element_type=F32)


def _ada_kernel(c_ref, w_ref, b_ref, o_ref):
    ca = _silu(c_ref[...])
    o_ref[...] = jnp.dot(ca, w_ref[...], preferred_element_type=F32, precision=HIGHEST) + b_ref[...]


def _ada_mod(c_all, ada_w, ada_b):
    R, D = c_all.shape
    n_out = ada_w.shape[-1]
    tn = min(1536, n_out)
    return pl.pallas_call(
        _ada_kernel,
        grid=(DEPTH, n_out // tn),
        in_specs=[
            pl.BlockSpec((R, D), lambda l, j: (0, 0)),
            pl.BlockSpec((None, D, tn), lambda l, j: (l, 0, j)),
            pl.BlockSpec((None, 1, tn), lambda l, j: (l, 0, j)),
        ],
        out_specs=pl.BlockSpec((None, R, tn), lambda l, j: (l, 0, j)),
        out_shape=jax.ShapeDtypeStruct((DEPTH, R, n_out), F32),
        compiler_params=_cparams(("arbitrary", "arbitrary")),
        name="ada_mod",
    )(c_all, ada_w, ada_b.reshape(DEPTH, 1, n_out))


def _norm_mod(x, g, sh, sc):
    ms = jnp.mean(x * x, axis=-1, keepdims=True)
    y = x * lax.rsqrt(ms + EPS) * g
    return y * (1.0 + sc) + sh


def _proj_kernel(x_ref, g_ref, sh_ref, sc_ref, w_ref, o_ref, h_scr):
    @pl.when(pl.program_id(2) == 0)
    def _():
        h_scr[...] = _norm_mod(x_ref[...], g_ref[...], sh_ref[...], sc_ref[...]).astype(BF16)

    o_ref[...] = _dot(h_scr[...], w_ref[...]).astype(o_ref.dtype)


def _proj_qknorm_kernel(x_ref, g_ref, sh_ref, sc_ref, w_ref, bd_ref, gain_ref, o_ref, h_scr, *, qk_blocks, tn):
    j = pl.program_id(2)

    @pl.when(j == 0)
    def _():
        h_scr[...] = _norm_mod(x_ref[...], g_ref[...], sh_ref[...], sc_ref[...]).astype(BF16)

    acc = _dot(h_scr[...], w_ref[...])

    @pl.when(j < qk_blocks)
    def _():
        for c in range(tn // 256):
            a = acc[:, c * 256:(c + 1) * 256]
            ss = _dot((a * a).astype(BF16), bd_ref[...])
            r = lax.rsqrt(ss * (1.0 / NA_HEAD_DIM) + EPS)
            o_ref[:, c * 256:(c + 1) * 256] = (a * r * gain_ref[:, c * 256:(c + 1) * 256]).astype(o_ref.dtype)

    @pl.when(j >= qk_blocks)
    def _():
        o_ref[...] = acc.astype(o_ref.dtype)


def _proj_rotary_kernel(x_ref, g_ref, sh_ref, sc_ref, w_ref, cos_ref, sin_ref, o_ref, h_scr, *, tn):
    j = pl.program_id(2)

    @pl.when(j == 0)
    def _():
        h_scr[...] = _norm_mod(x_ref[...], g_ref[...], sh_ref[...], sc_ref[...]).astype(BF16)

    acc = _dot(h_scr[...], w_ref[...])
    qk_blocks = 2 * RET_QK_WIDTH // tn
    half = RET_QK_DIM // 2

    @pl.when(j < qk_blocks)
    def _():
        scale = jnp.where(j < qk_blocks // 2, 1.0, RET_QK_DIM ** -0.5)
        cos = cos_ref[...] * scale
        sin = sin_ref[...] * scale
        for c in range(tn // RET_QK_DIM):
            a1 = acc[:, c * RET_QK_DIM:c * RET_QK_DIM + half]
            a2 = acc[:, c * RET_QK_DIM + half:(c + 1) * RET_QK_DIM]
            o_ref[:, c * RET_QK_DIM:c * RET_QK_DIM + half] = (a1 * cos - a2 * sin).astype(o_ref.dtype)
            o_ref[:, c * RET_QK_DIM + half:(c + 1) * RET_QK_DIM] = (a1 * sin + a2 * cos).astype(o_ref.dtype)

    @pl.when(j >= qk_blocks)
    def _():
        o_ref[...] = acc.astype(o_ref.dtype)


def _proj(x, g, sh, sc, w, qk=None, rotary=None):
    B, T, D = x.shape
    n_out = w.shape[1]
    tm = _tile(T, 2048)
    tn = _tile(n_out, 1024)
    grid = (B, T // tm, n_out // tn)
    in_specs = [
        pl.BlockSpec((None, tm, D), lambda b, i, j: (b, i, 0)),
        pl.BlockSpec((1, D), lambda b, i, j: (0, 0)),
        pl.BlockSpec((None, 1, D), lambda b, i, j: (b, 0, 0)),
        pl.BlockSpec((None, 1, D), lambda b, i, j: (b, 0, 0)),
        pl.BlockSpec((D, tn), lambda b, i, j: (0, j)),
    ]
    args = [x, g, sh, sc, w]
    if rotary is not None:
        half = RET_QK_DIM // 2
        in_specs += [pl.BlockSpec((tm, half), lambda b, i, j: (i, 0))] * 2
        args += list(rotary)
        body = functools.partial(_proj_rotary_kernel, tn=tn)
        name = "proj_rotary"
    elif qk is None:
        body = _proj_kernel
        name = "proj"
    else:
        bd, gain, n_qk = qk
        in_specs += [
            pl.BlockSpec((256, 256), lambda b, i, j: (0, 0)),
            pl.BlockSpec((1, tn), lambda b, i, j: (0, j)),
        ]
        args += [bd, gain]
        body = functools.partial(_proj_qknorm_kernel, qk_blocks=n_qk // tn, tn=tn)
        name = "proj_qknorm"
    return pl.pallas_call(
        body,
        grid=grid,
        in_specs=in_specs,
        out_specs=pl.BlockSpec((None, tm, tn), lambda b, i, j: (b, i, j)),
        out_shape=jax.ShapeDtypeStruct((B, T, n_out), BF16),
        scratch_shapes=[pltpu.VMEM((tm, D), BF16)],
        compiler_params=_cparams(("parallel", "parallel", "arbitrary")),
        name=name,
    )(*args)


def _outproj_kernel(y_ref, w_ref, x_ref, g_ref, o_ref):
    o_ref[...] = x_ref[...] + g_ref[...] * _dot(y_ref[...], w_ref[...])


def _ffn_pre_kernel(x_ref, g_ref, sh_ref, sc_ref, r_ref, h_ref, aff_ref):
    h = _norm_mod(x_ref[...], g_ref[...], sh_ref[...], sc_ref[...])
    h_hi = h.astype(BF16)
    h_ref[...] = h_hi
    h_lo = (h - h_hi.astype(F32)).astype(BF16)
    r = r_ref[...]
    r_hi = r.astype(BF16)
    r_lo = (r - r_hi.astype(F32)).astype(BF16)
    logits = _dot(h_hi, r_hi) + (_dot(h_hi, r_lo) + _dot(h_lo, r_hi))
    m = jnp.max(logits, axis=-1, keepdims=True)
    e = jnp.exp(logits - m)
    aff_ref[...] = e / jnp.sum(e, axis=-1, keepdims=True)


def _out_proj(y, w, x, gate, ng, sh, sc, router):
    B, T, K = y.shape
    D = w.shape[1]
    E = router.shape[1]
    tm = _tile(T, 1024)
    xn = pl.pallas_call(
        _outproj_kernel,
        grid=(B, T // tm),
        in_specs=[
            pl.BlockSpec((None, tm, K), lambda b, i: (b, i, 0)),
            pl.BlockSpec((K, D), lambda b, i: (0, 0)),
            pl.BlockSpec((None, tm, D), lambda b, i: (b, i, 0)),
            pl.BlockSpec((None, 1, D), lambda b, i: (b, 0, 0)),
        ],
        out_specs=pl.BlockSpec((None, tm, D), lambda b, i: (b, i, 0)),
        out_shape=jax.ShapeDtypeStruct((B, T, D), F32),
        compiler_params=_cparams(("parallel", "parallel")),
        name="out_proj",
    )(y, w, x, gate)
    tp = _tile(T, 1024)
    nt = T // tp
    h, aff = pl.pallas_call(
        _ffn_pre_kernel,
        grid=(B, nt),
        in_specs=[
            pl.BlockSpec((None, tp, D), lambda b, i: (b, i, 0)),
            pl.BlockSpec((1, D), lambda b, i: (0, 0)),
            pl.BlockSpec((None, 1, D), lambda b, i: (b, 0, 0)),
            pl.BlockSpec((None, 1, D), lambda b, i: (b, 0, 0)),
            pl.BlockSpec((D, E), lambda b, i: (0, 0)),
        ],
        out_specs=[
            pl.BlockSpec((tp, D), lambda b, i: (b * nt + i, 0)),
            pl.BlockSpec((tp, E), lambda b, i: (b * nt + i, 0)),
        ],
        out_shape=[jax.ShapeDtypeStruct((B * T, D), BF16), jax.ShapeDtypeStruct((B * T, E), F32)],
        compiler_params=_cparams(("parallel", "parallel")),
        name="ffn_pre",
    )(xn, ng, sh, sc, router)
    return xn, h, aff


def _retention_kernel(*refs, has_prev):
    if has_prev:
        q_ref, k_ref, v_ref, gate_ref, intra_ref, qdec_ref, kdec_ref, cdec_ref, prev_ref, o_ref, state = refs
    else:
        q_ref, k_ref, v_ref, gate_ref, intra_ref, qdec_ref, kdec_ref, cdec_ref, o_ref, state = refs
        prev_ref = None

    @pl.when(pl.program_id(1) == 0)
    def _():
        state[...] = jnp.zeros_like(state)

    for h in range(RET_HEADS):
        qk = slice(h * RET_QK_DIM, (h + 1) * RET_QK_DIM)
        vs = slice(h * RET_V_DIM, (h + 1) * RET_V_DIM)
        qb = q_ref[:, qk]
        kb = k_ref[:, qk]
        v = v_ref[:, vs]
        scores = _dot_nt(qb, kb) * intra_ref[h]
        st = state[h]
        o = _dot(scores.astype(BF16), v) + _dot(qb, st.astype(BF16)) * qdec_ref[h]
        state[h] = st * cdec_ref[h] + _dot_tn((kb.astype(F32) * kdec_ref[h]).astype(BF16), v)

        mu = jnp.mean(o, axis=-1, keepdims=True)
        oc = o - mu
        var = jnp.mean(oc * oc, axis=-1, keepdims=True)
        y = oc * lax.rsqrt(var + EPS) * _silu(gate_ref[:, vs].astype(F32))
        if has_prev:
            y = y + prev_ref[:, vs].astype(F32)
        o_ref[:, vs] = y.astype(o_ref.dtype)


def _retention_dir(proj, tabs, direction, prev):
    B, T, _ = proj.shape
    rc = min(RET_BLOCK, T)
    nc = T // rc
    intra, qdec, kdec, cdec = tabs
    if direction == 0:
        cidx = lambda c: c
    else:
        cidx = lambda c: nc - 1 - c
    H = RET_HEADS
    v_blk = 2 * RET_QK_WIDTH // RET_V_WIDTH
    in_specs = [
        pl.BlockSpec((None, rc, RET_QK_WIDTH), lambda b, c: (b, cidx(c), 0)),
        pl.BlockSpec((None, rc, RET_QK_WIDTH), lambda b, c: (b, cidx(c), 1)),
        pl.BlockSpec((None, rc, RET_V_WIDTH), lambda b, c: (b, cidx(c), v_blk)),
        pl.BlockSpec((None, rc, RET_V_WIDTH), lambda b, c: (b, cidx(c), v_blk + 1 + direction)),
        pl.BlockSpec((H, rc, rc), lambda b, c: (direction, 0, 0)),
        pl.BlockSpec((H, rc, 1), lambda b, c: (direction, 0, 0)),
        pl.BlockSpec((H, rc, 1), lambda b, c: (direction, 0, 0)),
        pl.BlockSpec((H, 1, 1), lambda b, c: (direction, 0, 0)),
    ]
    args = [proj, proj, proj, proj, intra, qdec, kdec, cdec]
    if prev is not None:
        in_specs.append(pl.BlockSpec((None, rc, RET_V_WIDTH), lambda b, c: (b, cidx(c), 0)))
        args.append(prev)
    return pl.pallas_call(
        functools.partial(_retention_kernel, has_prev=prev is not None),
        grid=(B, nc),
        in_specs=in_specs,
        out_specs=pl.BlockSpec((None, rc, RET_V_WIDTH), lambda b, c: (b, cidx(c), 0)),
        out_shape=jax.ShapeDtypeStruct((B, T, RET_V_WIDTH), BF16),
        scratch_shapes=[pltpu.VMEM((H, RET_QK_DIM, RET_V_DIM), F32)],
        compiler_params=_cparams(("parallel", "arbitrary")),
        name="retention_bwd" if direction else "retention_fwd",
    )(*args)


def _retention_tables(decay_logit, rc):
    lg = jax.nn.log_sigmoid(decay_logit.astype(F32).reshape(-1))
    pos = jnp.arange(rc, dtype=F32)
    rel = pos[:, None] - pos[None, :]
    l3 = lg[:, None, None]
    fwd = jnp.where(rel >= 0, jnp.exp(jnp.maximum(rel, 0.0) * l3), 0.0)
    bwd = jnp.where(rel <= 0, jnp.exp(jnp.maximum(-rel, 0.0) * l3), 0.0)
    is_bwd = (jnp.arange(2 * RET_HEADS) >= RET_HEADS)
    intra = jnp.where(is_bwd[:, None, None], bwd, fwd)
    q_pow = jnp.where(is_bwd[:, None], rc - pos[None, :], pos[None, :] + 1.0)
    k_pow = jnp.where(is_bwd[:, None], pos[None, :], rc - 1.0 - pos[None, :])
    qdec = jnp.exp(q_pow * lg[:, None])[..., None]
    kdec = jnp.exp(k_pow * lg[:, None])[..., None]
    cdec = jnp.exp(rc * lg)[:, None, None]
    return intra, qdec, kdec, cdec


def _rotary_tables(T):
    d = RET_QK_DIM
    inv = 1.0 / (ROPE_BASE ** (jnp.arange(0, d, 2, dtype=F32) / d))
    ang = jnp.arange(T, dtype=F32)[:, None] * inv[None, :]
    return jnp.cos(ang), jnp.sin(ang)


def _na_kernel(q_ref, k_ref, v_ref, bias_ref, o_ref, *, rows):
    n_groups = rows // NA_ROWS_PER_GROUP
    lane = lax.broadcasted_iota(I32, (NA_Q, LANES), 1)
    first = lane < NA_HEAD_DIM

    def group(gi, carry):
        ustart = jnp.clip(gi * NA_ROWS_PER_GROUP - NA_KH // 2, 0, rows - NA_WIN_ROWS)
        pat = jnp.where(gi == 0, 0, jnp.where(gi == n_groups - 1, 2, 1))
        kstart = pl.multiple_of(ustart * GRID_W, GRID_W)
        qstart = pl.multiple_of(gi * NA_Q, NA_Q)
        kw = k_ref[pl.ds(kstart, NA_K), :]
        vw = v_ref[pl.ds(kstart, NA_K), :]
        q = q_ref[pl.ds(qstart, NA_Q), :].astype(F32)
        q2 = jnp.concatenate([jnp.where(first, q, 0.0), jnp.where(first, 0.0, q)], axis=0).astype(BF16)
        s = _dot_nt(q2, kw) + bias_ref[pat].reshape(2 * NA_Q, NA_K)
        m = jnp.max(s, axis=-1, keepdims=True)
        e = jnp.exp(s - m)
        l = jnp.sum(e, axis=-1, keepdims=True)
        o2 = _dot(e.astype(BF16), vw) / l
        o_ref[pl.ds(qstart, NA_Q), :] = jnp.where(first, o2[:NA_Q], o2[NA_Q:]).astype(o_ref.dtype)
        return carry

    lax.fori_loop(0, n_groups, group, 0, unroll=8)


def _na_attention(qkv, bias):
    B, T, _ = qkv.shape
    rows = T // GRID_W
    assert rows % NA_ROWS_PER_GROUP == 0 and rows >= NA_WIN_ROWS + 1
    n_pairs = D_MODEL // LANES
    blk = lambda off: pl.BlockSpec((None, T, LANES), lambda p, b: (b, 0, off + p))
    return pl.pallas_call(
        functools.partial(_na_kernel, rows=rows),
        grid=(n_pairs, B),
        in_specs=[
            blk(0), blk(n_pairs), blk(2 * n_pairs),
            pl.BlockSpec((3, 2, NA_Q, NA_K), lambda p, b: (0, p, 0, 0)),
        ],
        out_specs=pl.BlockSpec((None, T, LANES), lambda p, b: (b, 0, p)),
        out_shape=jax.ShapeDtypeStruct((B, T, D_MODEL), BF16),
        compiler_params=_cparams(("parallel", "parallel")),
        name="na_attention",
    )(qkv, qkv, qkv, bias)


def _na_bias_table(rpb):
    H = rpb.shape[0]
    r = rpb.astype(F32)
    pad = GRID_W
    rp = jnp.concatenate([jnp.repeat(r[..., :1], pad, -1), r, jnp.repeat(r[..., -1:], pad, -1)], -1)
    base = NA_KW - 1 + pad
    t1 = jnp.stack([rp[..., base - qc: base - qc + GRID_W] for qc in range(GRID_W)], axis=2)
    qc = np.arange(GRID_W)[:, None]
    kc = np.arange(GRID_W)[None, :]
    ws = np.clip(qc - NA_KW // 2, 0, GRID_W - NA_KW)
    t1 = jnp.where(jnp.asarray((kc >= ws) & (kc < ws + NA_KW)), t1, NEG_INF)
    masked = jnp.full((H, GRID_W, GRID_W), NEG_INF, F32)
    pats = []
    for p in range(3):
        per_row = []
        for i in range(NA_ROWS_PER_GROUP):
            lo = (0, i, NA_ROWS_PER_GROUP - 1)[p]
            qrow = (i, i + NA_KH // 2, i + NA_KH - 1)[p]
            us = [t1[:, u - qrow + NA_KH - 1] if lo <= u < lo + NA_KH else masked for u in range(NA_WIN_ROWS)]
            per_row.append(jnp.stack(us, axis=2))
        pats.append(jnp.stack(per_row, axis=1))
    return jnp.stack(pats, axis=0).reshape(3, H, NA_Q, NA_K)


def _select_kernel(aff_ref, pos_ref, pre_ref, *, cap, n_tok, rb):
    E = N_EXPERTS
    per_row = LANES // E
    rows = n_tok // per_row
    bits = pltpu.bitcast(aff_ref[...], I32)
    lane = lax.broadcasted_iota(I32, (rows, LANES), 1)
    tok = lax.broadcasted_iota(I32, (rows, LANES), 0) * per_row + lane // E

    def count(pred):
        c = jnp.sum(pred.astype(I32), axis=0, keepdims=True)
        c = jnp.broadcast_to(c, (8, LANES))
        sh = E
        while sh < LANES:
            c = c + pltpu.roll(c, sh, 1)
            sh *= 2
        return c[0:1]

    def thr_step(i, thr):
        cand = thr | jnp.left_shift(jnp.int32(1), 30 - i)
        return jnp.where(count(bits >= cand) >= cap, cand, thr)

    thr = lax.fori_loop(0, 31, thr_step, jnp.zeros((1, LANES), I32))
    gt = bits > thr
    eq = bits == thr
    need = cap - count(gt)

    nbits = max(1, int(math.ceil(math.log2(n_tok))))

    def tie_step(i, jmax):
        cand = jmax + jnp.left_shift(jnp.int32(1), nbits - 1 - i)
        return jnp.where(count(jnp.logical_and(eq, tok < cand)) < need, cand, jmax)

    jmax = lax.fori_loop(0, nbits, tie_step, jnp.zeros((1, LANES), I32))

    li = lax.broadcasted_iota(I32, (LANES, LANES), 0)
    lj = lax.broadcasted_iota(I32, (LANES, LANES), 1)
    same_e = (li % E) == (lj % E)
    a_before = _ones_where(jnp.logical_and(same_e, li // E < lj // E))
    a_all = _ones_where(same_e)
    ri = lax.broadcasted_iota(I32, (rb, rb), 0)
    rj = lax.broadcasted_iota(I32, (rb, rb), 1)
    lower = _ones_where(rj < ri)

    running = jnp.zeros((1, LANES), F32)
    for blk in range(rows // rb):
        sl = slice(blk * rb, (blk + 1) * rb)
        b_blk = pltpu.bitcast(aff_ref[sl, :], I32)
        t_blk = ((lax.broadcasted_iota(I32, (rb, LANES), 0) + blk * rb) * per_row
                 + lax.broadcasted_iota(I32, (rb, LANES), 1) // E)
        keep = jnp.logical_or(b_blk > thr, jnp.logical_and(b_blk == thr, t_blk <= jmax))
        kb = _ones_where(keep)
        row_tot = _dot(kb, a_all)
        within = _dot(lower, row_tot.astype(BF16))
        pre = running + within + _dot(kb, a_before)
        running = running + jnp.sum(row_tot, axis=0, keepdims=True)
        pre_i = pre.astype(I32)
        pre_ref[sl, :] = pre_i
        pos_ref[sl, :] = jnp.where(keep, pre_i, -1)


def _select(aff, cap):
    n_tok, E = aff.shape
    rows = n_tok * E // LANES
    rb = min(256, rows)
    aff8 = aff.reshape(rows, LANES)
    pos, pre = pl.pallas_call(
        functools.partial(_select_kernel, cap=cap, n_tok=n_tok, rb=rb),
        out_shape=[jax.ShapeDtypeStruct((rows, LANES), I32)] * 2,
        compiler_params=pltpu.CompilerParams(vmem_limit_bytes=VMEM_LIMIT),
        name="expert_select",
    )(aff8)
    return pos.reshape(n_tok, E), pre.reshape(n_tok, E)


DISPATCH_GROUP = 2


def _dispatch_slab(tn):
    return tn // 4 + BF16_SUBLANES


def _dispatch_rounds(tn):
    return -(-(tn + BF16_SUBLANES) // _dispatch_slab(tn))


def _dispatch_kernel(ws_ref, need_ref, pos_ref, aff_ref, h_ref, xe_ref, gs_ref, *, n_tiles, sub, batch):
    eg = pl.program_id(0)
    tb = pl.program_id(1)
    group, _, tn = pos_ref.shape
    head = BF16_SUBLANES
    slab = _dispatch_slab(tn)
    n_rounds = _dispatch_rounds(tn)

    @pl.when(tb == 0)
    def _():
        xe_ref[...] = jnp.zeros_like(xe_ref)
        gs_ref[...] = jnp.zeros_like(gs_ref)

    def one_round(s, r):
        t = tb * sub + s
        hs = h_ref[pl.ds(pl.multiple_of(s * tn, tn), tn), :]
        row = lax.broadcasted_iota(I32, (slab, tn), 0) + r * slab
        ws = [pl.multiple_of(ws_ref[(eg * group + k) * n_tiles + t], head) for k in range(group)]
        hits = [row == pos_ref[k, pl.ds(s, 1), :] - ws[k] for k in range(group)]
        rows = _dot(jnp.concatenate([_ones_where(hk) for hk in hits], axis=0), hs)
        for k in range(group):
            rk = rows[k * slab:(k + 1) * slab]
            gk = jnp.sum(jnp.where(hits[k], aff_ref[k, pl.ds(s, 1), :], 0.0), axis=1, keepdims=True)
            base = pl.multiple_of(ws[k] + r * slab, head)
            acc_rows = head if r == 0 else slab
            xe_ref[k, pl.ds(base, acc_rows), :] = (
                xe_ref[k, pl.ds(base, acc_rows), :].astype(F32) + rk[:acc_rows]).astype(xe_ref.dtype)
            gs_ref[k, pl.ds(base, acc_rows), :] = gs_ref[k, pl.ds(base, acc_rows), :] + gk[:acc_rows]
            if r == 0:
                rest = pl.multiple_of(base + head, head)
                xe_ref[k, pl.ds(rest, slab - head), :] = rk[head:].astype(xe_ref.dtype)
                gs_ref[k, pl.ds(rest, slab - head), :] = gk[head:]

    def tile_batch(sb, carry):
        for u in range(batch):
            one_round(sb * batch + u, 0)
        for u in range(batch):
            s = sb * batch + u
            need = need_ref[eg * n_tiles + tb * sub + s]
            for r in range(1, n_rounds):
                pl.when(need > r * slab)(functools.partial(one_round, s, r))
        return carry

    lax.fori_loop(0, sub // batch, tile_batch, 0)


def _dispatch(wstart, need, pos_t, aff_t, h, cap, tn):
    E, n_tiles, _ = pos_t.shape
    N, D = h.shape
    rows = cap + _dispatch_rounds(tn) * _dispatch_slab(tn)
    sub = 8 if n_tiles % 8 == 0 else n_tiles
    group = DISPATCH_GROUP
    resident = dict(pipeline_mode=pl.Buffered(1))
    grid_spec = pltpu.PrefetchScalarGridSpec(
        num_scalar_prefetch=2,
        grid=(E // group, n_tiles // sub),
        in_specs=[
            pl.BlockSpec((group, sub, tn), lambda e, t, ws, nd: (e, t, 0)),
            pl.BlockSpec((group, sub, tn), lambda e, t, ws, nd: (e, t, 0)),
            pl.BlockSpec((sub * tn, D), lambda e, t, ws, nd: (t, 0)),
        ],
        out_specs=[
            pl.BlockSpec((group, rows, D), lambda e, t, ws, nd: (e, 0, 0), **resident),
            pl.BlockSpec((group, rows, 1), lambda e, t, ws, nd: (e, 0, 0), **resident),
        ],
    )
    return pl.pallas_call(
        functools.partial(_dispatch_kernel, n_tiles=n_tiles, sub=sub, batch=8 if sub % 8 == 0 else 1),
        grid_spec=grid_spec,
        out_shape=[jax.ShapeDtypeStruct((E, rows, D), BF16), jax.ShapeDtypeStruct((E, rows, 1), F32)],
        compiler_params=_cparams(("parallel", "arbitrary")),
        name="moe_dispatch",
    )(wstart, need, pos_t, aff_t, h)


def _ffn_kernel(x_ref, gs_ref, wg_ref, wu_ref, wd_ref, o_ref, acc, *, n_f):
    f = pl.program_id(2)

    @pl.when(f == 0)
    def _():
        acc[...] = jnp.zeros_like(acc)

    x = x_ref[...]
    hid = _silu(_dot(x, wg_ref[...].astype(BF16))) * _dot(x, wu_ref[...].astype(BF16))
    acc[...] += _dot(hid.astype(BF16), wd_ref[...].astype(BF16))

    @pl.when(f == n_f - 1)
    def _():
        o_ref[...] = (acc[...] * gs_ref[...]).astype(o_ref.dtype)


def _expert_ffn(xe, gs, wg, wu, wd, layer, C):
    E, _, D = xe.shape
    F = wg.shape[-1]
    tm = _tile(C, 2048)
    tf = _tile(F, 512)
    n_f = F // tf
    return pl.pallas_call(
        functools.partial(_ffn_kernel, n_f=n_f),
        grid=(E, C // tm, n_f),
        in_specs=[
            pl.BlockSpec((None, tm, D), lambda e, m, f: (e, m, 0)),
            pl.BlockSpec((None, tm, 1), lambda e, m, f: (e, m, 0)),
            pl.BlockSpec((None, None, D, tf), lambda e, m, f: (layer, e, 0, f)),
            pl.BlockSpec((None, None, D, tf), lambda e, m, f: (layer, e, 0, f)),
            pl.BlockSpec((None, None, tf, D), lambda e, m, f: (layer, e, f, 0)),
        ],
        out_specs=pl.BlockSpec((None, tm, D), lambda e, m, f: (e, m, 0)),
        out_shape=jax.ShapeDtypeStruct((E, C, D), BF16),
        scratch_shapes=[pltpu.VMEM((tm, D), F32)],
        compiler_params=_cparams(("parallel", "parallel", "arbitrary")),
        name="expert_ffn",
    )(xe, gs, wg, wu, wd)


COMBINE_SLAB = LANES
COMBINE_PARTS = 4


def _combine_kernel(ws_ref, end_ref, pos_ref, ye_any, x_ref, g_ref, o_ref, *scratch, n_tiles, n_exp, win, slab):
    hit_scrs = scratch[:-3]
    y_buf, late_scr, sem = scratch[-3:]
    per = n_exp // len(hit_scrs)
    t = pl.program_id(0)
    tn = pos_ref.shape[0]
    slot = t % 2

    def window_copy(tile, buf, e):
        ws = pl.multiple_of(ws_ref[e * n_tiles + tile], BF16_SUBLANES)
        return pltpu.make_async_copy(ye_any.at[e, pl.ds(ws, slab), :],
                                     y_buf.at[buf, pl.ds(e * slab, slab), :], sem.at[buf])

    @pl.when(t == 0)
    def _():
        for e in range(n_exp):
            window_copy(0, 0, e).start()

    @pl.when(t + 1 < n_tiles)
    def _():
        for e in range(n_exp):
            window_copy(t + 1, 1 - slot, e).start()

    lane = lax.broadcasted_iota(I32, (tn, slab), 1)
    for gi, hit_scr in enumerate(hit_scrs):
        for k in range(per):
            e = gi * per + k
            prel = pos_ref[:, e:e + 1] - ws_ref[e * n_tiles + t]
            hit_scr[:, k * slab:(k + 1) * slab] = _ones_where(lane == prel)

    for e in range(n_exp):
        window_copy(t, slot, e).wait()

    acc = None
    for gi, hit_scr in enumerate(hit_scrs):
        part = _dot(hit_scr[...], y_buf[slot, gi * per * slab:(gi + 1) * per * slab, :])
        acc = part if acc is None else acc + part
    o_ref[...] = x_ref[...] + g_ref[...] * acc

    if win > slab:
        for e in range(n_exp):
            ws = ws_ref[e * n_tiles + t]

            @pl.when(end_ref[e * n_tiles + t] - ws > slab)
            def _():
                late = pl.multiple_of(ws + slab, BF16_SUBLANES)
                pltpu.sync_copy(ye_any.at[e, pl.ds(late, win - slab), :], late_scr)
                prel = pos_ref[:, e:e + 1] - ws - slab
                hit = _ones_where(lax.broadcasted_iota(I32, (tn, win - slab), 1) == prel)
                o_ref[...] += g_ref[...] * _dot(hit, late_scr[...])


def _combine(wstart, wend, pos, ye, x, gate, tn):
    E, C, D = ye.shape
    B, T, _ = x.shape
    N = B * T
    n_tiles = N // tn
    tpb = T // tn
    win = tn + BF16_SUBLANES
    slab = min(COMBINE_SLAB, win)
    grid_spec = pltpu.PrefetchScalarGridSpec(
        num_scalar_prefetch=2,
        grid=(n_tiles,),
        in_specs=[pl.BlockSpec((tn, E), lambda t, ws, we: (t, 0)),
                  pl.BlockSpec(memory_space=pl.ANY),
                  pl.BlockSpec((None, tn, D), lambda t, ws, we: (t // tpb, t % tpb, 0)),
                  pl.BlockSpec((None, 1, D), lambda t, ws, we: (t // tpb, 0, 0))],
        out_specs=pl.BlockSpec((None, tn, D), lambda t, ws, we: (t // tpb, t % tpb, 0)),
        scratch_shapes=[pltpu.VMEM((tn, E // COMBINE_PARTS * slab), BF16)] * COMBINE_PARTS
        + [pltpu.VMEM((2, E * slab, D), BF16),
           pltpu.VMEM((max(win - slab, BF16_SUBLANES), D), BF16),
           pltpu.SemaphoreType.DMA((2,))],
    )
    return pl.pallas_call(
        functools.partial(_combine_kernel, n_tiles=n_tiles, n_exp=E, win=win, slab=slab),
        grid_spec=grid_spec,
        out_shape=jax.ShapeDtypeStruct((B, T, D), F32),
        compiler_params=_cparams(("arbitrary",)),
        name="moe_combine",
    )(wstart, wend, pos, ye, x, gate)


def _moe_tile(n_tok, cap):
    tn = 256
    while tn + BF16_SUBLANES > cap or n_tok % tn:
        tn //= 2
    return tn


def _moe_layer(x, h, aff, gate, wg, wu, wd, layer):
    B, T, D = x.shape
    N = B * T
    E = aff.shape[1]
    cap = EC_CAPACITY_FACTOR * N // E
    tn = _moe_tile(T, cap)
    win = tn + BF16_SUBLANES
    pos, pre = _select(aff, cap)
    start = pre[::tn].T
    aligned = (start // BF16_SUBLANES * BF16_SUBLANES).astype(I32)
    wdisp = aligned.reshape(-1)
    wstart = jnp.minimum(aligned, cap - win).reshape(-1)
    end = jnp.concatenate([start[:, 1:], jnp.full((E, 1), cap, I32)], axis=1).astype(I32)
    wend = end.reshape(-1)
    need = (end - aligned).reshape(E // DISPATCH_GROUP, DISPATCH_GROUP, -1).max(axis=1).reshape(-1)
    pos_t = pos.T.reshape(E, N // tn, tn)
    aff_t = aff.T.reshape(E, N // tn, tn)
    xe, gs = _dispatch(wdisp, need, pos_t, aff_t, h, cap, tn)
    ye = _expert_ffn(xe, gs, wg, wu, wd, layer, cap)
    return _combine(wstart, wend, pos, ye, x, gate, tn)


def _trunk(x, mod, p):
    B, T, D = x.shape
    rc = min(RET_BLOCK, T)
    cos, sin = _rotary_tables(T)
    for i in range(DEPTH):
        sh1, sc1, g1, sh2, sc2, g2 = [mod[i, :, k][:, None, :] for k in range(6)]
        j = i // 2
        if i % 2 == 0:
            proj = _proj(x, p["norm_mix_g"][i][None], sh1, sc1, p["ret_w_in"][j], rotary=(cos, sin))
            tabs = _retention_tables(p["ret_decay_logit"][j], rc)
            y_f = _retention_dir(proj, tabs, 0, None)
            y = _retention_dir(proj, tabs, 1, y_f)
            w_out = p["ret_w_out"][j]
        else:
            qkv = _proj(x, p["norm_mix_g"][i][None], sh1, sc1, p["na_w_in"][j],
                        qk=(p["na_bd"], p["na_gain"][j], 2 * D))
            y = _na_attention(qkv, p["na_bias"][j])
            w_out = p["na_w_out"][j]
        x, h, aff = _out_proj(y, w_out, x, g1, p["norm_ffn_g"][i][None], sh2, sc2, p["moe_router"][i])
        x = _moe_layer(x, h, aff, g2, p["moe_w_gate"], p["moe_w_up"], p["moe_w_down"], i)
    return x


def kernel(x_prompt, x_sample, c_prompt, c_sample, norm_mix_g, norm_ffn_g, ada_w, ada_b, ret_w_in, ret_decay_logit, ret_w_out, na_w_in, na_q_gain, na_k_gain, na_rpb, na_w_out, moe_router, moe_w_gate, moe_w_up, moe_w_down):
    D = D_MODEL
    bp, bs = c_prompt.shape[0], c_sample.shape[0]
    c_all = jnp.concatenate([c_prompt, c_sample], axis=0)
    pad = (-c_all.shape[0]) % 8
    if pad:
        c_all = jnp.pad(c_all, ((0, pad), (0, 0)))
    mod = _ada_mod(c_all, ada_w, ada_b).reshape(DEPTH, c_all.shape[0], 6, D)

    heads_row = lambda v: jnp.tile(v.astype(F32), (1, NA_HEADS))
    na_gain = jnp.concatenate(
        [heads_row(na_q_gain) * (NA_HEAD_DIM ** -0.5), heads_row(na_k_gain),
         jnp.ones((na_q_gain.shape[0], D), F32)], axis=-1)[:, None, :]
    bd = np.kron(np.eye(256 // NA_HEAD_DIM), np.ones((NA_HEAD_DIM, NA_HEAD_DIM)))
    p = dict(
        norm_mix_g=norm_mix_g, norm_ffn_g=norm_ffn_g,
        ret_w_in=ret_w_in.astype(BF16), ret_decay_logit=ret_decay_logit, ret_w_out=ret_w_out.astype(BF16),
        na_w_in=na_w_in.astype(BF16), na_w_out=na_w_out.astype(BF16),
        na_gain=na_gain, na_bd=jnp.asarray(bd, BF16),
        na_bias=jnp.stack([_na_bias_table(na_rpb[l]) for l in range(na_rpb.shape[0])]),
        moe_router=moe_router,
        moe_w_gate=moe_w_gate, moe_w_up=moe_w_up, moe_w_down=moe_w_down,
    )
    y_prompt = _trunk(x_prompt, mod[:, :bp], p)
    y_sample = _trunk(x_sample, mod[:, bp:bp + bs], p)
    return (y_prompt, y_sample)
```

```python
import functools
import math

import numpy as np
import jax
import jax.numpy as jnp
from jax import lax
from jax.experimental import pallas as pl
from jax.experimental.pallas import tpu as pltpu

F32 = jnp.float32
BF16 = jnp.bfloat16
I32 = jnp.int32

D_MODEL = 1024
DEPTH = 4

RET_HEADS = 4
RET_QK_DIM = 256
RET_V_DIM = 512
RET_QK_WIDTH = RET_HEADS * RET_QK_DIM
RET_V_WIDTH = RET_HEADS * RET_V_DIM
RET_IN_COLS = 2 * RET_QK_WIDTH + 3 * RET_V_WIDTH
ROPE_BASE = 10000.0
RET_BLOCK = 256

NA_HEADS = 16
NA_HEAD_DIM = 64
NA_KH = 8
NA_KW = 16
GRID_W = 64
NA_ROWS_PER_GROUP = 4
NA_WIN_ROWS = NA_ROWS_PER_GROUP + NA_KH - 1
NA_Q = NA_ROWS_PER_GROUP * GRID_W
NA_K = NA_WIN_ROWS * GRID_W

N_EXPERTS = 16
EC_CAPACITY_FACTOR = 2
D_FF = 2 * D_MODEL

EPS = 1e-6
NEG_INF = -1e30

LANES = 128
BF16_SUBLANES = 16
VMEM_LIMIT = 56 * 1024 * 1024

HIGHEST = lax.Precision.HIGHEST


def _cparams(sem):
    return pltpu.CompilerParams(dimension_semantics=sem, vmem_limit_bytes=VMEM_LIMIT)


def _tile(n, pref):
    t = min(pref, n)
    while n % t:
        t //= 2
    return t


def _silu(x):
    return x * (1.0 / (1.0 + jnp.exp(-x)))


def _ones_where(mask):
    return jnp.where(mask, 1.0, 0.0).astype(BF16)


def _dot(a, b):
    return jnp.dot(a, b, preferred_element_type=F32)


def _dot_nt(a, b):
    return lax.dot_general(a, b, (((1,), (1,)), ((), ())), preferred_element_type=F32)


def _dot_tn(a, b):
    return lax.dot_general(a, b, (((0,), (0,)), ((), ())), preferred_element_type=F32)


def _ada_kernel(c_ref, w_ref, b_ref, o_ref):
    ca = _silu(c_ref[...])
    o_ref[...] = jnp.dot(ca, w_ref[...], preferred_element_type=F32, precision=HIGHEST) + b_ref[...]


def _ada_mod(c_all, ada_w, ada_b):
    R, D = c_all.shape
    n_out = ada_w.shape[-1]
    tn = min(1536, n_out)
    return pl.pallas_call(
        _ada_kernel,
        grid=(DEPTH, n_out // tn),
        in_specs=[
            pl.BlockSpec((R, D), lambda l, j: (0, 0)),
            pl.BlockSpec((None, D, tn), lambda l, j: (l, 0, j)),
            pl.BlockSpec((None, 1, tn), lambda l, j: (l, 0, j)),
        ],
        out_specs=pl.BlockSpec((None, R, tn), lambda l, j: (l, 0, j)),
        out_shape=jax.ShapeDtypeStruct((DEPTH, R, n_out), F32),
        compiler_params=_cparams(("arbitrary", "arbitrary")),
        name="ada_mod",
    )(c_all, ada_w, ada_b.reshape(DEPTH, 1, n_out))


def _norm_mod(x, g, sh, sc):
    ms = jnp.mean(x * x, axis=-1, keepdims=True)
    y = x * lax.rsqrt(ms + EPS) * g
    return y * (1.0 + sc) + sh


def _proj_kernel(x_ref, g_ref, sh_ref, sc_ref, w_ref, o_ref, h_scr):
    @pl.when(pl.program_id(2) == 0)
    def _():
        h_scr[...] = _norm_mod(x_ref[...], g_ref[...], sh_ref[...], sc_ref[...]).astype(BF16)

    o_ref[...] = _dot(h_scr[...], w_ref[...]).astype(o_ref.dtype)


def _proj_qknorm_kernel(x_ref, g_ref, sh_ref, sc_ref, w_ref, bd_ref, gain_ref, o_ref, h_scr, *, qk_blocks, tn):
    j = pl.program_id(2)

    @pl.when(j == 0)
    def _():
        h_scr[...] = _norm_mod(x_ref[...], g_ref[...], sh_ref[...], sc_ref[...]).astype(BF16)

    acc = _dot(h_scr[...], w_ref[...])

    @pl.when(j < qk_blocks)
    def _():
        for c in range(tn // 256):
            a = acc[:, c * 256:(c + 1) * 256]
            ss = _dot((a * a).astype(BF16), bd_ref[...])
            r = lax.rsqrt(ss * (1.0 / NA_HEAD_DIM) + EPS)
            o_ref[:, c * 256:(c + 1) * 256] = (a * r * gain_ref[:, c * 256:(c + 1) * 256]).astype(o_ref.dtype)

    @pl.when(j >= qk_blocks)
    def _():
        o_ref[...] = acc.astype(o_ref.dtype)


def _proj_rotary_kernel(x_ref, g_ref, sh_ref, sc_ref, w_ref, cos_ref, sin_ref, o_ref, h_scr, *, tn):
    j = pl.program_id(2)

    @pl.when(j == 0)
    def _():
        h_scr[...] = _norm_mod(x_ref[...], g_ref[...], sh_ref[...], sc_ref[...]).astype(BF16)

    acc = _dot(h_scr[...], w_ref[...])
    qk_blocks = 2 * RET_QK_WIDTH // tn
    half = RET_QK_DIM // 2

    @pl.when(j < qk_blocks)
    def _():
        scale = jnp.where(j < qk_blocks // 2, 1.0, RET_QK_DIM ** -0.5)
        cos = cos_ref[...] * scale
        sin = sin_ref[...] * scale
        for c in range(tn // RET_QK_DIM):
            a1 = acc[:, c * RET_QK_DIM:c * RET_QK_DIM + half]
            a2 = acc[:, c * RET_QK_DIM + half:(c + 1) * RET_QK_DIM]
            o_ref[:, c * RET_QK_DIM:c * RET_QK_DIM + half] = (a1 * cos - a2 * sin).astype(o_ref.dtype)
            o_ref[:, c * RET_QK_DIM + half:(c + 1) * RET_QK_DIM] = (a1 * sin + a2 * cos).astype(o_ref.dtype)

    @pl.when(j >= qk_blocks)
    def _():
        o_ref[...] = acc.astype(o_ref.dtype)


def _proj(x, g, sh, sc, w, qk=None, rotary=None):
    B, T, D = x.shape
    n_out = w.shape[1]
    tm = _tile(T, 2048)
    tn = _tile(n_out, 1024)
    grid = (B, T // tm, n_out // tn)
    in_specs = [
        pl.BlockSpec((None, tm, D), lambda b, i, j: (b, i, 0)),
        pl.BlockSpec((1, D), lambda b, i, j: (0, 0)),
        pl.BlockSpec((None, 1, D), lambda b, i, j: (b, 0, 0)),
        pl.BlockSpec((None, 1, D), lambda b, i, j: (b, 0, 0)),
        pl.BlockSpec((D, tn), lambda b, i, j: (0, j)),
    ]
    args = [x, g, sh, sc, w]
    if rotary is not None:
        half = RET_QK_DIM // 2
        in_specs += [pl.BlockSpec((tm, half), lambda b, i, j: (i, 0))] * 2
        args += list(rotary)
        body = functools.partial(_proj_rotary_kernel, tn=tn)
        name = "proj_rotary"
    elif qk is None:
        body = _proj_kernel
        name = "proj"
    else:
        bd, gain, n_qk = qk
        in_specs += [
            pl.BlockSpec((256, 256), lambda b, i, j: (0, 0)),
            pl.BlockSpec((1, tn), lambda b, i, j: (0, j)),
        ]
        args += [bd, gain]
        body = functools.partial(_proj_qknorm_kernel, qk_blocks=n_qk // tn, tn=tn)
        name = "proj_qknorm"
    return pl.pallas_call(
        body,
        grid=grid,
        in_specs=in_specs,
        out_specs=pl.BlockSpec((None, tm, tn), lambda b, i, j: (b, i, j)),
        out_shape=jax.ShapeDtypeStruct((B, T, n_out), BF16),
        scratch_shapes=[pltpu.VMEM((tm, D), BF16)],
        compiler_params=_cparams(("parallel", "parallel", "arbitrary")),
        name=name,
    )(*args)


def _outproj_kernel(y_ref, w_ref, x_ref, g_ref, o_ref):
    o_ref[...] = x_ref[...] + g_ref[...] * _dot(y_ref[...], w_ref[...])


def _ffn_pre_kernel(x_ref, g_ref, sh_ref, sc_ref, r_ref, h_ref, aff_ref):
    h = _norm_mod(x_ref[...], g_ref[...], sh_ref[...], sc_ref[...])
    h_hi = h.astype(BF16)
    h_ref[...] = h_hi
    h_lo = (h - h_hi.astype(F32)).astype(BF16)
    r = r_ref[...]
    r_hi = r.astype(BF16)
    r_lo = (r - r_hi.astype(F32)).astype(BF16)
    logits = _dot(h_hi, r_hi) + (_dot(h_hi, r_lo) + _dot(h_lo, r_hi))
    m = jnp.max(logits, axis=-1, keepdims=True)
    e = jnp.exp(logits - m)
    aff_ref[...] = e / jnp.sum(e, axis=-1, keepdims=True)


def _out_proj(y, w, x, gate, ng, sh, sc, router):
    B, T, K = y.shape
    D = w.shape[1]
    E = router.shape[1]
    tm = _tile(T, 1024)
    xn = pl.pallas_call(
        _outproj_kernel,
        grid=(B, T // tm),
        in_specs=[
            pl.BlockSpec((None, tm, K), lambda b, i: (b, i, 0)),
            pl.BlockSpec((K, D), lambda b, i: (0, 0)),
            pl.BlockSpec((None, tm, D), lambda b, i: (b, i, 0)),
            pl.BlockSpec((None, 1, D), lambda b, i: (b, 0, 0)),
        ],
        out_specs=pl.BlockSpec((None, tm, D), lambda b, i: (b, i, 0)),
        out_shape=jax.ShapeDtypeStruct((B, T, D), F32),
        compiler_params=_cparams(("parallel", "parallel")),
        name="out_proj",
    )(y, w, x, gate)
    tp = _tile(T, 1024)
    nt = T // tp
    h, aff = pl.pallas_call(
        _ffn_pre_kernel,
        grid=(B, nt),
        in_specs=[
            pl.BlockSpec((None, tp, D), lambda b, i: (b, i, 0)),
            pl.BlockSpec((1, D), lambda b, i: (0, 0)),
            pl.BlockSpec((None, 1, D), lambda b, i: (b, 0, 0)),
            pl.BlockSpec((None, 1, D), lambda b, i: (b, 0, 0)),
            pl.BlockSpec((D, E), lambda b, i: (0, 0)),
        ],
        out_specs=[
            pl.BlockSpec((tp, D), lambda b, i: (b * nt + i, 0)),
            pl.BlockSpec((tp, E), lambda b, i: (b * nt + i, 0)),
        ],
        out_shape=[jax.ShapeDtypeStruct((B * T, D), BF16), jax.ShapeDtypeStruct((B * T, E), F32)],
        compiler_params=_cparams(("parallel", "parallel")),
        name="ffn_pre",
    )(xn, ng, sh, sc, router)
    return xn, h, aff


def _retention_kernel(*refs, has_prev):
    if has_prev:
        q_ref, k_ref, v_ref, gate_ref, intra_ref, qdec_ref, kdec_ref, cdec_ref, prev_ref, o_ref, state = refs
    else:
        q_ref, k_ref, v_ref, gate_ref, intra_ref, qdec_ref, kdec_ref, cdec_ref, o_ref, state = refs
        prev_ref = None

    @pl.when(pl.program_id(1) == 0)
    def _():
        state[...] = jnp.zeros_like(state)

    for h in range(RET_HEADS):
        qk = slice(h * RET_QK_DIM, (h + 1) * RET_QK_DIM)
        vs = slice(h * RET_V_DIM, (h + 1) * RET_V_DIM)
        qb = q_ref[:, qk]
        kb = k_ref[:, qk]
        v = v_ref[:, vs]
        scores = _dot_nt(qb, kb) * intra_ref[h]
        st = state[h]
        o = _dot(scores.astype(BF16), v) + _dot(qb, st.astype(BF16)) * qdec_ref[h]
        state[h] = st * cdec_ref[h] + _dot_tn((kb.astype(F32) * kdec_ref[h]).astype(BF16), v)

        mu = jnp.mean(o, axis=-1, keepdims=True)
        oc = o - mu
        var = jnp.mean(oc * oc, axis=-1, keepdims=True)
        y = oc * lax.rsqrt(var + EPS) * _silu(gate_ref[:, vs].astype(F32))
        if has_prev:
            y = y + prev_ref[:, vs].astype(F32)
        o_ref[:, vs] = y.astype(o_ref.dtype)


def _retention_dir(proj, tabs, direction, prev):
    B, T, _ = proj.shape
    rc = min(RET_BLOCK, T)
    nc = T // rc
    intra, qdec, kdec, cdec = tabs
    if direction == 0:
        cidx = lambda c: c
    else:
        cidx = lambda c: nc - 1 - c
    H = RET_HEADS
    v_blk = 2 * RET_QK_WIDTH // RET_V_WIDTH
    in_specs = [
        pl.BlockSpec((None, rc, RET_QK_WIDTH), lambda b, c: (b, cidx(c), 0)),
        pl.BlockSpec((None, rc, RET_QK_WIDTH), lambda b, c: (b, cidx(c), 1)),
        pl.BlockSpec((None, rc, RET_V_WIDTH), lambda b, c: (b, cidx(c), v_blk)),
        pl.BlockSpec((None, rc, RET_V_WIDTH), lambda b, c: (b, cidx(c), v_blk + 1 + direction)),
        pl.BlockSpec((H, rc, rc), lambda b, c: (direction, 0, 0)),
        pl.BlockSpec((H, rc, 1), lambda b, c: (direction, 0, 0)),
        pl.BlockSpec((H, rc, 1), lambda b, c: (direction, 0, 0)),
        pl.BlockSpec((H, 1, 1), lambda b, c: (direction, 0, 0)),
    ]
    args = [proj, proj, proj, proj, intra, qdec, kdec, cdec]
    if prev is not None:
        in_specs.append(pl.BlockSpec((None, rc, RET_V_WIDTH), lambda b, c: (b, cidx(c), 0)))
        args.append(prev)
    return pl.pallas_call(
        functools.partial(_retention_kernel, has_prev=prev is not None),
        grid=(B, nc),
        in_specs=in_specs,
        out_specs=pl.BlockSpec((None, rc, RET_V_WIDTH), lambda b, c: (b, cidx(c), 0)),
        out_shape=jax.ShapeDtypeStruct((B, T, RET_V_WIDTH), BF16),
        scratch_shapes=[pltpu.VMEM((H, RET_QK_DIM, RET_V_DIM), F32)],
        compiler_params=_cparams(("parallel", "arbitrary")),
        name="retention_bwd" if direction else "retention_fwd",
    )(*args)


def _retention_tables(decay_logit, rc):
    lg = jax.nn.log_sigmoid(decay_logit.astype(F32).reshape(-1))
    pos = jnp.arange(rc, dtype=F32)
    rel = pos[:, None] - pos[None, :]
    l3 = lg[:, None, None]
    fwd = jnp.where(rel >= 0, jnp.exp(jnp.maximum(rel, 0.0) * l3), 0.0)
    bwd = jnp.where(rel <= 0, jnp.exp(jnp.maximum(-rel, 0.0) * l3), 0.0)
    is_bwd = (jnp.arange(2 * RET_HEADS) >= RET_HEADS)
    intra = jnp.where(is_bwd[:, None, None], bwd, fwd)
    q_pow = jnp.where(is_bwd[:, None], rc - pos[None, :], pos[None, :] + 1.0)
    k_pow = jnp.where(is_bwd[:, None], pos[None, :], rc - 1.0 - pos[None, :])
    qdec = jnp.exp(q_pow * lg[:, None])[..., None]
    kdec = jnp.exp(k_pow * lg[:, None])[..., None]
    cdec = jnp.exp(rc * lg)[:, None, None]
    return intra, qdec, kdec, cdec


def _rotary_tables(T):
    d = RET_QK_DIM
    inv = 1.0 / (ROPE_BASE ** (jnp.arange(0, d, 2, dtype=F32) / d))
    ang = jnp.arange(T, dtype=F32)[:, None] * inv[None, :]
    return jnp.cos(ang), jnp.sin(ang)


def _na_kernel(q_ref, k_ref, v_ref, bias_ref, o_ref, *, rows):
    n_groups = rows // NA_ROWS_PER_GROUP
    lane = lax.broadcasted_iota(I32, (NA_Q, LANES), 1)
    first = lane < NA_HEAD_DIM

    def group(gi, carry):
        ustart = jnp.clip(gi * NA_ROWS_PER_GROUP - NA_KH // 2, 0, rows - NA_WIN_ROWS)
        pat = jnp.where(gi == 0, 0, jnp.where(gi == n_groups - 1, 2, 1))
        kstart = pl.multiple_of(ustart * GRID_W, GRID_W)
        qstart = pl.multiple_of(gi * NA_Q, NA_Q)
        kw = k_ref[pl.ds(kstart, NA_K), :]
        vw = v_ref[pl.ds(kstart, NA_K), :]
        q = q_ref[pl.ds(qstart, NA_Q), :].astype(F32)
        q2 = jnp.concatenate([jnp.where(first, q, 0.0), jnp.where(first, 0.0, q)], axis=0).astype(BF16)
        s = _dot_nt(q2, kw) + bias_ref[pat].reshape(2 * NA_Q, NA_K)
        m = jnp.max(s, axis=-1, keepdims=True)
        e = jnp.exp(s - m)
        l = jnp.sum(e, axis=-1, keepdims=True)
        o2 = _dot(e.astype(BF16), vw) / l
        o_ref[pl.ds(qstart, NA_Q), :] = jnp.where(first, o2[:NA_Q], o2[NA_Q:]).astype(o_ref.dtype)
        return carry

    lax.fori_loop(0, n_groups, group, 0, unroll=8)


def _na_attention(qkv, bias):
    B, T, _ = qkv.shape
    rows = T // GRID_W
    assert rows % NA_ROWS_PER_GROUP == 0 and rows >= NA_WIN_ROWS + 1
    n_pairs = D_MODEL // LANES
    blk = lambda off: pl.BlockSpec((None, T, LANES), lambda p, b: (b, 0, off + p))
    return pl.pallas_call(
        functools.partial(_na_kernel, rows=rows),
        grid=(n_pairs, B),
        in_specs=[
            blk(0), blk(n_pairs), blk(2 * n_pairs),
            pl.BlockSpec((3, 2, NA_Q, NA_K), lambda p, b: (0, p, 0, 0)),
        ],
        out_specs=pl.BlockSpec((None, T, LANES), lambda p, b: (b, 0, p)),
        out_shape=jax.ShapeDtypeStruct((B, T, D_MODEL), BF16),
        compiler_params=_cparams(("parallel", "parallel")),
        name="na_attention",
    )(qkv, qkv, qkv, bias)


def _na_bias_table(rpb):
    H = rpb.shape[0]
    r = rpb.astype(F32)
    pad = GRID_W
    rp = jnp.concatenate([jnp.repeat(r[..., :1], pad, -1), r, jnp.repeat(r[..., -1:], pad, -1)], -1)
    base = NA_KW - 1 + pad
    t1 = jnp.stack([rp[..., base - qc: base - qc + GRID_W] for qc in range(GRID_W)], axis=2)
    qc = np.arange(GRID_W)[:, None]
    kc = np.arange(GRID_W)[None, :]
    ws = np.clip(qc - NA_KW // 2, 0, GRID_W - NA_KW)
    t1 = jnp.where(jnp.asarray((kc >= ws) & (kc < ws + NA_KW)), t1, NEG_INF)
    n_dr = 2 * NA_KH - 1
    t1 = jnp.concatenate([t1, jnp.full((H, 1, GRID_W, GRID_W), NEG_INF, F32)], axis=1)
    pick = np.full((3, NA_ROWS_PER_GROUP, NA_WIN_ROWS), n_dr, np.int32)
    for p in range(3):
        for i in range(NA_ROWS_PER_GROUP):
            lo = (0, i, NA_ROWS_PER_GROUP - 1)[p]
            qrow = (i, i + NA_KH // 2, i + NA_KH - 1)[p]
            for u in range(lo, lo + NA_KH):
                pick[p, i, u] = u - qrow + NA_KH - 1
    table = jnp.take(t1, jnp.asarray(pick.reshape(-1)), axis=1)
    table = table.reshape(H, 3, NA_ROWS_PER_GROUP, NA_WIN_ROWS, GRID_W, GRID_W)
    return table.transpose(1, 0, 2, 4, 3, 5).reshape(3, H, NA_Q, NA_K)


def _select_kernel(aff_ref, pos_ref, pre_ref, *, cap, n_tok, rb):
    E = N_EXPERTS
    per_row = LANES // E
    rows = n_tok // per_row
    bits = pltpu.bitcast(aff_ref[...], I32)
    lane = lax.broadcasted_iota(I32, (rows, LANES), 1)
    tok = lax.broadcasted_iota(I32, (rows, LANES), 0) * per_row + lane // E

    def count(pred):
        c = jnp.sum(pred.astype(I32), axis=0, keepdims=True)
        c = jnp.broadcast_to(c, (8, LANES))
        sh = E
        while sh < LANES:
            c = c + pltpu.roll(c, sh, 1)
            sh *= 2
        return c[0:1]

    def thr_step(i, thr):
        cand = thr | jnp.left_shift(jnp.int32(1), 30 - i)
        return jnp.where(count(bits >= cand) >= cap, cand, thr)

    thr = lax.fori_loop(0, 31, thr_step, jnp.zeros((1, LANES), I32))
    gt = bits > thr
    eq = bits == thr
    need = cap - count(gt)

    nbits = max(1, int(math.ceil(math.log2(n_tok))))

    def tie_step(i, jmax):
        cand = jmax + jnp.left_shift(jnp.int32(1), nbits - 1 - i)
        return jnp.where(count(jnp.logical_and(eq, tok < cand)) < need, cand, jmax)

    jmax = lax.fori_loop(0, nbits, tie_step, jnp.zeros((1, LANES), I32))

    li = lax.broadcasted_iota(I32, (LANES, LANES), 0)
    lj = lax.broadcasted_iota(I32, (LANES, LANES), 1)
    same_e = (li % E) == (lj % E)
    a_before = _ones_where(jnp.logical_and(same_e, li // E < lj // E))
    a_all = _ones_where(same_e)
    ri = lax.broadcasted_iota(I32, (rb, rb), 0)
    rj = lax.broadcasted_iota(I32, (rb, rb), 1)
    lower = _ones_where(rj < ri)

    running = jnp.zeros((1, LANES), F32)
    for blk in range(rows // rb):
        sl = slice(blk * rb, (blk + 1) * rb)
        b_blk = pltpu.bitcast(aff_ref[sl, :], I32)
        t_blk = ((lax.broadcasted_iota(I32, (rb, LANES), 0) + blk * rb) * per_row
                 + lax.broadcasted_iota(I32, (rb, LANES), 1) // E)
        keep = jnp.logical_or(b_blk > thr, jnp.logical_and(b_blk == thr, t_blk <= jmax))
        kb = _ones_where(keep)
        row_tot = _dot(kb, a_all)
        within = _dot(lower, row_tot.astype(BF16))
        pre = running + within + _dot(kb, a_before)
        running = running + jnp.sum(row_tot, axis=0, keepdims=True)
        pre_i = pre.astype(I32)
        pre_ref[sl, :] = pre_i
        pos_ref[sl, :] = jnp.where(keep, pre_i, -1)


def _select(aff, cap):
    n_tok, E = aff.shape
    rows = n_tok * E // LANES
    rb = min(256, rows)
    aff8 = aff.reshape(rows, LANES)
    pos, pre = pl.pallas_call(
        functools.partial(_select_kernel, cap=cap, n_tok=n_tok, rb=rb),
        out_shape=[jax.ShapeDtypeStruct((rows, LANES), I32)] * 2,
        compiler_params=pltpu.CompilerParams(vmem_limit_bytes=VMEM_LIMIT),
        name="expert_select",
    )(aff8)
    return pos.reshape(n_tok, E), pre.reshape(n_tok, E)


DISPATCH_GROUP = 4


def _dispatch_slab(tn):
    return tn // 4 + BF16_SUBLANES


def _dispatch_rounds(tn):
    return -(-(tn + BF16_SUBLANES) // _dispatch_slab(tn))


def _dispatch_kernel(ws_ref, need_ref, pos_ref, aff_ref, h_ref, xe_ref, gs_ref, *, n_tiles, sub, batch):
    eg = pl.program_id(0)
    tb = pl.program_id(1)
    group, _, tn = pos_ref.shape
    head = BF16_SUBLANES
    slab = _dispatch_slab(tn)
    n_rounds = _dispatch_rounds(tn)

    @pl.when(tb == 0)
    def _():
        xe_ref[...] = jnp.zeros_like(xe_ref)
        gs_ref[...] = jnp.zeros_like(gs_ref)

    def one_round(s, r):
        t = tb * sub + s
        hs = h_ref[pl.ds(pl.multiple_of(s * tn, tn), tn), :]
        row = lax.broadcasted_iota(I32, (slab, tn), 0) + r * slab
        ws = [pl.multiple_of(ws_ref[(eg * group + k) * n_tiles + t], head) for k in range(group)]
        hits = [row == pos_ref[k, pl.ds(s, 1), :] - ws[k] for k in range(group)]
        rows = _dot(jnp.concatenate([_ones_where(hk) for hk in hits], axis=0), hs)
        for k in range(group):
            rk = rows[k * slab:(k + 1) * slab]
            gk = jnp.sum(jnp.where(hits[k], aff_ref[k, pl.ds(s, 1), :], 0.0), axis=1, keepdims=True)
            base = pl.multiple_of(ws[k] + r * slab, head)
            acc_rows = head if r == 0 else slab
            xe_ref[k, pl.ds(base, acc_rows), :] = (
                xe_ref[k, pl.ds(base, acc_rows), :].astype(F32) + rk[:acc_rows]).astype(xe_ref.dtype)
            gs_ref[k, pl.ds(base, acc_rows), :] = gs_ref[k, pl.ds(base, acc_rows), :] + gk[:acc_rows]
            if r == 0:
                rest = pl.multiple_of(base + head, head)
                xe_ref[k, pl.ds(rest, slab - head), :] = rk[head:].astype(xe_ref.dtype)
                gs_ref[k, pl.ds(rest, slab - head), :] = gk[head:]

    def tile_batch(sb, carry):
        for u in range(batch):
            one_round(sb * batch + u, 0)
        for u in range(batch):
            s = sb * batch + u
            need = need_ref[eg * n_tiles + tb * sub + s]
            for r in range(1, n_rounds):
                pl.when(need > r * slab)(functools.partial(one_round, s, r))
        return carry

    lax.fori_loop(0, sub // batch, tile_batch, 0)


def _dispatch(wstart, need, pos_t, aff_t, h, cap, tn):
    E, n_tiles, _ = pos_t.shape
    N, D = h.shape
    rows = cap + _dispatch_rounds(tn) * _dispatch_slab(tn)
    sub = 8 if n_tiles % 8 == 0 else n_tiles
    group = DISPATCH_GROUP
    resident = dict(pipeline_mode=pl.Buffered(1))
    grid_spec = pltpu.PrefetchScalarGridSpec(
        num_scalar_prefetch=2,
        grid=(E // group, n_tiles // sub),
        in_specs=[
            pl.BlockSpec((group, sub, tn), lambda e, t, ws, nd: (e, t, 0)),
            pl.BlockSpec((group, sub, tn), lambda e, t, ws, nd: (e, t, 0)),
            pl.BlockSpec((sub * tn, D), lambda e, t, ws, nd: (t, 0)),
        ],
        out_specs=[
            pl.BlockSpec((group, rows, D), lambda e, t, ws, nd: (e, 0, 0), **resident),
            pl.BlockSpec((group, rows, 1), lambda e, t, ws, nd: (e, 0, 0), **resident),
        ],
    )
    return pl.pallas_call(
        functools.partial(_dispatch_kernel, n_tiles=n_tiles, sub=sub, batch=8 if sub % 8 == 0 else 1),
        grid_spec=grid_spec,
        out_shape=[jax.ShapeDtypeStruct((E, rows, D), BF16), jax.ShapeDtypeStruct((E, rows, 1), F32)],
        compiler_params=_cparams(("parallel", "arbitrary")),
        name="moe_dispatch",
    )(wstart, need, pos_t, aff_t, h)


def _ffn_kernel(x_ref, gs_ref, wg_ref, wu_ref, wd_ref, o_ref, acc, *, n_f):
    f = pl.program_id(2)

    @pl.when(f == 0)
    def _():
        acc[...] = jnp.zeros_like(acc)

    x = x_ref[...]
    hid = _silu(_dot(x, wg_ref[...].astype(BF16))) * _dot(x, wu_ref[...].astype(BF16))
    acc[...] += _dot(hid.astype(BF16), wd_ref[...].astype(BF16))

    @pl.when(f == n_f - 1)
    def _():
        o_ref[...] = (acc[...] * gs_ref[...]).astype(o_ref.dtype)


def _expert_ffn(xe, gs, wg, wu, wd, layer, C):
    E, _, D = xe.shape
    F = wg.shape[-1]
    tm = _tile(C, 2048)
    tf = _tile(F, 512)
    n_f = F // tf
    return pl.pallas_call(
        functools.partial(_ffn_kernel, n_f=n_f),
        grid=(E, C // tm, n_f),
        in_specs=[
            pl.BlockSpec((None, tm, D), lambda e, m, f: (e, m, 0)),
            pl.BlockSpec((None, tm, 1), lambda e, m, f: (e, m, 0)),
            pl.BlockSpec((None, None, D, tf), lambda e, m, f: (layer, e, 0, f)),
            pl.BlockSpec((None, None, D, tf), lambda e, m, f: (layer, e, 0, f)),
            pl.BlockSpec((None, None, tf, D), lambda e, m, f: (layer, e, f, 0)),
        ],
        out_specs=pl.BlockSpec((None, tm, D), lambda e, m, f: (e, m, 0)),
        out_shape=jax.ShapeDtypeStruct((E, C, D), BF16),
        scratch_shapes=[pltpu.VMEM((tm, D), F32)],
        compiler_params=_cparams(("parallel", "parallel", "arbitrary")),
        name="expert_ffn",
    )(xe, gs, wg, wu, wd)


COMBINE_SLAB = LANES
COMBINE_FETCH = 96
COMBINE_PARTS = 4


def _combine_kernel(ws_ref, end_ref, pos_ref, ye_any, x_ref, g_ref, o_ref, *scratch,
                    n_tiles, n_exp, win, slab, fetch):
    hit_scrs = scratch[:-3]
    y_buf, late_scr, sem = scratch[-3:]
    per = n_exp // len(hit_scrs)
    t = pl.program_id(0)
    tn = pos_ref.shape[0]
    slot = t % 2

    def window_copy(tile, buf, e):
        ws = pl.multiple_of(ws_ref[e * n_tiles + tile], BF16_SUBLANES)
        return pltpu.make_async_copy(ye_any.at[e, pl.ds(ws, fetch), :],
                                     y_buf.at[buf, pl.ds(e * slab, fetch), :], sem.at[buf])

    @pl.when(t == 0)
    def _():
        y_buf[...] = jnp.zeros_like(y_buf)
        for e in range(n_exp):
            window_copy(0, 0, e).start()

    @pl.when(t + 1 < n_tiles)
    def _():
        for e in range(n_exp):
            window_copy(t + 1, 1 - slot, e).start()

    lane = lax.broadcasted_iota(I32, (tn, slab), 1)
    for gi, hit_scr in enumerate(hit_scrs):
        for k in range(per):
            e = gi * per + k
            prel = pos_ref[:, e:e + 1] - ws_ref[e * n_tiles + t]
            hit_scr[:, k * slab:(k + 1) * slab] = _ones_where(lane == prel)

    for e in range(n_exp):
        window_copy(t, slot, e).wait()

    acc = None
    for gi, hit_scr in enumerate(hit_scrs):
        part = _dot(hit_scr[...], y_buf[slot, gi * per * slab:(gi + 1) * per * slab, :])
        acc = part if acc is None else acc + part
    o_ref[...] = x_ref[...] + g_ref[...] * acc

    if win > fetch:
        for e in range(n_exp):
            ws = ws_ref[e * n_tiles + t]

            @pl.when(end_ref[e * n_tiles + t] - ws > fetch)
            def _():
                late = pl.multiple_of(ws + fetch, BF16_SUBLANES)
                pltpu.sync_copy(ye_any.at[e, pl.ds(late, win - fetch), :], late_scr)
                prel = pos_ref[:, e:e + 1] - ws - fetch
                hit = _ones_where(lax.broadcasted_iota(I32, (tn, win - fetch), 1) == prel)
                o_ref[...] += g_ref[...] * _dot(hit, late_scr[...])


def _combine(wstart, wend, pos, ye, x, gate, tn):
    E, C, D = ye.shape
    B, T, _ = x.shape
    N = B * T
    n_tiles = N // tn
    tpb = T // tn
    win = tn + BF16_SUBLANES
    slab = min(COMBINE_SLAB, win)
    fetch = min(COMBINE_FETCH, slab)
    grid_spec = pltpu.PrefetchScalarGridSpec(
        num_scalar_prefetch=2,
        grid=(n_tiles,),
        in_specs=[pl.BlockSpec((tn, E), lambda t, ws, we: (t, 0)),
                  pl.BlockSpec(memory_space=pl.ANY),
                  pl.BlockSpec((None, tn, D), lambda t, ws, we: (t // tpb, t % tpb, 0)),
                  pl.BlockSpec((None, 1, D), lambda t, ws, we: (t // tpb, 0, 0))],
        out_specs=pl.BlockSpec((None, tn, D), lambda t, ws, we: (t // tpb, t % tpb, 0)),
        scratch_shapes=[pltpu.VMEM((tn, E // COMBINE_PARTS * slab), BF16)] * COMBINE_PARTS
        + [pltpu.VMEM((2, E * slab, D), BF16),
           pltpu.VMEM((win - fetch, D), BF16),
           pltpu.SemaphoreType.DMA((2,))],
    )
    return pl.pallas_call(
        functools.partial(_combine_kernel, n_tiles=n_tiles, n_exp=E, win=win, slab=slab, fetch=fetch),
        grid_spec=grid_spec,
        out_shape=jax.ShapeDtypeStruct((B, T, D), F32),
        compiler_params=_cparams(("arbitrary",)),
        name="moe_combine",
    )(wstart, wend, pos, ye, x, gate)


def _moe_tile(n_tok, cap):
    tn = 256
    while tn + BF16_SUBLANES > cap or n_tok % tn:
        tn //= 2
    return tn


def _moe_layer(x, h, aff, gate, wg, wu, wd, layer):
    B, T, D = x.shape
    N = B * T
    E = aff.shape[1]
    cap = EC_CAPACITY_FACTOR * N // E
    tn = _moe_tile(T, cap)
    win = tn + BF16_SUBLANES
    pos, pre = _select(aff, cap)
    start = pre[::tn].T
    aligned = (start // BF16_SUBLANES * BF16_SUBLANES).astype(I32)
    wdisp = aligned.reshape(-1)
    wstart = jnp.minimum(aligned, cap - win).reshape(-1)
    end = jnp.concatenate([start[:, 1:], jnp.full((E, 1), cap, I32)], axis=1).astype(I32)
    wend = end.reshape(-1)
    need = (end - aligned).reshape(E // DISPATCH_GROUP, DISPATCH_GROUP, -1).max(axis=1).reshape(-1)
    pos_t = pos.T.reshape(E, N // tn, tn)
    aff_t = aff.T.reshape(E, N // tn, tn)
    xe, gs = _dispatch(wdisp, need, pos_t, aff_t, h, cap, tn)
    ye = _expert_ffn(xe, gs, wg, wu, wd, layer, cap)
    return _combine(wstart, wend, pos, ye, x, gate, tn)


def _trunk(x, mod, p):
    B, T, D = x.shape
    rc = min(RET_BLOCK, T)
    cos, sin = _rotary_tables(T)
    for i in range(DEPTH):
        sh1, sc1, g1, sh2, sc2, g2 = [mod[i, :, k][:, None, :] for k in range(6)]
        j = i // 2
        if i % 2 == 0:
            proj = _proj(x, p["norm_mix_g"][i][None], sh1, sc1, p["ret_w_in"][j], rotary=(cos, sin))
            tabs = _retention_tables(p["ret_decay_logit"][j], rc)
            y_f = _retention_dir(proj, tabs, 0, None)
            y = _retention_dir(proj, tabs, 1, y_f)
            w_out = p["ret_w_out"][j]
        else:
            qkv = _proj(x, p["norm_mix_g"][i][None], sh1, sc1, p["na_w_in"][j],
                        qk=(p["na_bd"], p["na_gain"][j], 2 * D))
            y = _na_attention(qkv, p["na_bias"][j])
            w_out = p["na_w_out"][j]
        x, h, aff = _out_proj(y, w_out, x, g1, p["norm_ffn_g"][i][None], sh2, sc2, p["moe_router"][i])
        x = _moe_layer(x, h, aff, g2, p["moe_w_gate"], p["moe_w_up"], p["moe_w_down"], i)
    return x


def kernel(x_prompt, x_sample, c_prompt, c_sample, norm_mix_g, norm_ffn_g, ada_w, ada_b, ret_w_in, ret_decay_logit, ret_w_out, na_w_in, na_q_gain, na_k_gain, na_rpb, na_w_out, moe_router, moe_w_gate, moe_w_up, moe_w_down):
    D = D_MODEL
    bp, bs = c_prompt.shape[0], c_sample.shape[0]
    c_all = jnp.concatenate([c_prompt, c_sample], axis=0)
    pad = (-c_all.shape[0]) % 8
    if pad:
        c_all = jnp.pad(c_all, ((0, pad), (0, 0)))
    mod = _ada_mod(c_all, ada_w, ada_b).reshape(DEPTH, c_all.shape[0], 6, D)

    heads_row = lambda v: jnp.tile(v.astype(F32), (1, NA_HEADS))
    na_gain = jnp.concatenate(
        [heads_row(na_q_gain) * (NA_HEAD_DIM ** -0.5), heads_row(na_k_gain),
         jnp.ones((na_q_gain.shape[0], D), F32)], axis=-1)[:, None, :]
    bd = np.kron(np.eye(256 // NA_HEAD_DIM), np.ones((NA_HEAD_DIM, NA_HEAD_DIM)))
    p = dict(
        norm_mix_g=norm_mix_g, norm_ffn_g=norm_ffn_g,
        ret_w_in=ret_w_in.astype(BF16), ret_decay_logit=ret_decay_logit, ret_w_out=ret_w_out.astype(BF16),
        na_w_in=na_w_in.astype(BF16), na_w_out=na_w_out.astype(BF16),
        na_gain=na_gain, na_bd=jnp.asarray(bd, BF16),
        na_bias=jnp.stack([_na_bias_table(na_rpb[l]) for l in range(na_rpb.shape[0])]),
        moe_router=moe_router,
        moe_w_gate=moe_w_gate, moe_w_up=moe_w_up, moe_w_down=moe_w_down,
    )
    y_prompt = _trunk(x_prompt, mod[:, :bp], p)
    y_sample = _trunk(x_sample, mod[:, bp:bp + bs], p)
    return (y_prompt, y_sample)
```

```python
import functools
import math

import numpy as np
import jax
import jax.numpy as jnp
from jax import lax
from jax.experimental import pallas as pl
from jax.experimental.pallas import tpu as pltpu

F32 = jnp.float32
BF16 = jnp.bfloat16
I32 = jnp.int32

D_MODEL = 1024
DEPTH = 4

RET_HEADS = 4
RET_QK_DIM = 256
RET_V_DIM = 512
RET_QK_WIDTH = RET_HEADS * RET_QK_DIM
RET_V_WIDTH = RET_HEADS * RET_V_DIM
RET_IN_COLS = 2 * RET_QK_WIDTH + 3 * RET_V_WIDTH
ROPE_BASE = 10000.0
RET_BLOCK = 256

NA_HEADS = 16
NA_HEAD_DIM = 64
NA_KH = 8
NA_KW = 16
GRID_W = 64
NA_ROWS_PER_GROUP = 4
NA_WIN_ROWS = NA_ROWS_PER_GROUP + NA_KH - 1
NA_Q = NA_ROWS_PER_GROUP * GRID_W
NA_K = NA_WIN_ROWS * GRID_W

N_EXPERTS = 16
EC_CAPACITY_FACTOR = 2
D_FF = 2 * D_MODEL

EPS = 1e-6
NEG_INF = -1e30

LANES = 128
BF16_SUBLANES = 16
VMEM_LIMIT = 56 * 1024 * 1024

HIGHEST = lax.Precision.HIGHEST


def _cparams(sem):
    return pltpu.CompilerParams(dimension_semantics=sem, vmem_limit_bytes=VMEM_LIMIT)


def _tile(n, pref):
    t = min(pref, n)
    while n % t:
        t //= 2
    return t


def _silu(x):
    return (0.5 * x) * (1.0 + jnp.tanh(0.5 * x))


def _ones_where(mask):
    return jnp.where(mask, 1.0, 0.0).astype(BF16)


def _dot(a, b):
    return jnp.dot(a, b, preferred_element_type=F32)


def _dot_nt(a, b):
    return lax.dot_general(a, b, (((1,), (1,)), ((), ())), preferred_element_type=F32)


def _dot_tn(a, b):
    return lax.dot_general(a, b, (((0,), (0,)), ((), ())), preferred_element_type=F32)


def _ada_kernel(c_ref, w_ref, b_ref, o_ref):
    ca = _silu(c_ref[...])
    o_ref[...] = jnp.dot(ca, w_ref[...], preferred_element_type=F32, precision=HIGHEST) + b_ref[...]


def _ada_mod(c_all, ada_w, ada_b):
    R, D = c_all.shape
    n_out = ada_w.shape[-1]
    tn = min(1536, n_out)
    return pl.pallas_call(
        _ada_kernel,
        grid=(DEPTH, n_out // tn),
        in_specs=[
            pl.BlockSpec((R, D), lambda l, j: (0, 0)),
            pl.BlockSpec((None, D, tn), lambda l, j: (l, 0, j)),
            pl.BlockSpec((None, 1, tn), lambda l, j: (l, 0, j)),
        ],
        out_specs=pl.BlockSpec((None, R, tn), lambda l, j: (l, 0, j)),
        out_shape=jax.ShapeDtypeStruct((DEPTH, R, n_out), F32),
        compiler_params=_cparams(("arbitrary", "arbitrary")),
        name="ada_mod",
    )(c_all, ada_w, ada_b.reshape(DEPTH, 1, n_out))


def _norm_mod(x, g, sh, sc):
    ms = jnp.mean(x * x, axis=-1, keepdims=True)
    y = x * lax.rsqrt(ms + EPS) * g
    return y * (1.0 + sc) + sh


def _proj_kernel(x_ref, g_ref, sh_ref, sc_ref, w_ref, o_ref, h_scr):
    @pl.when(pl.program_id(2) == 0)
    def _():
        h_scr[...] = _norm_mod(x_ref[...], g_ref[...], sh_ref[...], sc_ref[...]).astype(BF16)

    o_ref[...] = _dot(h_scr[...], w_ref[...]).astype(o_ref.dtype)


def _proj_qknorm_kernel(x_ref, g_ref, sh_ref, sc_ref, w_ref, bd_ref, gain_ref, o_ref, h_scr, *, qk_blocks, tn):
    j = pl.program_id(2)

    @pl.when(j == 0)
    def _():
        h_scr[...] = _norm_mod(x_ref[...], g_ref[...], sh_ref[...], sc_ref[...]).astype(BF16)

    acc = _dot(h_scr[...], w_ref[...])

    @pl.when(j < qk_blocks)
    def _():
        for c in range(tn // 256):
            a = acc[:, c * 256:(c + 1) * 256]
            ss = _dot((a * a).astype(BF16), bd_ref[...])
            r = lax.rsqrt(ss * (1.0 / NA_HEAD_DIM) + EPS)
            o_ref[:, c * 256:(c + 1) * 256] = (a * r * gain_ref[:, c * 256:(c + 1) * 256]).astype(o_ref.dtype)

    @pl.when(j >= qk_blocks)
    def _():
        o_ref[...] = acc.astype(o_ref.dtype)


def _proj_rotary_kernel(x_ref, g_ref, sh_ref, sc_ref, w_ref, cos_ref, sin_ref, o_ref, h_scr, *, tn):
    j = pl.program_id(2)

    @pl.when(j == 0)
    def _():
        h_scr[...] = _norm_mod(x_ref[...], g_ref[...], sh_ref[...], sc_ref[...]).astype(BF16)

    acc = _dot(h_scr[...], w_ref[...])
    qk_blocks = 2 * RET_QK_WIDTH // tn
    half = RET_QK_DIM // 2

    @pl.when(j < qk_blocks)
    def _():
        scale = jnp.where(j < qk_blocks // 2, 1.0, RET_QK_DIM ** -0.5)
        cos = cos_ref[...] * scale
        sin = sin_ref[...] * scale
        for c in range(tn // RET_QK_DIM):
            a1 = acc[:, c * RET_QK_DIM:c * RET_QK_DIM + half]
            a2 = acc[:, c * RET_QK_DIM + half:(c + 1) * RET_QK_DIM]
            o_ref[:, c * RET_QK_DIM:c * RET_QK_DIM + half] = (a1 * cos - a2 * sin).astype(o_ref.dtype)
            o_ref[:, c * RET_QK_DIM + half:(c + 1) * RET_QK_DIM] = (a1 * sin + a2 * cos).astype(o_ref.dtype)

    @pl.when(j >= qk_blocks)
    def _():
        o_ref[...] = acc.astype(o_ref.dtype)


def _proj(x, g, sh, sc, w, qk=None, rotary=None):
    B, T, D = x.shape
    n_out = w.shape[1]
    tm = _tile(T, 2048)
    tn = _tile(n_out, 1024)
    grid = (B, T // tm, n_out // tn)
    in_specs = [
        pl.BlockSpec((None, tm, D), lambda b, i, j: (b, i, 0)),
        pl.BlockSpec((1, D), lambda b, i, j: (0, 0)),
        pl.BlockSpec((None, 1, D), lambda b, i, j: (b, 0, 0)),
        pl.BlockSpec((None, 1, D), lambda b, i, j: (b, 0, 0)),
        pl.BlockSpec((D, tn), lambda b, i, j: (0, j)),
    ]
    args = [x, g, sh, sc, w]
    if rotary is not None:
        half = RET_QK_DIM // 2
        in_specs += [pl.BlockSpec((tm, half), lambda b, i, j: (i, 0))] * 2
        args += list(rotary)
        body = functools.partial(_proj_rotary_kernel, tn=tn)
        name = "proj_rotary"
    elif qk is None:
        body = _proj_kernel
        name = "proj"
    else:
        bd, gain, n_qk = qk
        in_specs += [
            pl.BlockSpec((256, 256), lambda b, i, j: (0, 0)),
            pl.BlockSpec((1, tn), lambda b, i, j: (0, j)),
        ]
        args += [bd, gain]
        body = functools.partial(_proj_qknorm_kernel, qk_blocks=n_qk // tn, tn=tn)
        name = "proj_qknorm"
    return pl.pallas_call(
        body,
        grid=grid,
        in_specs=in_specs,
        out_specs=pl.BlockSpec((None, tm, tn), lambda b, i, j: (b, i, j)),
        out_shape=jax.ShapeDtypeStruct((B, T, n_out), BF16),
        scratch_shapes=[pltpu.VMEM((tm, D), BF16)],
        compiler_params=_cparams(("parallel", "parallel", "arbitrary")),
        name=name,
    )(*args)


def _outproj_kernel(y_ref, w_ref, x_ref, g_ref, o_ref):
    o_ref[...] = x_ref[...] + g_ref[...] * _dot(y_ref[...], w_ref[...])


def _ffn_pre_kernel(x_ref, g_ref, sh_ref, sc_ref, r_ref, h_ref, aff_ref):
    h = _norm_mod(x_ref[...], g_ref[...], sh_ref[...], sc_ref[...])
    h_hi = h.astype(BF16)
    h_ref[...] = h_hi
    h_lo = (h - h_hi.astype(F32)).astype(BF16)
    r = r_ref[...]
    r_hi = r.astype(BF16)
    r_lo = (r - r_hi.astype(F32)).astype(BF16)
    logits = _dot(h_hi, r_hi) + (_dot(h_hi, r_lo) + _dot(h_lo, r_hi))
    m = jnp.max(logits, axis=-1, keepdims=True)
    e = jnp.exp(logits - m)
    aff_ref[...] = e / jnp.sum(e, axis=-1, keepdims=True)


def _out_proj(y, w, x, gate, ng, sh, sc, router):
    B, T, K = y.shape
    D = w.shape[1]
    E = router.shape[1]
    tm = _tile(T, 1024)
    xn = pl.pallas_call(
        _outproj_kernel,
        grid=(B, T // tm),
        in_specs=[
            pl.BlockSpec((None, tm, K), lambda b, i: (b, i, 0)),
            pl.BlockSpec((K, D), lambda b, i: (0, 0)),
            pl.BlockSpec((None, tm, D), lambda b, i: (b, i, 0)),
            pl.BlockSpec((None, 1, D), lambda b, i: (b, 0, 0)),
        ],
        out_specs=pl.BlockSpec((None, tm, D), lambda b, i: (b, i, 0)),
        out_shape=jax.ShapeDtypeStruct((B, T, D), F32),
        compiler_params=_cparams(("parallel", "parallel")),
        name="out_proj",
    )(y, w, x, gate)
    tp = _tile(T, 1024)
    nt = T // tp
    h, aff = pl.pallas_call(
        _ffn_pre_kernel,
        grid=(B, nt),
        in_specs=[
            pl.BlockSpec((None, tp, D), lambda b, i: (b, i, 0)),
            pl.BlockSpec((1, D), lambda b, i: (0, 0)),
            pl.BlockSpec((None, 1, D), lambda b, i: (b, 0, 0)),
            pl.BlockSpec((None, 1, D), lambda b, i: (b, 0, 0)),
            pl.BlockSpec((D, E), lambda b, i: (0, 0)),
        ],
        out_specs=[
            pl.BlockSpec((tp, D), lambda b, i: (b * nt + i, 0)),
            pl.BlockSpec((tp, E), lambda b, i: (b * nt + i, 0)),
        ],
        out_shape=[jax.ShapeDtypeStruct((B * T, D), BF16), jax.ShapeDtypeStruct((B * T, E), F32)],
        compiler_params=_cparams(("parallel", "parallel")),
        name="ffn_pre",
    )(xn, ng, sh, sc, router)
    return xn, h, aff


def _retention_kernel(*refs, has_prev):
    if has_prev:
        q_ref, k_ref, v_ref, gate_ref, intra_ref, qdec_ref, kdec_ref, cdec_ref, prev_ref, o_ref, state = refs
    else:
        q_ref, k_ref, v_ref, gate_ref, intra_ref, qdec_ref, kdec_ref, cdec_ref, o_ref, state = refs
        prev_ref = None

    @pl.when(pl.program_id(1) == 0)
    def _():
        state[...] = jnp.zeros_like(state)

    for h in range(RET_HEADS):
        qk = slice(h * RET_QK_DIM, (h + 1) * RET_QK_DIM)
        vs = slice(h * RET_V_DIM, (h + 1) * RET_V_DIM)
        qb = q_ref[:, qk]
        kb = k_ref[:, qk]
        v = v_ref[:, vs]
        scores = _dot_nt(qb, kb) * intra_ref[h]
        st = state[h]
        o = _dot(scores.astype(BF16), v) + _dot(qb, st.astype(BF16)) * qdec_ref[h]
        state[h] = st * cdec_ref[h] + _dot_tn((kb.astype(F32) * kdec_ref[h]).astype(BF16), v)

        mu = jnp.mean(o, axis=-1, keepdims=True)
        oc = o - mu
        var = jnp.mean(oc * oc, axis=-1, keepdims=True)
        y = oc * lax.rsqrt(var + EPS) * _silu(gate_ref[:, vs].astype(F32))
        if has_prev:
            y = y + prev_ref[:, vs].astype(F32)
        o_ref[:, vs] = y.astype(o_ref.dtype)


def _retention_dir(proj, tabs, direction, prev):
    B, T, _ = proj.shape
    rc = min(RET_BLOCK, T)
    nc = T // rc
    intra, qdec, kdec, cdec = tabs
    if direction == 0:
        cidx = lambda c: c
    else:
        cidx = lambda c: nc - 1 - c
    H = RET_HEADS
    v_blk = 2 * RET_QK_WIDTH // RET_V_WIDTH
    in_specs = [
        pl.BlockSpec((None, rc, RET_QK_WIDTH), lambda b, c: (b, cidx(c), 0)),
        pl.BlockSpec((None, rc, RET_QK_WIDTH), lambda b, c: (b, cidx(c), 1)),
        pl.BlockSpec((None, rc, RET_V_WIDTH), lambda b, c: (b, cidx(c), v_blk)),
        pl.BlockSpec((None, rc, RET_V_WIDTH), lambda b, c: (b, cidx(c), v_blk + 1 + direction)),
        pl.BlockSpec((H, rc, rc), lambda b, c: (direction, 0, 0)),
        pl.BlockSpec((H, rc, 1), lambda b, c: (direction, 0, 0)),
        pl.BlockSpec((H, rc, 1), lambda b, c: (direction, 0, 0)),
        pl.BlockSpec((H, 1, 1), lambda b, c: (direction, 0, 0)),
    ]
    args = [proj, proj, proj, proj, intra, qdec, kdec, cdec]
    if prev is not None:
        in_specs.append(pl.BlockSpec((None, rc, RET_V_WIDTH), lambda b, c: (b, cidx(c), 0)))
        args.append(prev)
    return pl.pallas_call(
        functools.partial(_retention_kernel, has_prev=prev is not None),
        grid=(B, nc),
        in_specs=in_specs,
        out_specs=pl.BlockSpec((None, rc, RET_V_WIDTH), lambda b, c: (b, cidx(c), 0)),
        out_shape=jax.ShapeDtypeStruct((B, T, RET_V_WIDTH), BF16),
        scratch_shapes=[pltpu.VMEM((H, RET_QK_DIM, RET_V_DIM), F32)],
        compiler_params=_cparams(("parallel", "arbitrary")),
        name="retention_bwd" if direction else "retention_fwd",
    )(*args)


def _retention_tables(decay_logit, rc):
    lg = jax.nn.log_sigmoid(decay_logit.astype(F32).reshape(-1))
    pos = jnp.arange(rc, dtype=F32)
    rel = pos[:, None] - pos[None, :]
    l3 = lg[:, None, None]
    fwd = jnp.where(rel >= 0, jnp.exp(jnp.maximum(rel, 0.0) * l3), 0.0)
    bwd = jnp.where(rel <= 0, jnp.exp(jnp.maximum(-rel, 0.0) * l3), 0.0)
    is_bwd = (jnp.arange(2 * RET_HEADS) >= RET_HEADS)
    intra = jnp.where(is_bwd[:, None, None], bwd, fwd)
    q_pow = jnp.where(is_bwd[:, None], rc - pos[None, :], pos[None, :] + 1.0)
    k_pow = jnp.where(is_bwd[:, None], pos[None, :], rc - 1.0 - pos[None, :])
    qdec = jnp.exp(q_pow * lg[:, None])[..., None]
    kdec = jnp.exp(k_pow * lg[:, None])[..., None]
    cdec = jnp.exp(rc * lg)[:, None, None]
    return intra, qdec, kdec, cdec


def _rotary_tables(T):
    d = RET_QK_DIM
    inv = 1.0 / (ROPE_BASE ** (jnp.arange(0, d, 2, dtype=F32) / d))
    ang = jnp.arange(T, dtype=F32)[:, None] * inv[None, :]
    return jnp.cos(ang), jnp.sin(ang)


def _na_kernel(q_ref, k_ref, v_ref, bias_ref, o_ref, *, rows):
    n_groups = rows // NA_ROWS_PER_GROUP
    lane = lax.broadcasted_iota(I32, (NA_Q, LANES), 1)
    first = lane < NA_HEAD_DIM

    def group(gi, carry):
        ustart = jnp.clip(gi * NA_ROWS_PER_GROUP - NA_KH // 2, 0, rows - NA_WIN_ROWS)
        pat = jnp.where(gi == 0, 0, jnp.where(gi == n_groups - 1, 2, 1))
        kstart = pl.multiple_of(ustart * GRID_W, GRID_W)
        qstart = pl.multiple_of(gi * NA_Q, NA_Q)
        kw = k_ref[pl.ds(kstart, NA_K), :]
        vw = v_ref[pl.ds(kstart, NA_K), :]
        q = q_ref[pl.ds(qstart, NA_Q), :].astype(F32)
        q2 = jnp.concatenate([jnp.where(first, q, 0.0), jnp.where(first, 0.0, q)], axis=0).astype(BF16)
        s = _dot_nt(q2, kw) + bias_ref[pat].reshape(2 * NA_Q, NA_K)
        m = jnp.max(s, axis=-1, keepdims=True)
        e = jnp.exp(s - m)
        l = jnp.sum(e, axis=-1, keepdims=True)
        o2 = _dot(e.astype(BF16), vw) / l
        o_ref[pl.ds(qstart, NA_Q), :] = jnp.where(first, o2[:NA_Q], o2[NA_Q:]).astype(o_ref.dtype)
        return carry

    lax.fori_loop(0, n_groups, group, 0, unroll=8)


def _na_attention(qkv, bias):
    B, T, _ = qkv.shape
    rows = T // GRID_W
    assert rows % NA_ROWS_PER_GROUP == 0 and rows >= NA_WIN_ROWS + 1
    n_pairs = D_MODEL // LANES
    blk = lambda off: pl.BlockSpec((None, T, LANES), lambda p, b: (b, 0, off + p))
    return pl.pallas_call(
        functools.partial(_na_kernel, rows=rows),
        grid=(n_pairs, B),
        in_specs=[
            blk(0), blk(n_pairs), blk(2 * n_pairs),
            pl.BlockSpec((3, 2, NA_Q, NA_K), lambda p, b: (0, p, 0, 0)),
        ],
        out_specs=pl.BlockSpec((None, T, LANES), lambda p, b: (b, 0, p)),
        out_shape=jax.ShapeDtypeStruct((B, T, D_MODEL), BF16),
        compiler_params=_cparams(("parallel", "parallel")),
        name="na_attention",
    )(qkv, qkv, qkv, bias)


def _na_bias_table(rpb):
    H = rpb.shape[0]
    r = rpb.astype(F32)
    pad = GRID_W
    rp = jnp.concatenate([jnp.repeat(r[..., :1], pad, -1), r, jnp.repeat(r[..., -1:], pad, -1)], -1)
    base = NA_KW - 1 + pad
    t1 = jnp.stack([rp[..., base - qc: base - qc + GRID_W] for qc in range(GRID_W)], axis=2)
    qc = np.arange(GRID_W)[:, None]
    kc = np.arange(GRID_W)[None, :]
    ws = np.clip(qc - NA_KW // 2, 0, GRID_W - NA_KW)
    t1 = jnp.where(jnp.asarray((kc >= ws) & (kc < ws + NA_KW)), t1, NEG_INF)
    n_dr = 2 * NA_KH - 1
    t1 = jnp.concatenate([t1, jnp.full((H, 1, GRID_W, GRID_W), NEG_INF, F32)], axis=1)
    pick = np.full((3, NA_ROWS_PER_GROUP, NA_WIN_ROWS), n_dr, np.int32)
    for p in range(3):
        for i in range(NA_ROWS_PER_GROUP):
            lo = (0, i, NA_ROWS_PER_GROUP - 1)[p]
            qrow = (i, i + NA_KH // 2, i + NA_KH - 1)[p]
            for u in range(lo, lo + NA_KH):
                pick[p, i, u] = u - qrow + NA_KH - 1
    table = jnp.take(t1, jnp.asarray(pick.reshape(-1)), axis=1)
    table = table.reshape(H, 3, NA_ROWS_PER_GROUP, NA_WIN_ROWS, GRID_W, GRID_W)
    return table.transpose(1, 0, 2, 4, 3, 5).reshape(3, H, NA_Q, NA_K)


def _select_kernel(aff_ref, pos_ref, pre_ref, *, cap, n_tok, rb):
    E = N_EXPERTS
    per_row = LANES // E
    rows = n_tok // per_row
    bits = pltpu.bitcast(aff_ref[...], I32)
    lane = lax.broadcasted_iota(I32, (rows, LANES), 1)
    tok = lax.broadcasted_iota(I32, (rows, LANES), 0) * per_row + lane // E

    def count(pred):
        c = jnp.sum(pred.astype(I32), axis=0, keepdims=True)
        c = jnp.broadcast_to(c, (8, LANES))
        sh = E
        while sh < LANES:
            c = c + pltpu.roll(c, sh, 1)
            sh *= 2
        return c[0:1]

    def thr_step(i, thr):
        cand = thr | jnp.left_shift(jnp.int32(1), 30 - i)
        return jnp.where(count(bits >= cand) >= cap, cand, thr)

    thr = lax.fori_loop(0, 31, thr_step, jnp.zeros((1, LANES), I32))
    gt = bits > thr
    eq = bits == thr
    need = cap - count(gt)

    nbits = max(1, int(math.ceil(math.log2(n_tok))))

    def tie_step(i, jmax):
        cand = jmax + jnp.left_shift(jnp.int32(1), nbits - 1 - i)
        return jnp.where(count(jnp.logical_and(eq, tok < cand)) < need, cand, jmax)

    jmax = lax.fori_loop(0, nbits, tie_step, jnp.zeros((1, LANES), I32))

    li = lax.broadcasted_iota(I32, (LANES, LANES), 0)
    lj = lax.broadcasted_iota(I32, (LANES, LANES), 1)
    same_e = (li % E) == (lj % E)
    a_before = _ones_where(jnp.logical_and(same_e, li // E < lj // E))
    a_all = _ones_where(same_e)
    ri = lax.broadcasted_iota(I32, (rb, rb), 0)
    rj = lax.broadcasted_iota(I32, (rb, rb), 1)
    lower = _ones_where(rj < ri)

    running = jnp.zeros((1, LANES), F32)
    for blk in range(rows // rb):
        sl = slice(blk * rb, (blk + 1) * rb)
        b_blk = pltpu.bitcast(aff_ref[sl, :], I32)
        t_blk = ((lax.broadcasted_iota(I32, (rb, LANES), 0) + blk * rb) * per_row
                 + lax.broadcasted_iota(I32, (rb, LANES), 1) // E)
        keep = jnp.logical_or(b_blk > thr, jnp.logical_and(b_blk == thr, t_blk <= jmax))
        kb = _ones_where(keep)
        row_tot = _dot(kb, a_all)
        within = _dot(lower, row_tot.astype(BF16))
        pre = running + within + _dot(kb, a_before)
        running = running + jnp.sum(row_tot, axis=0, keepdims=True)
        pre_i = pre.astype(I32)
        pre_ref[sl, :] = pre_i
        pos_ref[sl, :] = jnp.where(keep, pre_i, -1)


def _select(aff, cap):
    n_tok, E = aff.shape
    rows = n_tok * E // LANES
    rb = min(256, rows)
    aff8 = aff.reshape(rows, LANES)
    pos, pre = pl.pallas_call(
        functools.partial(_select_kernel, cap=cap, n_tok=n_tok, rb=rb),
        out_shape=[jax.ShapeDtypeStruct((rows, LANES), I32)] * 2,
        compiler_params=pltpu.CompilerParams(vmem_limit_bytes=VMEM_LIMIT),
        name="expert_select",
    )(aff8)
    return pos.reshape(n_tok, E), pre.reshape(n_tok, E)


DISPATCH_GROUP = 4


def _dispatch_slab(tn):
    return tn // 4 + BF16_SUBLANES


def _dispatch_rounds(tn):
    return -(-(tn + BF16_SUBLANES) // _dispatch_slab(tn))


def _dispatch_kernel(ws_ref, need_ref, pos_ref, aff_ref, h_ref, xe_ref, gs_ref, *, n_tiles, sub, batch):
    eg = pl.program_id(0)
    tb = pl.program_id(1)
    group, _, tn = pos_ref.shape
    head = BF16_SUBLANES
    slab = _dispatch_slab(tn)
    n_rounds = _dispatch_rounds(tn)

    @pl.when(tb == 0)
    def _():
        xe_ref[...] = jnp.zeros_like(xe_ref)
        gs_ref[...] = jnp.zeros_like(gs_ref)

    def one_round(s, r):
        t = tb * sub + s
        hs = h_ref[pl.ds(pl.multiple_of(s * tn, tn), tn), :]
        row = lax.broadcasted_iota(I32, (slab, tn), 0) + r * slab
        ws = [pl.multiple_of(ws_ref[(eg * group + k) * n_tiles + t], head) for k in range(group)]
        hits = [row == pos_ref[k, pl.ds(s, 1), :] - ws[k] for k in range(group)]
        rows = _dot(jnp.concatenate([_ones_where(hk) for hk in hits], axis=0), hs)
        for k in range(group):
            rk = rows[k * slab:(k + 1) * slab]
            gk = jnp.sum(jnp.where(hits[k], aff_ref[k, pl.ds(s, 1), :], 0.0), axis=1, keepdims=True)
            base = pl.multiple_of(ws[k] + r * slab, head)
            acc_rows = head if r == 0 else slab
            xe_ref[k, pl.ds(base, acc_rows), :] = (
                xe_ref[k, pl.ds(base, acc_rows), :].astype(F32) + rk[:acc_rows]).astype(xe_ref.dtype)
            gs_ref[k, pl.ds(base, acc_rows), :] = gs_ref[k, pl.ds(base, acc_rows), :] + gk[:acc_rows]
            if r == 0:
                rest = pl.multiple_of(base + head, head)
                xe_ref[k, pl.ds(rest, slab - head), :] = rk[head:].astype(xe_ref.dtype)
                gs_ref[k, pl.ds(rest, slab - head), :] = gk[head:]

    def tile_batch(sb, carry):
        for u in range(batch):
            one_round(sb * batch + u, 0)
        for u in range(batch):
            s = sb * batch + u
            need = need_ref[eg * n_tiles + tb * sub + s]
            for r in range(1, n_rounds):
                pl.when(need > r * slab)(functools.partial(one_round, s, r))
        return carry

    lax.fori_loop(0, sub // batch, tile_batch, 0)


def _dispatch(wstart, need, pos_t, aff_t, h, cap, tn):
    E, n_tiles, _ = pos_t.shape
    N, D = h.shape
    rows = cap + _dispatch_rounds(tn) * _dispatch_slab(tn)
    sub = 8 if n_tiles % 8 == 0 else n_tiles
    group = DISPATCH_GROUP
    resident = dict(pipeline_mode=pl.Buffered(1))
    grid_spec = pltpu.PrefetchScalarGridSpec(
        num_scalar_prefetch=2,
        grid=(E // group, n_tiles // sub),
        in_specs=[
            pl.BlockSpec((group, sub, tn), lambda e, t, ws, nd: (e, t, 0)),
            pl.BlockSpec((group, sub, tn), lambda e, t, ws, nd: (e, t, 0)),
            pl.BlockSpec((sub * tn, D), lambda e, t, ws, nd: (t, 0)),
        ],
        out_specs=[
            pl.BlockSpec((group, rows, D), lambda e, t, ws, nd: (e, 0, 0), **resident),
            pl.BlockSpec((group, rows, 1), lambda e, t, ws, nd: (e, 0, 0), **resident),
        ],
    )
    return pl.pallas_call(
        functools.partial(_dispatch_kernel, n_tiles=n_tiles, sub=sub, batch=8 if sub % 8 == 0 else 1),
        grid_spec=grid_spec,
        out_shape=[jax.ShapeDtypeStruct((E, rows, D), BF16), jax.ShapeDtypeStruct((E, rows, 1), F32)],
        compiler_params=_cparams(("parallel", "arbitrary")),
        name="moe_dispatch",
    )(wstart, need, pos_t, aff_t, h)


def _ffn_kernel(x_ref, gs_ref, wg_ref, wu_ref, wd_ref, o_ref, acc, *, n_f):
    f = pl.program_id(2)

    @pl.when(f == 0)
    def _():
        acc[...] = jnp.zeros_like(acc)

    x = x_ref[...]
    hid = _silu(_dot(x, wg_ref[...].astype(BF16))) * _dot(x, wu_ref[...].astype(BF16))
    acc[...] += _dot(hid.astype(BF16), wd_ref[...].astype(BF16))

    @pl.when(f == n_f - 1)
    def _():
        o_ref[...] = (acc[...] * gs_ref[...]).astype(o_ref.dtype)


def _expert_ffn(xe, gs, wg, wu, wd, layer, C):
    E, _, D = xe.shape
    F = wg.shape[-1]
    tm = _tile(C, 2048)
    tf = _tile(F, 512)
    n_f = F // tf
    return pl.pallas_call(
        functools.partial(_ffn_kernel, n_f=n_f),
        grid=(E, C // tm, n_f),
        in_specs=[
            pl.BlockSpec((None, tm, D), lambda e, m, f: (e, m, 0)),
            pl.BlockSpec((None, tm, 1), lambda e, m, f: (e, m, 0)),
            pl.BlockSpec((None, None, D, tf), lambda e, m, f: (layer, e, 0, f)),
            pl.BlockSpec((None, None, D, tf), lambda e, m, f: (layer, e, 0, f)),
            pl.BlockSpec((None, None, tf, D), lambda e, m, f: (layer, e, f, 0)),
        ],
        out_specs=pl.BlockSpec((None, tm, D), lambda e, m, f: (e, m, 0)),
        out_shape=jax.ShapeDtypeStruct((E, C, D), BF16),
        scratch_shapes=[pltpu.VMEM((tm, D), F32)],
        compiler_params=_cparams(("parallel", "parallel", "arbitrary")),
        name="expert_ffn",
    )(xe, gs, wg, wu, wd)


COMBINE_SLAB = LANES
COMBINE_FETCH = LANES
COMBINE_PARTS = 4


def _combine_kernel(ws_ref, end_ref, pos_ref, ye_any, x_ref, g_ref, o_ref, *scratch,
                    n_tiles, n_exp, win, slab, fetch):
    hit_scrs = scratch[:-3]
    y_buf, late_scr, sem = scratch[-3:]
    per = n_exp // len(hit_scrs)
    t = pl.program_id(0)
    tn = pos_ref.shape[0]
    slot = t % 2

    def window_copy(tile, buf, e):
        ws = pl.multiple_of(ws_ref[e * n_tiles + tile], BF16_SUBLANES)
        return pltpu.make_async_copy(ye_any.at[e, pl.ds(ws, fetch), :],
                                     y_buf.at[buf, pl.ds(e * slab, fetch), :], sem.at[buf])

    @pl.when(t == 0)
    def _():
        y_buf[...] = jnp.zeros_like(y_buf)
        for e in range(n_exp):
            window_copy(0, 0, e).start()

    @pl.when(t + 1 < n_tiles)
    def _():
        for e in range(n_exp):
            window_copy(t + 1, 1 - slot, e).start()

    lane = lax.broadcasted_iota(I32, (tn, slab), 1)
    for gi, hit_scr in enumerate(hit_scrs):
        for k in range(per):
            e = gi * per + k
            prel = pos_ref[:, e:e + 1] - ws_ref[e * n_tiles + t]
            hit_scr[:, k * slab:(k + 1) * slab] = _ones_where(lane == prel)

    if fetch == slab:
        pltpu.make_async_copy(y_buf.at[1 - slot], y_buf.at[slot], sem.at[slot]).wait()
    else:
        for e in range(n_exp):
            window_copy(t, slot, e).wait()

    acc = None
    for gi, hit_scr in enumerate(hit_scrs):
        part = _dot(hit_scr[...], y_buf[slot, gi * per * slab:(gi + 1) * per * slab, :])
        acc = part if acc is None else acc + part
    o_ref[...] = x_ref[...] + g_ref[...] * acc

    if win > fetch:
        for e in range(n_exp):
            ws = ws_ref[e * n_tiles + t]

            @pl.when(end_ref[e * n_tiles + t] - ws > fetch)
            def _():
                late = pl.multiple_of(ws + fetch, BF16_SUBLANES)
                pltpu.sync_copy(ye_any.at[e, pl.ds(late, win - fetch), :], late_scr)
                prel = pos_ref[:, e:e + 1] - ws - fetch
                hit = _ones_where(lax.broadcasted_iota(I32, (tn, win - fetch), 1) == prel)
                o_ref[...] += g_ref[...] * _dot(hit, late_scr[...])


def _combine(wstart, wend, pos, ye, x, gate, tn):
    E, C, D = ye.shape
    B, T, _ = x.shape
    N = B * T
    n_tiles = N // tn
    tpb = T // tn
    win = tn + BF16_SUBLANES
    slab = min(COMBINE_SLAB, win)
    fetch = min(COMBINE_FETCH, slab)
    grid_spec = pltpu.PrefetchScalarGridSpec(
        num_scalar_prefetch=2,
        grid=(n_tiles,),
        in_specs=[pl.BlockSpec((tn, E), lambda t, ws, we: (t, 0)),
                  pl.BlockSpec(memory_space=pl.ANY),
                  pl.BlockSpec((None, tn, D), lambda t, ws, we: (t // tpb, t % tpb, 0)),
                  pl.BlockSpec((None, 1, D), lambda t, ws, we: (t // tpb, 0, 0))],
        out_specs=pl.BlockSpec((None, tn, D), lambda t, ws, we: (t // tpb, t % tpb, 0)),
        scratch_shapes=[pltpu.VMEM((tn, E // COMBINE_PARTS * slab), BF16)] * COMBINE_PARTS
        + [pltpu.VMEM((2, E * slab, D), BF16),
           pltpu.VMEM((win - fetch, D), BF16),
           pltpu.SemaphoreType.DMA((2,))],
    )
    return pl.pallas_call(
        functools.partial(_combine_kernel, n_tiles=n_tiles, n_exp=E, win=win, slab=slab, fetch=fetch),
        grid_spec=grid_spec,
        out_shape=jax.ShapeDtypeStruct((B, T, D), F32),
        compiler_params=_cparams(("arbitrary",)),
        name="moe_combine",
    )(wstart, wend, pos, ye, x, gate)


def _moe_tile(n_tok, cap):
    tn = 256
    while tn + BF16_SUBLANES > cap or n_tok % tn:
        tn //= 2
    return tn


def _moe_layer(x, h, aff, gate, wg, wu, wd, layer):
    B, T, D = x.shape
    N = B * T
    E = aff.shape[1]
    cap = EC_CAPACITY_FACTOR * N // E
    tn = _moe_tile(T, cap)
    win = tn + BF16_SUBLANES
    pos, pre = _select(aff, cap)
    start = pre[::tn].T
    aligned = (start // BF16_SUBLANES * BF16_SUBLANES).astype(I32)
    wdisp = aligned.reshape(-1)
    wstart = jnp.minimum(aligned, cap - win).reshape(-1)
    end = jnp.concatenate([start[:, 1:], jnp.full((E, 1), cap, I32)], axis=1).astype(I32)
    wend = end.reshape(-1)
    need = (end - aligned).reshape(E // DISPATCH_GROUP, DISPATCH_GROUP, -1).max(axis=1).reshape(-1)
    pos_t = pos.T.reshape(E, N // tn, tn)
    aff_t = aff.T.reshape(E, N // tn, tn)
    xe, gs = _dispatch(wdisp, need, pos_t, aff_t, h, cap, tn)
    ye = _expert_ffn(xe, gs, wg, wu, wd, layer, cap)
    return _combine(wstart, wend, pos, ye, x, gate, tn)


def _trunk(x, mod, p):
    B, T, D = x.shape
    rc = min(RET_BLOCK, T)
    cos, sin = _rotary_tables(T)
    for i in range(DEPTH):
        sh1, sc1, g1, sh2, sc2, g2 = [mod[i, :, k][:, None, :] for k in range(6)]
        j = i // 2
        if i % 2 == 0:
            proj = _proj(x, p["norm_mix_g"][i][None], sh1, sc1, p["ret_w_in"][j], rotary=(cos, sin))
            tabs = _retention_tables(p["ret_decay_logit"][j], rc)
            y_f = _retention_dir(proj, tabs, 0, None)
            y = _retention_dir(proj, tabs, 1, y_f)
            w_out = p["ret_w_out"][j]
        else:
            qkv = _proj(x, p["norm_mix_g"][i][None], sh1, sc1, p["na_w_in"][j],
                        qk=(p["na_bd"], p["na_gain"][j], 2 * D))
            y = _na_attention(qkv, p["na_bias"][j])
            w_out = p["na_w_out"][j]
        x, h, aff = _out_proj(y, w_out, x, g1, p["norm_ffn_g"][i][None], sh2, sc2, p["moe_router"][i])
        x = _moe_layer(x, h, aff, g2, p["moe_w_gate"], p["moe_w_up"], p["moe_w_down"], i)
    return x


def kernel(x_prompt, x_sample, c_prompt, c_sample, norm_mix_g, norm_ffn_g, ada_w, ada_b, ret_w_in, ret_decay_logit, ret_w_out, na_w_in, na_q_gain, na_k_gain, na_rpb, na_w_out, moe_router, moe_w_gate, moe_w_up, moe_w_down):
    D = D_MODEL
    bp, bs = c_prompt.shape[0], c_sample.shape[0]
    c_all = jnp.concatenate([c_prompt, c_sample], axis=0)
    pad = (-c_all.shape[0]) % 8
    if pad:
        c_all = jnp.pad(c_all, ((0, pad), (0, 0)))
    mod = _ada_mod(c_all, ada_w, ada_b).reshape(DEPTH, c_all.shape[0], 6, D)

    heads_row = lambda v: jnp.tile(v.astype(F32), (1, NA_HEADS))
    na_gain = jnp.concatenate(
        [heads_row(na_q_gain) * (NA_HEAD_DIM ** -0.5), heads_row(na_k_gain),
         jnp.ones((na_q_gain.shape[0], D), F32)], axis=-1)[:, None, :]
    bd = np.kron(np.eye(256 // NA_HEAD_DIM), np.ones((NA_HEAD_DIM, NA_HEAD_DIM)))
    p = dict(
        norm_mix_g=norm_mix_g, norm_ffn_g=norm_ffn_g,
        ret_w_in=ret_w_in.astype(BF16), ret_decay_logit=ret_decay_logit, ret_w_out=ret_w_out.astype(BF16),
        na_w_in=na_w_in.astype(BF16), na_w_out=na_w_out.astype(BF16),
        na_gain=na_gain, na_bd=jnp.asarray(bd, BF16),
        na_bias=jnp.stack([_na_bias_table(na_rpb[l]) for l in range(na_rpb.shape[0])]),
        moe_router=moe_router,
        moe_w_gate=moe_w_gate, moe_w_up=moe_w_up, moe_w_down=moe_w_down,
    )
    y_prompt = _trunk(x_prompt, mod[:, :bp], p)
    y_sample = _trunk(x_sample, mod[:, bp:bp + bs], p)
    return (y_prompt, y_sample)
```

```python
import functools
import math

import numpy as np
import jax
import jax.numpy as jnp
from jax import lax
from jax.experimental import pallas as pl
from jax.experimental.pallas import tpu as pltpu

F32 = jnp.float32
BF16 = jnp.bfloat16
I32 = jnp.int32

D_MODEL = 1024
DEPTH = 4

RET_HEADS = 4
RET_QK_DIM = 256
RET_V_DIM = 512
RET_QK_WIDTH = RET_HEADS * RET_QK_DIM
RET_V_WIDTH = RET_HEADS * RET_V_DIM
RET_IN_COLS = 2 * RET_QK_WIDTH + 3 * RET_V_WIDTH
ROPE_BASE = 10000.0
RET_BLOCK = 256

NA_HEADS = 16
NA_HEAD_DIM = 64
NA_KH = 8
NA_KW = 16
GRID_W = 64
NA_ROWS_PER_GROUP = 4
NA_WIN_ROWS = NA_ROWS_PER_GROUP + NA_KH - 1
NA_Q = NA_ROWS_PER_GROUP * GRID_W
NA_K = NA_WIN_ROWS * GRID_W

N_EXPERTS = 16
EC_CAPACITY_FACTOR = 2
D_FF = 2 * D_MODEL

EPS = 1e-6
NEG_INF = -1e30

LANES = 128
BF16_SUBLANES = 16
VMEM_LIMIT = 56 * 1024 * 1024

HIGHEST = lax.Precision.HIGHEST


def _cparams(sem):
    return pltpu.CompilerParams(dimension_semantics=sem, vmem_limit_bytes=VMEM_LIMIT)


def _tile(n, pref):
    t = min(pref, n)
    while n % t:
        t //= 2
    return t


def _silu(x):
    return (0.5 * x) * (1.0 + jnp.tanh(0.5 * x))


def _ones_where(mask):
    return jnp.where(mask, 1.0, 0.0).astype(BF16)


def _dot(a, b):
    return jnp.dot(a, b, preferred_element_type=F32)


def _dot_nt(a, b):
    return lax.dot_general(a, b, (((1,), (1,)), ((), ())), preferred_element_type=F32)


def _dot_tn(a, b):
    return lax.dot_general(a, b, (((0,), (0,)), ((), ())), preferred_element_type=F32)


def _ada_kernel(c_ref, w_ref, b_ref, o_ref):
    ca = _silu(c_ref[...])
    o_ref[...] = jnp.dot(ca, w_ref[...], preferred_element_type=F32, precision=HIGHEST) + b_ref[...]


def _ada_mod(c_all, ada_w, ada_b):
    R, D = c_all.shape
    n_out = ada_w.shape[-1]
    tn = min(1536, n_out)
    return pl.pallas_call(
        _ada_kernel,
        grid=(DEPTH, n_out // tn),
        in_specs=[
            pl.BlockSpec((R, D), lambda l, j: (0, 0)),
            pl.BlockSpec((None, D, tn), lambda l, j: (l, 0, j)),
            pl.BlockSpec((None, 1, tn), lambda l, j: (l, 0, j)),
        ],
        out_specs=pl.BlockSpec((None, R, tn), lambda l, j: (l, 0, j)),
        out_shape=jax.ShapeDtypeStruct((DEPTH, R, n_out), F32),
        compiler_params=_cparams(("arbitrary", "arbitrary")),
        name="ada_mod",
    )(c_all, ada_w, ada_b.reshape(DEPTH, 1, n_out))


def _norm_mod(x, g, sh, sc):
    ms = jnp.mean(x * x, axis=-1, keepdims=True)
    y = x * lax.rsqrt(ms + EPS) * g
    return y * (1.0 + sc) + sh


def _proj_kernel(x_ref, g_ref, sh_ref, sc_ref, w_ref, o_ref, h_scr):
    @pl.when(pl.program_id(2) == 0)
    def _():
        h_scr[...] = _norm_mod(x_ref[...], g_ref[...], sh_ref[...], sc_ref[...]).astype(BF16)

    o_ref[...] = _dot(h_scr[...], w_ref[...]).astype(o_ref.dtype)


def _proj_qknorm_kernel(x_ref, g_ref, sh_ref, sc_ref, w_ref, bd_ref, gain_ref, o_ref, h_scr, *, qk_blocks, tn):
    j = pl.program_id(2)

    @pl.when(j == 0)
    def _():
        h_scr[...] = _norm_mod(x_ref[...], g_ref[...], sh_ref[...], sc_ref[...]).astype(BF16)

    acc = _dot(h_scr[...], w_ref[...])

    @pl.when(j < qk_blocks)
    def _():
        for c in range(tn // 256):
            a = acc[:, c * 256:(c + 1) * 256]
            ss = _dot((a * a).astype(BF16), bd_ref[...])
            r = lax.rsqrt(ss * (1.0 / NA_HEAD_DIM) + EPS)
            o_ref[:, c * 256:(c + 1) * 256] = (a * r * gain_ref[:, c * 256:(c + 1) * 256]).astype(o_ref.dtype)

    @pl.when(j >= qk_blocks)
    def _():
        o_ref[...] = acc.astype(o_ref.dtype)


def _proj_rotary_kernel(x_ref, g_ref, sh_ref, sc_ref, w_ref, cos_ref, sin_ref, o_ref, h_scr, *, tn):
    j = pl.program_id(2)

    @pl.when(j == 0)
    def _():
        h_scr[...] = _norm_mod(x_ref[...], g_ref[...], sh_ref[...], sc_ref[...]).astype(BF16)

    acc = _dot(h_scr[...], w_ref[...])
    qk_blocks = 2 * RET_QK_WIDTH // tn
    half = RET_QK_DIM // 2

    @pl.when(j < qk_blocks)
    def _():
        scale = jnp.where(j < qk_blocks // 2, 1.0, RET_QK_DIM ** -0.5)
        cos = cos_ref[...] * scale
        sin = sin_ref[...] * scale
        for c in range(tn // RET_QK_DIM):
            a1 = acc[:, c * RET_QK_DIM:c * RET_QK_DIM + half]
            a2 = acc[:, c * RET_QK_DIM + half:(c + 1) * RET_QK_DIM]
            o_ref[:, c * RET_QK_DIM:c * RET_QK_DIM + half] = (a1 * cos - a2 * sin).astype(o_ref.dtype)
            o_ref[:, c * RET_QK_DIM + half:(c + 1) * RET_QK_DIM] = (a1 * sin + a2 * cos).astype(o_ref.dtype)

    @pl.when(j >= qk_blocks)
    def _():
        o_ref[...] = acc.astype(o_ref.dtype)


def _proj(x, g, sh, sc, w, qk=None, rotary=None):
    B, T, D = x.shape
    n_out = w.shape[1]
    tm = _tile(T, 2048)
    tn = _tile(n_out, 1024)
    grid = (B, T // tm, n_out // tn)
    in_specs = [
        pl.BlockSpec((None, tm, D), lambda b, i, j: (b, i, 0)),
        pl.BlockSpec((1, D), lambda b, i, j: (0, 0)),
        pl.BlockSpec((None, 1, D), lambda b, i, j: (b, 0, 0)),
        pl.BlockSpec((None, 1, D), lambda b, i, j: (b, 0, 0)),
        pl.BlockSpec((D, tn), lambda b, i, j: (0, j)),
    ]
    args = [x, g, sh, sc, w]
    if rotary is not None:
        half = RET_QK_DIM // 2
        in_specs += [pl.BlockSpec((tm, half), lambda b, i, j: (i, 0))] * 2
        args += list(rotary)
        body = functools.partial(_proj_rotary_kernel, tn=tn)
        name = "proj_rotary"
    elif qk is None:
        body = _proj_kernel
        name = "proj"
    else:
        bd, gain, n_qk = qk
        in_specs += [
            pl.BlockSpec((256, 256), lambda b, i, j: (0, 0)),
            pl.BlockSpec((1, tn), lambda b, i, j: (0, j)),
        ]
        args += [bd, gain]
        body = functools.partial(_proj_qknorm_kernel, qk_blocks=n_qk // tn, tn=tn)
        name = "proj_qknorm"
    return pl.pallas_call(
        body,
        grid=grid,
        in_specs=in_specs,
        out_specs=pl.BlockSpec((None, tm, tn), lambda b, i, j: (b, i, j)),
        out_shape=jax.ShapeDtypeStruct((B, T, n_out), BF16),
        scratch_shapes=[pltpu.VMEM((tm, D), BF16)],
        compiler_params=_cparams(("parallel", "parallel", "arbitrary")),
        name=name,
    )(*args)


def _outproj_kernel(y_ref, w_ref, x_ref, g_ref, o_ref):
    o_ref[...] = x_ref[...] + g_ref[...] * _dot(y_ref[...], w_ref[...])


def _ffn_pre_kernel(x_ref, g_ref, sh_ref, sc_ref, r_ref, h_ref, aff_ref):
    h = _norm_mod(x_ref[...], g_ref[...], sh_ref[...], sc_ref[...])
    h_hi = h.astype(BF16)
    h_ref[...] = h_hi
    h_lo = (h - h_hi.astype(F32)).astype(BF16)
    r = r_ref[...]
    r_hi = r.astype(BF16)
    r_lo = (r - r_hi.astype(F32)).astype(BF16)
    logits = _dot(h_hi, r_hi) + (_dot(h_hi, r_lo) + _dot(h_lo, r_hi))
    m = jnp.max(logits, axis=-1, keepdims=True)
    e = jnp.exp(logits - m)
    aff_ref[...] = e / jnp.sum(e, axis=-1, keepdims=True)


def _out_proj(y, w, x, gate, ng, sh, sc, router):
    B, T, K = y.shape
    D = w.shape[1]
    E = router.shape[1]
    tm = _tile(T, 1024)
    xn = pl.pallas_call(
        _outproj_kernel,
        grid=(B, T // tm),
        in_specs=[
            pl.BlockSpec((None, tm, K), lambda b, i: (b, i, 0)),
            pl.BlockSpec((K, D), lambda b, i: (0, 0)),
            pl.BlockSpec((None, tm, D), lambda b, i: (b, i, 0)),
            pl.BlockSpec((None, 1, D), lambda b, i: (b, 0, 0)),
        ],
        out_specs=pl.BlockSpec((None, tm, D), lambda b, i: (b, i, 0)),
        out_shape=jax.ShapeDtypeStruct((B, T, D), F32),
        compiler_params=_cparams(("parallel", "parallel")),
        name="out_proj",
    )(y, w, x, gate)
    tp = _tile(T, 1024)
    nt = T // tp
    h, aff = pl.pallas_call(
        _ffn_pre_kernel,
        grid=(B, nt),
        in_specs=[
            pl.BlockSpec((None, tp, D), lambda b, i: (b, i, 0)),
            pl.BlockSpec((1, D), lambda b, i: (0, 0)),
            pl.BlockSpec((None, 1, D), lambda b, i: (b, 0, 0)),
            pl.BlockSpec((None, 1, D), lambda b, i: (b, 0, 0)),
            pl.BlockSpec((D, E), lambda b, i: (0, 0)),
        ],
        out_specs=[
            pl.BlockSpec((tp, D), lambda b, i: (b * nt + i, 0)),
            pl.BlockSpec((tp, E), lambda b, i: (b * nt + i, 0)),
        ],
        out_shape=[jax.ShapeDtypeStruct((B * T, D), BF16), jax.ShapeDtypeStruct((B * T, E), F32)],
        compiler_params=_cparams(("parallel", "parallel")),
        name="ffn_pre",
    )(xn, ng, sh, sc, router)
    return xn, h, aff


def _retention_kernel(*refs, has_prev):
    if has_prev:
        q_ref, k_ref, v_ref, gate_ref, intra_ref, qdec_ref, kdec_ref, cdec_ref, prev_ref, o_ref, state = refs
    else:
        q_ref, k_ref, v_ref, gate_ref, intra_ref, qdec_ref, kdec_ref, cdec_ref, o_ref, state = refs
        prev_ref = None

    @pl.when(pl.program_id(1) == 0)
    def _():
        state[...] = jnp.zeros_like(state)

    for h in range(RET_HEADS):
        qk = slice(h * RET_QK_DIM, (h + 1) * RET_QK_DIM)
        vs = slice(h * RET_V_DIM, (h + 1) * RET_V_DIM)
        qb = q_ref[:, qk]
        kb = k_ref[:, qk]
        v = v_ref[:, vs]
        scores = _dot_nt(qb, kb) * intra_ref[h]
        st = state[h]
        o = _dot(scores.astype(BF16), v) + _dot(qb, st.astype(BF16)) * qdec_ref[h]
        state[h] = st * cdec_ref[h] + _dot_tn((kb.astype(F32) * kdec_ref[h]).astype(BF16), v)

        mu = jnp.mean(o, axis=-1, keepdims=True)
        oc = o - mu
        var = jnp.mean(oc * oc, axis=-1, keepdims=True)
        y = oc * lax.rsqrt(var + EPS) * _silu(gate_ref[:, vs].astype(F32))
        if has_prev:
            y = y + prev_ref[:, vs].astype(F32)
        o_ref[:, vs] = y.astype(o_ref.dtype)


def _retention_dir(proj, tabs, direction, prev):
    B, T, _ = proj.shape
    rc = min(RET_BLOCK, T)
    nc = T // rc
    intra, qdec, kdec, cdec = tabs
    if direction == 0:
        cidx = lambda c: c
    else:
        cidx = lambda c: nc - 1 - c
    H = RET_HEADS
    v_blk = 2 * RET_QK_WIDTH // RET_V_WIDTH
    in_specs = [
        pl.BlockSpec((None, rc, RET_QK_WIDTH), lambda b, c: (b, cidx(c), 0)),
        pl.BlockSpec((None, rc, RET_QK_WIDTH), lambda b, c: (b, cidx(c), 1)),
        pl.BlockSpec((None, rc, RET_V_WIDTH), lambda b, c: (b, cidx(c), v_blk)),
        pl.BlockSpec((None, rc, RET_V_WIDTH), lambda b, c: (b, cidx(c), v_blk + 1 + direction)),
        pl.BlockSpec((H, rc, rc), lambda b, c: (direction, 0, 0)),
        pl.BlockSpec((H, rc, 1), lambda b, c: (direction, 0, 0)),
        pl.BlockSpec((H, rc, 1), lambda b, c: (direction, 0, 0)),
        pl.BlockSpec((H, 1, 1), lambda b, c: (direction, 0, 0)),
    ]
    args = [proj, proj, proj, proj, intra, qdec, kdec, cdec]
    if prev is not None:
        in_specs.append(pl.BlockSpec((None, rc, RET_V_WIDTH), lambda b, c: (b, cidx(c), 0)))
        args.append(prev)
    return pl.pallas_call(
        functools.partial(_retention_kernel, has_prev=prev is not None),
        grid=(B, nc),
        in_specs=in_specs,
        out_specs=pl.BlockSpec((None, rc, RET_V_WIDTH), lambda b, c: (b, cidx(c), 0)),
        out_shape=jax.ShapeDtypeStruct((B, T, RET_V_WIDTH), BF16),
        scratch_shapes=[pltpu.VMEM((H, RET_QK_DIM, RET_V_DIM), F32)],
        compiler_params=_cparams(("parallel", "arbitrary")),
        name="retention_bwd" if direction else "retention_fwd",
    )(*args)


def _retention_tables(decay_logit, rc):
    lg = jax.nn.log_sigmoid(decay_logit.astype(F32).reshape(-1))
    pos = jnp.arange(rc, dtype=F32)
    rel = pos[:, None] - pos[None, :]
    l3 = lg[:, None, None]
    fwd = jnp.where(rel >= 0, jnp.exp(jnp.maximum(rel, 0.0) * l3), 0.0)
    bwd = jnp.where(rel <= 0, jnp.exp(jnp.maximum(-rel, 0.0) * l3), 0.0)
    is_bwd = (jnp.arange(2 * RET_HEADS) >= RET_HEADS)
    intra = jnp.where(is_bwd[:, None, None], bwd, fwd)
    q_pow = jnp.where(is_bwd[:, None], rc - pos[None, :], pos[None, :] + 1.0)
    k_pow = jnp.where(is_bwd[:, None], pos[None, :], rc - 1.0 - pos[None, :])
    qdec = jnp.exp(q_pow * lg[:, None])[..., None]
    kdec = jnp.exp(k_pow * lg[:, None])[..., None]
    cdec = jnp.exp(rc * lg)[:, None, None]
    return intra, qdec, kdec, cdec


def _rotary_tables(T):
    d = RET_QK_DIM
    inv = 1.0 / (ROPE_BASE ** (jnp.arange(0, d, 2, dtype=F32) / d))
    ang = jnp.arange(T, dtype=F32)[:, None] * inv[None, :]
    return jnp.cos(ang), jnp.sin(ang)


def _na_kernel(q_ref, k_ref, v_ref, bias_ref, o_ref, *, rows):
    n_groups = rows // NA_ROWS_PER_GROUP
    lane = lax.broadcasted_iota(I32, (NA_Q, LANES), 1)
    first = lane < NA_HEAD_DIM

    def group(gi, carry):
        ustart = jnp.clip(gi * NA_ROWS_PER_GROUP - NA_KH // 2, 0, rows - NA_WIN_ROWS)
        pat = jnp.where(gi == 0, 0, jnp.where(gi == n_groups - 1, 2, 1))
        kstart = pl.multiple_of(ustart * GRID_W, GRID_W)
        qstart = pl.multiple_of(gi * NA_Q, NA_Q)
        kw = k_ref[pl.ds(kstart, NA_K), :]
        vw = v_ref[pl.ds(kstart, NA_K), :]
        q = q_ref[pl.ds(qstart, NA_Q), :].astype(F32)
        q2 = jnp.concatenate([jnp.where(first, q, 0.0), jnp.where(first, 0.0, q)], axis=0).astype(BF16)
        s = _dot_nt(q2, kw) + bias_ref[pat].reshape(2 * NA_Q, NA_K)
        m = jnp.max(s, axis=-1, keepdims=True)
        e = jnp.exp(s - m)
        l = jnp.sum(e, axis=-1, keepdims=True)
        o2 = _dot(e.astype(BF16), vw) / l
        o_ref[pl.ds(qstart, NA_Q), :] = jnp.where(first, o2[:NA_Q], o2[NA_Q:]).astype(o_ref.dtype)
        return carry

    lax.fori_loop(0, n_groups, group, 0, unroll=8)


def _na_attention(qkv, bias):
    B, T, _ = qkv.shape
    rows = T // GRID_W
    assert rows % NA_ROWS_PER_GROUP == 0 and rows >= NA_WIN_ROWS + 1
    n_pairs = D_MODEL // LANES
    blk = lambda off: pl.BlockSpec((None, T, LANES), lambda p, b: (b, 0, off + p))
    return pl.pallas_call(
        functools.partial(_na_kernel, rows=rows),
        grid=(n_pairs, B),
        in_specs=[
            blk(0), blk(n_pairs), blk(2 * n_pairs),
            pl.BlockSpec((3, 2, NA_Q, NA_K), lambda p, b: (0, p, 0, 0)),
        ],
        out_specs=pl.BlockSpec((None, T, LANES), lambda p, b: (b, 0, p)),
        out_shape=jax.ShapeDtypeStruct((B, T, D_MODEL), BF16),
        compiler_params=_cparams(("parallel", "parallel")),
        name="na_attention",
    )(qkv, qkv, qkv, bias)


def _na_bias_table(rpb):
    H = rpb.shape[0]
    r = rpb.astype(F32)
    pad = GRID_W
    rp = jnp.concatenate([jnp.repeat(r[..., :1], pad, -1), r, jnp.repeat(r[..., -1:], pad, -1)], -1)
    base = NA_KW - 1 + pad
    t1 = jnp.stack([rp[..., base - qc: base - qc + GRID_W] for qc in range(GRID_W)], axis=2)
    qc = np.arange(GRID_W)[:, None]
    kc = np.arange(GRID_W)[None, :]
    ws = np.clip(qc - NA_KW // 2, 0, GRID_W - NA_KW)
    t1 = jnp.where(jnp.asarray((kc >= ws) & (kc < ws + NA_KW)), t1, NEG_INF)
    n_dr = 2 * NA_KH - 1
    t1 = jnp.concatenate([t1, jnp.full((H, 1, GRID_W, GRID_W), NEG_INF, F32)], axis=1)
    pick = np.full((3, NA_ROWS_PER_GROUP, NA_WIN_ROWS), n_dr, np.int32)
    for p in range(3):
        for i in range(NA_ROWS_PER_GROUP):
            lo = (0, i, NA_ROWS_PER_GROUP - 1)[p]
            qrow = (i, i + NA_KH // 2, i + NA_KH - 1)[p]
            for u in range(lo, lo + NA_KH):
                pick[p, i, u] = u - qrow + NA_KH - 1
    table = jnp.take(t1, jnp.asarray(pick.reshape(-1)), axis=1)
    table = table.reshape(H, 3, NA_ROWS_PER_GROUP, NA_WIN_ROWS, GRID_W, GRID_W)
    return table.transpose(1, 0, 2, 4, 3, 5).reshape(3, H, NA_Q, NA_K)


def _select_kernel(aff_ref, pos_ref, pre_ref, *, cap, n_tok, rb):
    E = N_EXPERTS
    per_row = LANES // E
    rows = n_tok // per_row
    bits = pltpu.bitcast(aff_ref[...], I32)
    lane = lax.broadcasted_iota(I32, (rows, LANES), 1)
    tok = lax.broadcasted_iota(I32, (rows, LANES), 0) * per_row + lane // E

    def count(pred):
        c = jnp.sum(pred.astype(I32), axis=0, keepdims=True)
        c = jnp.broadcast_to(c, (8, LANES))
        sh = E
        while sh < LANES:
            c = c + pltpu.roll(c, sh, 1)
            sh *= 2
        return c[0:1]

    def thr_step(i, thr):
        cand = thr | jnp.left_shift(jnp.int32(1), 30 - i)
        return jnp.where(count(bits >= cand) >= cap, cand, thr)

    thr = lax.fori_loop(0, 31, thr_step, jnp.zeros((1, LANES), I32))
    gt = bits > thr
    eq = bits == thr
    need = cap - count(gt)

    nbits = max(1, int(math.ceil(math.log2(n_tok))))

    def tie_step(i, jmax):
        cand = jmax + jnp.left_shift(jnp.int32(1), nbits - 1 - i)
        return jnp.where(count(jnp.logical_and(eq, tok < cand)) < need, cand, jmax)

    jmax = lax.fori_loop(0, nbits, tie_step, jnp.zeros((1, LANES), I32))

    li = lax.broadcasted_iota(I32, (LANES, LANES), 0)
    lj = lax.broadcasted_iota(I32, (LANES, LANES), 1)
    same_e = (li % E) == (lj % E)
    a_before = _ones_where(jnp.logical_and(same_e, li // E < lj // E))
    a_all = _ones_where(same_e)
    ri = lax.broadcasted_iota(I32, (rb, rb), 0)
    rj = lax.broadcasted_iota(I32, (rb, rb), 1)
    lower = _ones_where(rj < ri)

    running = jnp.zeros((1, LANES), F32)
    for blk in range(rows // rb):
        sl = slice(blk * rb, (blk + 1) * rb)
        b_blk = pltpu.bitcast(aff_ref[sl, :], I32)
        t_blk = ((lax.broadcasted_iota(I32, (rb, LANES), 0) + blk * rb) * per_row
                 + lax.broadcasted_iota(I32, (rb, LANES), 1) // E)
        keep = jnp.logical_or(b_blk > thr, jnp.logical_and(b_blk == thr, t_blk <= jmax))
        kb = _ones_where(keep)
        row_tot = _dot(kb, a_all)
        within = _dot(lower, row_tot.astype(BF16))
        pre = running + within + _dot(kb, a_before)
        running = running + jnp.sum(row_tot, axis=0, keepdims=True)
        pre_i = pre.astype(I32)
        pre_ref[sl, :] = pre_i
        pos_ref[sl, :] = jnp.where(keep, pre_i, -1)


def _select(aff, cap):
    n_tok, E = aff.shape
    rows = n_tok * E // LANES
    rb = min(256, rows)
    aff8 = aff.reshape(rows, LANES)
    pos, pre = pl.pallas_call(
        functools.partial(_select_kernel, cap=cap, n_tok=n_tok, rb=rb),
        out_shape=[jax.ShapeDtypeStruct((rows, LANES), I32)] * 2,
        compiler_params=pltpu.CompilerParams(vmem_limit_bytes=VMEM_LIMIT),
        name="expert_select",
    )(aff8)
    return pos.reshape(n_tok, E), pre.reshape(n_tok, E)


DISPATCH_GROUP = 4


def _dispatch_slab(tn):
    return tn // 4 + BF16_SUBLANES


def _dispatch_rounds(tn):
    return -(-(tn + BF16_SUBLANES) // _dispatch_slab(tn))


def _dispatch_kernel(ws_ref, need_ref, pos_ref, aff_ref, h_ref, xe_ref, gs_ref, *, n_tiles, sub, batch):
    eg = pl.program_id(0)
    tb = pl.program_id(1)
    group, _, tn = pos_ref.shape
    head = BF16_SUBLANES
    slab = _dispatch_slab(tn)
    n_rounds = _dispatch_rounds(tn)

    @pl.when(tb == 0)
    def _():
        xe_ref[...] = jnp.zeros_like(xe_ref)
        gs_ref[...] = jnp.zeros_like(gs_ref)

    def one_round(s, r):
        t = tb * sub + s
        hs = h_ref[pl.ds(pl.multiple_of(s * tn, tn), tn), :]
        row = lax.broadcasted_iota(I32, (slab, tn), 0) + r * slab
        ws = [pl.multiple_of(ws_ref[(eg * group + k) * n_tiles + t], head) for k in range(group)]
        hits = [row == pos_ref[k, pl.ds(s, 1), :] - ws[k] for k in range(group)]
        rows = _dot(jnp.concatenate([_ones_where(hk) for hk in hits], axis=0), hs)
        for k in range(group):
            rk = rows[k * slab:(k + 1) * slab]
            gk = jnp.sum(jnp.where(hits[k], aff_ref[k, pl.ds(s, 1), :], 0.0), axis=1, keepdims=True)
            base = pl.multiple_of(ws[k] + r * slab, head)
            acc_rows = head if r == 0 else slab
            xe_ref[k, pl.ds(base, acc_rows), :] = (
                xe_ref[k, pl.ds(base, acc_rows), :].astype(F32) + rk[:acc_rows]).astype(xe_ref.dtype)
            gs_ref[k, pl.ds(base, acc_rows), :] = gs_ref[k, pl.ds(base, acc_rows), :] + gk[:acc_rows]
            if r == 0:
                rest = pl.multiple_of(base + head, head)
                xe_ref[k, pl.ds(rest, slab - head), :] = rk[head:].astype(xe_ref.dtype)
                gs_ref[k, pl.ds(rest, slab - head), :] = gk[head:]

    def tile_batch(sb, carry):
        for u in range(batch):
            one_round(sb * batch + u, 0)
        for u in range(batch):
            s = sb * batch + u
            need = need_ref[eg * n_tiles + tb * sub + s]
            for r in range(1, n_rounds):
                pl.when(need > r * slab)(functools.partial(one_round, s, r))
        return carry

    lax.fori_loop(0, sub // batch, tile_batch, 0)


def _dispatch(wstart, need, pos_t, aff_t, h, cap, tn):
    E, n_tiles, _ = pos_t.shape
    N, D = h.shape
    rows = cap + _dispatch_rounds(tn) * _dispatch_slab(tn)
    sub = 8 if n_tiles % 8 == 0 else n_tiles
    group = DISPATCH_GROUP
    resident = dict(pipeline_mode=pl.Buffered(1))
    grid_spec = pltpu.PrefetchScalarGridSpec(
        num_scalar_prefetch=2,
        grid=(E // group, n_tiles // sub),
        in_specs=[
            pl.BlockSpec((group, sub, tn), lambda e, t, ws, nd: (e, t, 0)),
            pl.BlockSpec((group, sub, tn), lambda e, t, ws, nd: (e, t, 0)),
            pl.BlockSpec((sub * tn, D), lambda e, t, ws, nd: (t, 0)),
        ],
        out_specs=[
            pl.BlockSpec((group, rows, D), lambda e, t, ws, nd: (e, 0, 0), **resident),
            pl.BlockSpec((group, rows, 1), lambda e, t, ws, nd: (e, 0, 0), **resident),
        ],
    )
    return pl.pallas_call(
        functools.partial(_dispatch_kernel, n_tiles=n_tiles, sub=sub, batch=8 if sub % 8 == 0 else 1),
        grid_spec=grid_spec,
        out_shape=[jax.ShapeDtypeStruct((E, rows, D), BF16), jax.ShapeDtypeStruct((E, rows, 1), F32)],
        compiler_params=_cparams(("parallel", "arbitrary")),
        name="moe_dispatch",
    )(wstart, need, pos_t, aff_t, h)


def _ffn_kernel(x_ref, gs_ref, wg_ref, wu_ref, wd_ref, o_ref, acc, *, n_f):
    f = pl.program_id(2)

    @pl.when(f == 0)
    def _():
        acc[...] = jnp.zeros_like(acc)

    x = x_ref[...]
    hid = _silu(_dot(x, wg_ref[...].astype(BF16))) * _dot(x, wu_ref[...].astype(BF16))
    acc[...] += _dot(hid.astype(BF16), wd_ref[...].astype(BF16))

    @pl.when(f == n_f - 1)
    def _():
        o_ref[...] = (acc[...] * gs_ref[...]).astype(o_ref.dtype)


def _expert_ffn(xe, gs, wg, wu, wd, layer, C):
    E, _, D = xe.shape
    F = wg.shape[-1]
    tm = _tile(C, 2048)
    tf = _tile(F, 512)
    n_f = F // tf
    return pl.pallas_call(
        functools.partial(_ffn_kernel, n_f=n_f),
        grid=(E, C // tm, n_f),
        in_specs=[
            pl.BlockSpec((None, tm, D), lambda e, m, f: (e, m, 0)),
            pl.BlockSpec((None, tm, 1), lambda e, m, f: (e, m, 0)),
            pl.BlockSpec((None, None, D, tf), lambda e, m, f: (layer, e, 0, f)),
            pl.BlockSpec((None, None, D, tf), lambda e, m, f: (layer, e, 0, f)),
            pl.BlockSpec((None, None, tf, D), lambda e, m, f: (layer, e, f, 0)),
        ],
        out_specs=pl.BlockSpec((None, tm, D), lambda e, m, f: (e, m, 0)),
        out_shape=jax.ShapeDtypeStruct((E, C, D), BF16),
        scratch_shapes=[pltpu.VMEM((tm, D), F32)],
        compiler_params=_cparams(("parallel", "parallel", "arbitrary")),
        name="expert_ffn",
    )(xe, gs, wg, wu, wd)


COMBINE_SLAB = LANES
COMBINE_PARTS = 4
COMBINE_TILES = 2


def _combine_kernel(ws_ref, end_ref, pos_ref, ye_any, x_ref, g_ref, o_ref, *scratch,
                    n_tiles, n_steps, n_exp, tn, win, slab):
    hit_scrs = scratch[:-3]
    y_buf, late_scr, sem = scratch[-3:]
    per = n_exp // len(hit_scrs)
    tps = pos_ref.shape[0] // tn
    step = pl.program_id(0)
    slot = step % 2

    def window_copy(tile, buf, u, e):
        ws = pl.multiple_of(ws_ref[e * n_tiles + tile], BF16_SUBLANES)
        return pltpu.make_async_copy(ye_any.at[e, pl.ds(ws, slab), :],
                                     y_buf.at[buf, pl.ds((u * n_exp + e) * slab, slab), :], sem.at[buf])

    def fetch(s, buf):
        for u in range(tps):
            for e in range(n_exp):
                window_copy(s * tps + u, buf, u, e).start()

    @pl.when(step == 0)
    def _():
        fetch(0, 0)

    @pl.when(step + 1 < n_steps)
    def _():
        fetch(step + 1, 1 - slot)

    pltpu.make_async_copy(y_buf.at[1 - slot], y_buf.at[slot], sem.at[slot]).wait()

    lane = lax.broadcasted_iota(I32, (tn, slab), 1)
    for u in range(tps):
        t = step * tps + u
        rows = pl.ds(u * tn, tn)
        acc = None
        for gi, hit_scr in enumerate(hit_scrs):
            for k in range(per):
                e = gi * per + k
                prel = pos_ref[rows, e:e + 1] - ws_ref[e * n_tiles + t]
                hit_scr[:, k * slab:(k + 1) * slab] = _ones_where(lane == prel)
            base = (u * n_exp + gi * per) * slab
            part = _dot(hit_scr[...], y_buf[slot, base:base + per * slab, :])
            acc = part if acc is None else acc + part
        o_ref[rows, :] = x_ref[rows, :] + g_ref[...] * acc

        if win > slab:
            for e in range(n_exp):
                ws = ws_ref[e * n_tiles + t]

                @pl.when(end_ref[e * n_tiles + t] - ws > slab)
                def _():
                    late = pl.multiple_of(ws + slab, BF16_SUBLANES)
                    pltpu.sync_copy(ye_any.at[e, pl.ds(late, win - slab), :], late_scr)
                    prel = pos_ref[rows, e:e + 1] - ws - slab
                    hit = _ones_where(lax.broadcasted_iota(I32, (tn, win - slab), 1) == prel)
                    o_ref[rows, :] += g_ref[...] * _dot(hit, late_scr[...])


def _combine(wstart, wend, pos, ye, x, gate, tn):
    E, C, D = ye.shape
    B, T, _ = x.shape
    N = B * T
    n_tiles = N // tn
    win = tn + BF16_SUBLANES
    slab = min(COMBINE_SLAB, win)
    tps = COMBINE_TILES if (T // tn) % COMBINE_TILES == 0 else 1
    n_steps = n_tiles // tps
    spb = T // (tn * tps)
    grid_spec = pltpu.PrefetchScalarGridSpec(
        num_scalar_prefetch=2,
        grid=(n_steps,),
        in_specs=[pl.BlockSpec((tps * tn, E), lambda s, ws, we: (s, 0)),
                  pl.BlockSpec(memory_space=pl.ANY),
                  pl.BlockSpec((None, tps * tn, D), lambda s, ws, we: (s // spb, s % spb, 0)),
                  pl.BlockSpec((None, 1, D), lambda s, ws, we: (s // spb, 0, 0))],
        out_specs=pl.BlockSpec((None, tps * tn, D), lambda s, ws, we: (s // spb, s % spb, 0)),
        scratch_shapes=[pltpu.VMEM((tn, E // COMBINE_PARTS * slab), BF16)] * COMBINE_PARTS
        + [pltpu.VMEM((2, tps * E * slab, D), BF16),
           pltpu.VMEM((max(win - slab, BF16_SUBLANES), D), BF16),
           pltpu.SemaphoreType.DMA((2,))],
    )
    return pl.pallas_call(
        functools.partial(_combine_kernel, n_tiles=n_tiles, n_steps=n_steps, n_exp=E, tn=tn, win=win, slab=slab),
        grid_spec=grid_spec,
        out_shape=jax.ShapeDtypeStruct((B, T, D), F32),
        compiler_params=_cparams(("arbitrary",)),
        name="moe_combine",
    )(wstart, wend, pos, ye, x, gate)


def _moe_tile(n_tok, cap):
    tn = 256
    while tn + BF16_SUBLANES > cap or n_tok % tn:
        tn //= 2
    return tn


def _moe_layer(x, h, aff, gate, wg, wu, wd, layer):
    B, T, D = x.shape
    N = B * T
    E = aff.shape[1]
    cap = EC_CAPACITY_FACTOR * N // E
    tn = _moe_tile(T, cap)
    win = tn + BF16_SUBLANES
    pos, pre = _select(aff, cap)
    start = pre[::tn].T
    aligned = (start // BF16_SUBLANES * BF16_SUBLANES).astype(I32)
    wdisp = aligned.reshape(-1)
    wstart = jnp.minimum(aligned, cap - win).reshape(-1)
    end = jnp.concatenate([start[:, 1:], jnp.full((E, 1), cap, I32)], axis=1).astype(I32)
    wend = end.reshape(-1)
    need = (end - aligned).reshape(E // DISPATCH_GROUP, DISPATCH_GROUP, -1).max(axis=1).reshape(-1)
    pos_t = pos.T.reshape(E, N // tn, tn)
    aff_t = aff.T.reshape(E, N // tn, tn)
    xe, gs = _dispatch(wdisp, need, pos_t, aff_t, h, cap, tn)
    ye = _expert_ffn(xe, gs, wg, wu, wd, layer, cap)
    return _combine(wstart, wend, pos, ye, x, gate, tn)


def _trunk(x, mod, p):
    B, T, D = x.shape
    rc = min(RET_BLOCK, T)
    cos, sin = _rotary_tables(T)
    for i in range(DEPTH):
        sh1, sc1, g1, sh2, sc2, g2 = [mod[i, :, k][:, None, :] for k in range(6)]
        j = i // 2
        if i % 2 == 0:
            proj = _proj(x, p["norm_mix_g"][i][None], sh1, sc1, p["ret_w_in"][j], rotary=(cos, sin))
            tabs = _retention_tables(p["ret_decay_logit"][j], rc)
            y_f = _retention_dir(proj, tabs, 0, None)
            y = _retention_dir(proj, tabs, 1, y_f)
            w_out = p["ret_w_out"][j]
        else:
            qkv = _proj(x, p["norm_mix_g"][i][None], sh1, sc1, p["na_w_in"][j],
                        qk=(p["na_bd"], p["na_gain"][j], 2 * D))
            y = _na_attention(qkv, p["na_bias"][j])
            w_out = p["na_w_out"][j]
        x, h, aff = _out_proj(y, w_out, x, g1, p["norm_ffn_g"][i][None], sh2, sc2, p["moe_router"][i])
        x = _moe_layer(x, h, aff, g2, p["moe_w_gate"], p["moe_w_up"], p["moe_w_down"], i)
    return x


def kernel(x_prompt, x_sample, c_prompt, c_sample, norm_mix_g, norm_ffn_g, ada_w, ada_b, ret_w_in, ret_decay_logit, ret_w_out, na_w_in, na_q_gain, na_k_gain, na_rpb, na_w_out, moe_router, moe_w_gate, moe_w_up, moe_w_down):
    D = D_MODEL
    bp, bs = c_prompt.shape[0], c_sample.shape[0]
    c_all = jnp.concatenate([c_prompt, c_sample], axis=0)
    pad = (-c_all.shape[0]) % 8
    if pad:
        c_all = jnp.pad(c_all, ((0, pad), (0, 0)))
    mod = _ada_mod(c_all, ada_w, ada_b).reshape(DEPTH, c_all.shape[0], 6, D)

    heads_row = lambda v: jnp.tile(v.astype(F32), (1, NA_HEADS))
    na_gain = jnp.concatenate(
        [heads_row(na_q_gain) * (NA_HEAD_DIM ** -0.5), heads_row(na_k_gain),
         jnp.ones((na_q_gain.shape[0], D), F32)], axis=-1)[:, None, :]
    bd = np.kron(np.eye(256 // NA_HEAD_DIM), np.ones((NA_HEAD_DIM, NA_HEAD_DIM)))
    p = dict(
        norm_mix_g=norm_mix_g, norm_ffn_g=norm_ffn_g,
        ret_w_in=ret_w_in.astype(BF16), ret_decay_logit=ret_decay_logit, ret_w_out=ret_w_out.astype(BF16),
        na_w_in=na_w_in.astype(BF16), na_w_out=na_w_out.astype(BF16),
        na_gain=na_gain, na_bd=jnp.asarray(bd, BF16),
        na_bias=jnp.stack([_na_bias_table(na_rpb[l]) for l in range(na_rpb.shape[0])]),
        moe_router=moe_router,
        moe_w_gate=moe_w_gate, moe_w_up=moe_w_up, moe_w_down=moe_w_down,
    )
    y_prompt = _trunk(x_prompt, mod[:, :bp], p)
    y_sample = _trunk(x_sample, mod[:, bp:bp + bs], p)
    return (y_prompt, y_sample)
```

```python
import functools
import math

import numpy as np
import jax
import jax.numpy as jnp
from jax import lax
from jax.experimental import pallas as pl
from jax.experimental.pallas import tpu as pltpu

F32 = jnp.float32
BF16 = jnp.bfloat16
I32 = jnp.int32

D_MODEL = 1024
DEPTH = 4

RET_HEADS = 4
RET_QK_DIM = 256
RET_V_DIM = 512
RET_QK_WIDTH = RET_HEADS * RET_QK_DIM
RET_V_WIDTH = RET_HEADS * RET_V_DIM
RET_IN_COLS = 2 * RET_QK_WIDTH + 3 * RET_V_WIDTH
ROPE_BASE = 10000.0
RET_BLOCK = 256

NA_HEADS = 16
NA_HEAD_DIM = 64
NA_KH = 8
NA_KW = 16
GRID_W = 64
NA_ROWS_PER_GROUP = 4
NA_WIN_ROWS = NA_ROWS_PER_GROUP + NA_KH - 1
NA_Q = NA_ROWS_PER_GROUP * GRID_W
NA_K = NA_WIN_ROWS * GRID_W

N_EXPERTS = 16
EC_CAPACITY_FACTOR = 2

EPS = 1e-6
NEG_INF = -1e30

LANES = 128
SUBLANES = 8
BF16_SUBLANES = 16
MXU_DIM = 256
VMEM_LIMIT = 56 * 1024 * 1024

HIGHEST = lax.Precision.HIGHEST


def _cparams(sem):
    return pltpu.CompilerParams(dimension_semantics=sem, vmem_limit_bytes=VMEM_LIMIT)


def _tile(n, pref):
    t = min(pref, n)
    while n % t:
        t //= 2
    return t


def _silu(x):
    return (0.5 * x) * (1.0 + jnp.tanh(0.5 * x))


def _ones_where(mask):
    return jnp.where(mask, 1.0, 0.0).astype(BF16)


def _dot(a, b):
    return jnp.dot(a, b, preferred_element_type=F32)


def _dot_nt(a, b):
    return lax.dot_general(a, b, (((1,), (1,)), ((), ())), preferred_element_type=F32)


def _dot_tn(a, b):
    return lax.dot_general(a, b, (((0,), (0,)), ((), ())), preferred_element_type=F32)


def _ada_kernel(c_ref, w_ref, b_ref, o_ref):
    ca = _silu(c_ref[...])
    o_ref[...] = jnp.dot(ca, w_ref[...], preferred_element_type=F32, precision=HIGHEST) + b_ref[...]


def _ada_mod(c_all, ada_w, ada_b):
    R, D = c_all.shape
    n_out = ada_w.shape[-1]
    tn = min(1536, n_out)
    return pl.pallas_call(
        _ada_kernel,
        grid=(DEPTH, n_out // tn),
        in_specs=[
            pl.BlockSpec((R, D), lambda l, j: (0, 0)),
            pl.BlockSpec((None, D, tn), lambda l, j: (l, 0, j)),
            pl.BlockSpec((None, 1, tn), lambda l, j: (l, 0, j)),
        ],
        out_specs=pl.BlockSpec((None, R, tn), lambda l, j: (l, 0, j)),
        out_shape=jax.ShapeDtypeStruct((DEPTH, R, n_out), F32),
        compiler_params=_cparams(("arbitrary", "arbitrary")),
        name="ada_mod",
    )(c_all, ada_w, ada_b.reshape(DEPTH, 1, n_out))


def _norm_mod(x, g, sh, sc):
    ms = jnp.mean(x * x, axis=-1, keepdims=True)
    y = x * lax.rsqrt(ms + EPS) * g
    return y * (1.0 + sc) + sh


def _proj_block(x_ref, g_ref, sh_ref, sc_ref, w_ref, h_scr):
    @pl.when(pl.program_id(2) == 0)
    def _():
        h_scr[...] = _norm_mod(x_ref[...], g_ref[...], sh_ref[...], sc_ref[...]).astype(BF16)

    return _dot(h_scr[...], w_ref[...])


def _proj_qknorm_kernel(x_ref, g_ref, sh_ref, sc_ref, w_ref, bd_ref, gain_ref, o_ref, h_scr, *, qk_blocks, tn):
    j = pl.program_id(2)
    acc = _proj_block(x_ref, g_ref, sh_ref, sc_ref, w_ref, h_scr)

    @pl.when(j < qk_blocks)
    def _():
        for c in range(tn // MXU_DIM):
            cols = slice(c * MXU_DIM, (c + 1) * MXU_DIM)
            a = acc[:, cols]
            ss = _dot((a * a).astype(BF16), bd_ref[...])
            r = lax.rsqrt(ss * (1.0 / NA_HEAD_DIM) + EPS)
            o_ref[:, cols] = (a * r * gain_ref[:, cols]).astype(o_ref.dtype)

    @pl.when(j >= qk_blocks)
    def _():
        o_ref[...] = acc.astype(o_ref.dtype)


def _proj_rotary_kernel(x_ref, g_ref, sh_ref, sc_ref, w_ref, cos_ref, sin_ref, o_ref, h_scr, *, tn):
    j = pl.program_id(2)
    acc = _proj_block(x_ref, g_ref, sh_ref, sc_ref, w_ref, h_scr)
    qk_blocks = 2 * RET_QK_WIDTH // tn
    half = RET_QK_DIM // 2

    @pl.when(j < qk_blocks)
    def _():
        scale = jnp.where(j < qk_blocks // 2, 1.0, RET_QK_DIM ** -0.5)
        cos = cos_ref[...] * scale
        sin = sin_ref[...] * scale
        for c in range(tn // RET_QK_DIM):
            a1 = acc[:, c * RET_QK_DIM:c * RET_QK_DIM + half]
            a2 = acc[:, c * RET_QK_DIM + half:(c + 1) * RET_QK_DIM]
            o_ref[:, c * RET_QK_DIM:c * RET_QK_DIM + half] = (a1 * cos - a2 * sin).astype(o_ref.dtype)
            o_ref[:, c * RET_QK_DIM + half:(c + 1) * RET_QK_DIM] = (a1 * sin + a2 * cos).astype(o_ref.dtype)

    @pl.when(j >= qk_blocks)
    def _():
        o_ref[...] = acc.astype(o_ref.dtype)


def _proj(x, g, sh, sc, w, qk=None, rotary=None):
    assert (qk is None) != (rotary is None)
    B, T, D = x.shape
    n_out = w.shape[1]
    tm = _tile(T, 2048)
    tn = _tile(n_out, 1024)
    grid = (B, T // tm, n_out // tn)
    in_specs = [
        pl.BlockSpec((None, tm, D), lambda b, i, j: (b, i, 0)),
        pl.BlockSpec((1, D), lambda b, i, j: (0, 0)),
        pl.BlockSpec((None, 1, D), lambda b, i, j: (b, 0, 0)),
        pl.BlockSpec((None, 1, D), lambda b, i, j: (b, 0, 0)),
        pl.BlockSpec((D, tn), lambda b, i, j: (0, j)),
    ]
    args = [x, g, sh, sc, w]
    if rotary is not None:
        half = RET_QK_DIM // 2
        in_specs += [pl.BlockSpec((tm, half), lambda b, i, j: (i, 0))] * 2
        args += list(rotary)
        body = functools.partial(_proj_rotary_kernel, tn=tn)
        name = "proj_rotary"
    else:
        bd, gain, n_qk = qk
        in_specs += [
            pl.BlockSpec((MXU_DIM, MXU_DIM), lambda b, i, j: (0, 0)),
            pl.BlockSpec((1, tn), lambda b, i, j: (0, j)),
        ]
        args += [bd, gain]
        body = functools.partial(_proj_qknorm_kernel, qk_blocks=n_qk // tn, tn=tn)
        name = "proj_qknorm"
    return pl.pallas_call(
        body,
        grid=grid,
        in_specs=in_specs,
        out_specs=pl.BlockSpec((None, tm, tn), lambda b, i, j: (b, i, j)),
        out_shape=jax.ShapeDtypeStruct((B, T, n_out), BF16),
        scratch_shapes=[pltpu.VMEM((tm, D), BF16)],
        compiler_params=_cparams(("parallel", "parallel", "arbitrary")),
        name=name,
    )(*args)


def _outproj_kernel(y_ref, w_ref, x_ref, g_ref, o_ref):
    o_ref[...] = x_ref[...] + g_ref[...] * _dot(y_ref[...], w_ref[...])


def _ffn_pre_kernel(x_ref, g_ref, sh_ref, sc_ref, r_ref, h_ref, aff_ref):
    h = _norm_mod(x_ref[...], g_ref[...], sh_ref[...], sc_ref[...])
    h_hi = h.astype(BF16)
    h_ref[...] = h_hi
    h_lo = (h - h_hi.astype(F32)).astype(BF16)
    r = r_ref[...]
    r_hi = r.astype(BF16)
    r_lo = (r - r_hi.astype(F32)).astype(BF16)
    logits = _dot(h_hi, r_hi) + (_dot(h_hi, r_lo) + _dot(h_lo, r_hi))
    m = jnp.max(logits, axis=-1, keepdims=True)
    e = jnp.exp(logits - m)
    aff_ref[...] = e / jnp.sum(e, axis=-1, keepdims=True)


def _out_proj(y, w, x, gate, ng, sh, sc, router):
    B, T, K = y.shape
    D = w.shape[1]
    E = router.shape[1]
    tm = _tile(T, 1024)
    xn = pl.pallas_call(
        _outproj_kernel,
        grid=(B, T // tm),
        in_specs=[
            pl.BlockSpec((None, tm, K), lambda b, i: (b, i, 0)),
            pl.BlockSpec((K, D), lambda b, i: (0, 0)),
            pl.BlockSpec((None, tm, D), lambda b, i: (b, i, 0)),
            pl.BlockSpec((None, 1, D), lambda b, i: (b, 0, 0)),
        ],
        out_specs=pl.BlockSpec((None, tm, D), lambda b, i: (b, i, 0)),
        out_shape=jax.ShapeDtypeStruct((B, T, D), F32),
        compiler_params=_cparams(("parallel", "parallel")),
        name="out_proj",
    )(y, w, x, gate)
    tp = _tile(T, 1024)
    nt = T // tp
    h, aff = pl.pallas_call(
        _ffn_pre_kernel,
        grid=(B, nt),
        in_specs=[
            pl.BlockSpec((None, tp, D), lambda b, i: (b, i, 0)),
            pl.BlockSpec((1, D), lambda b, i: (0, 0)),
            pl.BlockSpec((None, 1, D), lambda b, i: (b, 0, 0)),
            pl.BlockSpec((None, 1, D), lambda b, i: (b, 0, 0)),
            pl.BlockSpec((D, E), lambda b, i: (0, 0)),
        ],
        out_specs=[
            pl.BlockSpec((tp, D), lambda b, i: (b * nt + i, 0)),
            pl.BlockSpec((tp, E), lambda b, i: (b * nt + i, 0)),
        ],
        out_shape=[jax.ShapeDtypeStruct((B * T, D), BF16), jax.ShapeDtypeStruct((B * T, E), F32)],
        compiler_params=_cparams(("parallel", "parallel")),
        name="ffn_pre",
    )(xn, ng, sh, sc, router)
    return xn, h, aff


def _retention_kernel(*refs, has_prev):
    if has_prev:
        q_ref, k_ref, v_ref, gate_ref, intra_ref, qdec_ref, kdec_ref, cdec_ref, prev_ref, o_ref, state = refs
    else:
        q_ref, k_ref, v_ref, gate_ref, intra_ref, qdec_ref, kdec_ref, cdec_ref, o_ref, state = refs
        prev_ref = None

    @pl.when(pl.program_id(1) == 0)
    def _():
        state[...] = jnp.zeros_like(state)

    for h in range(RET_HEADS):
        qk = slice(h * RET_QK_DIM, (h + 1) * RET_QK_DIM)
        vs = slice(h * RET_V_DIM, (h + 1) * RET_V_DIM)
        qb = q_ref[:, qk]
        kb = k_ref[:, qk]
        v = v_ref[:, vs]
        scores = _dot_nt(qb, kb) * intra_ref[h]
        st = state[h]
        o = _dot(scores.astype(BF16), v) + _dot(qb, st.astype(BF16)) * qdec_ref[h]
        state[h] = st * cdec_ref[h] + _dot_tn((kb.astype(F32) * kdec_ref[h]).astype(BF16), v)

        mu = jnp.mean(o, axis=-1, keepdims=True)
        oc = o - mu
        var = jnp.mean(oc * oc, axis=-1, keepdims=True)
        y = oc * lax.rsqrt(var + EPS) * _silu(gate_ref[:, vs].astype(F32))
        if has_prev:
            y = y + prev_ref[:, vs].astype(F32)
        o_ref[:, vs] = y.astype(o_ref.dtype)


def _retention_dir(proj, tabs, direction, prev):
    B, T, _ = proj.shape
    rc = min(RET_BLOCK, T)
    nc = T // rc
    intra, qdec, kdec, cdec = tabs
    if direction == 0:
        cidx = lambda c: c
    else:
        cidx = lambda c: nc - 1 - c
    H = RET_HEADS
    v_blk = 2 * RET_QK_WIDTH // RET_V_WIDTH
    in_specs = [
        pl.BlockSpec((None, rc, RET_QK_WIDTH), lambda b, c: (b, cidx(c), 0)),
        pl.BlockSpec((None, rc, RET_QK_WIDTH), lambda b, c: (b, cidx(c), 1)),
        pl.BlockSpec((None, rc, RET_V_WIDTH), lambda b, c: (b, cidx(c), v_blk)),
        pl.BlockSpec((None, rc, RET_V_WIDTH), lambda b, c: (b, cidx(c), v_blk + 1 + direction)),
        pl.BlockSpec((H, rc, rc), lambda b, c: (direction, 0, 0)),
        pl.BlockSpec((H, rc, 1), lambda b, c: (direction, 0, 0)),
        pl.BlockSpec((H, rc, 1), lambda b, c: (direction, 0, 0)),
        pl.BlockSpec((H, 1, 1), lambda b, c: (direction, 0, 0)),
    ]
    args = [proj, proj, proj, proj, intra, qdec, kdec, cdec]
    if prev is not None:
        in_specs.append(pl.BlockSpec((None, rc, RET_V_WIDTH), lambda b, c: (b, cidx(c), 0)))
        args.append(prev)
    return pl.pallas_call(
        functools.partial(_retention_kernel, has_prev=prev is not None),
        grid=(B, nc),
        in_specs=in_specs,
        out_specs=pl.BlockSpec((None, rc, RET_V_WIDTH), lambda b, c: (b, cidx(c), 0)),
        out_shape=jax.ShapeDtypeStruct((B, T, RET_V_WIDTH), BF16),
        scratch_shapes=[pltpu.VMEM((H, RET_QK_DIM, RET_V_DIM), F32)],
        compiler_params=_cparams(("parallel", "arbitrary")),
        name="retention_bwd" if direction else "retention_fwd",
    )(*args)


def _retention_tables(decay_logit, rc):
    lg = jax.nn.log_sigmoid(decay_logit.astype(F32).reshape(-1))
    pos = jnp.arange(rc, dtype=F32)
    rel = pos[:, None] - pos[None, :]
    l3 = lg[:, None, None]
    fwd = jnp.where(rel >= 0, jnp.exp(jnp.maximum(rel, 0.0) * l3), 0.0)
    bwd = jnp.where(rel <= 0, jnp.exp(jnp.maximum(-rel, 0.0) * l3), 0.0)
    is_bwd = (jnp.arange(2 * RET_HEADS) >= RET_HEADS)
    intra = jnp.where(is_bwd[:, None, None], bwd, fwd)
    q_pow = jnp.where(is_bwd[:, None], rc - pos[None, :], pos[None, :] + 1.0)
    k_pow = jnp.where(is_bwd[:, None], pos[None, :], rc - 1.0 - pos[None, :])
    qdec = jnp.exp(q_pow * lg[:, None])[..., None]
    kdec = jnp.exp(k_pow * lg[:, None])[..., None]
    cdec = jnp.exp(rc * lg)[:, None, None]
    return intra, qdec, kdec, cdec


def _rotary_tables(T):
    d = RET_QK_DIM
    inv = 1.0 / (ROPE_BASE ** (jnp.arange(0, d, 2, dtype=F32) / d))
    ang = jnp.arange(T, dtype=F32)[:, None] * inv[None, :]
    return jnp.cos(ang), jnp.sin(ang)


def _na_kernel(q_ref, k_ref, v_ref, bias_ref, o_ref, *, rows):
    n_groups = rows // NA_ROWS_PER_GROUP
    lane = lax.broadcasted_iota(I32, (NA_Q, LANES), 1)
    first = lane < NA_HEAD_DIM

    def group(gi, carry):
        ustart = jnp.clip(gi * NA_ROWS_PER_GROUP - NA_KH // 2, 0, rows - NA_WIN_ROWS)
        pat = jnp.where(gi == 0, 0, jnp.where(gi == n_groups - 1, 2, 1))
        kstart = pl.multiple_of(ustart * GRID_W, GRID_W)
        qstart = pl.multiple_of(gi * NA_Q, NA_Q)
        kw = k_ref[pl.ds(kstart, NA_K), :]
        vw = v_ref[pl.ds(kstart, NA_K), :]
        q = q_ref[pl.ds(qstart, NA_Q), :].astype(F32)
        q2 = jnp.concatenate([jnp.where(first, q, 0.0), jnp.where(first, 0.0, q)], axis=0).astype(BF16)
        s = _dot_nt(q2, kw) + bias_ref[pat].reshape(2 * NA_Q, NA_K)
        m = jnp.max(s, axis=-1, keepdims=True)
        e = jnp.exp(s - m)
        l = jnp.sum(e, axis=-1, keepdims=True)
        o2 = _dot(e.astype(BF16), vw) / l
        o_ref[pl.ds(qstart, NA_Q), :] = jnp.where(first, o2[:NA_Q], o2[NA_Q:]).astype(o_ref.dtype)
        return carry

    lax.fori_loop(0, n_groups, group, 0, unroll=8)


def _na_attention(qkv, bias):
    B, T, _ = qkv.shape
    rows = T // GRID_W
    assert rows % NA_ROWS_PER_GROUP == 0 and rows >= NA_WIN_ROWS + 1
    n_pairs = D_MODEL // LANES
    blk = lambda off: pl.BlockSpec((None, T, LANES), lambda p, b: (b, 0, off + p))
    return pl.pallas_call(
        functools.partial(_na_kernel, rows=rows),
        grid=(n_pairs, B),
        in_specs=[
            blk(0), blk(n_pairs), blk(2 * n_pairs),
            pl.BlockSpec((3, 2, NA_Q, NA_K), lambda p, b: (0, p, 0, 0)),
        ],
        out_specs=pl.BlockSpec((None, T, LANES), lambda p, b: (b, 0, p)),
        out_shape=jax.ShapeDtypeStruct((B, T, D_MODEL), BF16),
        compiler_params=_cparams(("parallel", "parallel")),
        name="na_attention",
    )(qkv, qkv, qkv, bias)


def _na_bias_table(rpb):
    H = rpb.shape[0]
    r = rpb.astype(F32)
    pad = GRID_W
    rp = jnp.concatenate([jnp.repeat(r[..., :1], pad, -1), r, jnp.repeat(r[..., -1:], pad, -1)], -1)
    base = NA_KW - 1 + pad
    t1 = jnp.stack([rp[..., base - qc: base - qc + GRID_W] for qc in range(GRID_W)], axis=2)
    qc = np.arange(GRID_W)[:, None]
    kc = np.arange(GRID_W)[None, :]
    ws = np.clip(qc - NA_KW // 2, 0, GRID_W - NA_KW)
    t1 = jnp.where(jnp.asarray((kc >= ws) & (kc < ws + NA_KW)), t1, NEG_INF)
    n_dr = 2 * NA_KH - 1
    t1 = jnp.concatenate([t1, jnp.full((H, 1, GRID_W, GRID_W), NEG_INF, F32)], axis=1)
    pick = np.full((3, NA_ROWS_PER_GROUP, NA_WIN_ROWS), n_dr, np.int32)
    for p in range(3):
        for i in range(NA_ROWS_PER_GROUP):
            lo = (0, i, NA_ROWS_PER_GROUP - 1)[p]
            qrow = (i, i + NA_KH // 2, i + NA_KH - 1)[p]
            for u in range(lo, lo + NA_KH):
                pick[p, i, u] = u - qrow + NA_KH - 1
    table = jnp.take(t1, jnp.asarray(pick.reshape(-1)), axis=1)
    table = table.reshape(H, 3, NA_ROWS_PER_GROUP, NA_WIN_ROWS, GRID_W, GRID_W)
    return table.transpose(1, 0, 2, 4, 3, 5).reshape(3, H, NA_Q, NA_K)


def _select_kernel(aff_ref, pos_ref, pre_ref, *, cap, n_tok, rb):
    E = N_EXPERTS
    per_row = LANES // E
    rows = n_tok // per_row
    bits = pltpu.bitcast(aff_ref[...], I32)
    lane = lax.broadcasted_iota(I32, (rows, LANES), 1)
    tok = lax.broadcasted_iota(I32, (rows, LANES), 0) * per_row + lane // E

    def count(pred):
        c = jnp.sum(pred.astype(I32), axis=0, keepdims=True)
        c = jnp.broadcast_to(c, (SUBLANES, LANES))
        sh = E
        while sh < LANES:
            c = c + pltpu.roll(c, sh, 1)
            sh *= 2
        return c[0:1]

    def thr_step(i, thr):
        cand = thr | jnp.left_shift(jnp.int32(1), 30 - i)
        return jnp.where(count(bits >= cand) >= cap, cand, thr)

    thr = lax.fori_loop(0, 31, thr_step, jnp.zeros((1, LANES), I32))
    gt = bits > thr
    eq = bits == thr
    need = cap - count(gt)

    nbits = max(1, int(math.ceil(math.log2(n_tok))))

    def tie_step(i, jmax):
        cand = jmax + jnp.left_shift(jnp.int32(1), nbits - 1 - i)
        return jnp.where(count(jnp.logical_and(eq, tok < cand)) < need, cand, jmax)

    jmax = lax.fori_loop(0, nbits, tie_step, jnp.zeros((1, LANES), I32))

    li = lax.broadcasted_iota(I32, (LANES, LANES), 0)
    lj = lax.broadcasted_iota(I32, (LANES, LANES), 1)
    same_e = (li % E) == (lj % E)
    a_before = _ones_where(jnp.logical_and(same_e, li // E < lj // E))
    a_all = _ones_where(same_e)
    ri = lax.broadcasted_iota(I32, (rb, rb), 0)
    rj = lax.broadcasted_iota(I32, (rb, rb), 1)
    lower = _ones_where(rj < ri)

    running = jnp.zeros((1, LANES), F32)
    for blk in range(rows // rb):
        sl = slice(blk * rb, (blk + 1) * rb)
        b_blk = pltpu.bitcast(aff_ref[sl, :], I32)
        t_blk = ((lax.broadcasted_iota(I32, (rb, LANES), 0) + blk * rb) * per_row
                 + lax.broadcasted_iota(I32, (rb, LANES), 1) // E)
        keep = jnp.logical_or(b_blk > thr, jnp.logical_and(b_blk == thr, t_blk <= jmax))
        kb = _ones_where(keep)
        row_tot = _dot(kb, a_all)
        within = _dot(lower, row_tot.astype(BF16))
        pre = running + within + _dot(kb, a_before)
        running = running + jnp.sum(row_tot, axis=0, keepdims=True)
        pre_i = pre.astype(I32)
        pre_ref[sl, :] = pre_i
        pos_ref[sl, :] = jnp.where(keep, pre_i, -1)


def _select(aff, cap):
    n_tok, E = aff.shape
    rows = n_tok * E // LANES
    rb = min(256, rows)
    aff8 = aff.reshape(rows, LANES)
    pos, pre = pl.pallas_call(
        functools.partial(_select_kernel, cap=cap, n_tok=n_tok, rb=rb),
        out_shape=[jax.ShapeDtypeStruct((rows, LANES), I32)] * 2,
        compiler_params=pltpu.CompilerParams(vmem_limit_bytes=VMEM_LIMIT),
        name="expert_select",
    )(aff8)
    return pos.reshape(n_tok, E), pre.reshape(n_tok, E)


DISPATCH_GROUP = 4


def _dispatch_slab(tn):
    return tn // 4 + BF16_SUBLANES


def _dispatch_rounds(tn):
    return -(-(tn + BF16_SUBLANES) // _dispatch_slab(tn))


def _dispatch_kernel(ws_ref, need_ref, pos_ref, aff_ref, h_ref, xe_ref, gs_ref, *, n_tiles, sub, batch):
    eg = pl.program_id(0)
    tb = pl.program_id(1)
    group, _, tn = pos_ref.shape
    head = BF16_SUBLANES
    slab = _dispatch_slab(tn)
    n_rounds = _dispatch_rounds(tn)

    @pl.when(tb == 0)
    def _():
        xe_ref[...] = jnp.zeros_like(xe_ref)
        gs_ref[...] = jnp.zeros_like(gs_ref)

    def one_round(s, r):
        t = tb * sub + s
        hs = h_ref[pl.ds(pl.multiple_of(s * tn, tn), tn), :]
        row = lax.broadcasted_iota(I32, (slab, tn), 0) + r * slab
        ws = [pl.multiple_of(ws_ref[(eg * group + k) * n_tiles + t], head) for k in range(group)]
        hits = [row == pos_ref[k, pl.ds(s, 1), :] - ws[k] for k in range(group)]
        rows = _dot(jnp.concatenate([_ones_where(hk) for hk in hits], axis=0), hs)
        for k in range(group):
            rk = rows[k * slab:(k + 1) * slab]
            gk = jnp.sum(jnp.where(hits[k], aff_ref[k, pl.ds(s, 1), :], 0.0), axis=1, keepdims=True)
            base = pl.multiple_of(ws[k] + r * slab, head)
            acc_rows = head if r == 0 else slab
            xe_ref[k, pl.ds(base, acc_rows), :] = (
                xe_ref[k, pl.ds(base, acc_rows), :].astype(F32) + rk[:acc_rows]).astype(xe_ref.dtype)
            gs_ref[k, pl.ds(base, acc_rows), :] = gs_ref[k, pl.ds(base, acc_rows), :] + gk[:acc_rows]
            if r == 0:
                rest = pl.multiple_of(base + head, head)
                xe_ref[k, pl.ds(rest, slab - head), :] = rk[head:].astype(xe_ref.dtype)
                gs_ref[k, pl.ds(rest, slab - head), :] = gk[head:]

    def tile_batch(sb, carry):
        for u in range(batch):
            one_round(sb * batch + u, 0)
        for u in range(batch):
            s = sb * batch + u
            need = need_ref[eg * n_tiles + tb * sub + s]
            for r in range(1, n_rounds):
                pl.when(need > r * slab)(functools.partial(one_round, s, r))
        return carry

    lax.fori_loop(0, sub // batch, tile_batch, 0)


def _dispatch(wstart, need, pos_t, aff_t, h, cap, tn):
    E, n_tiles, _ = pos_t.shape
    N, D = h.shape
    rows = cap + _dispatch_rounds(tn) * _dispatch_slab(tn)
    sub = 8 if n_tiles % 8 == 0 else n_tiles
    group = DISPATCH_GROUP
    resident = dict(pipeline_mode=pl.Buffered(1))
    grid_spec = pltpu.PrefetchScalarGridSpec(
        num_scalar_prefetch=2,
        grid=(E // group, n_tiles // sub),
        in_specs=[
            pl.BlockSpec((group, sub, tn), lambda e, t, ws, nd: (e, t, 0)),
            pl.BlockSpec((group, sub, tn), lambda e, t, ws, nd: (e, t, 0)),
            pl.BlockSpec((sub * tn, D), lambda e, t, ws, nd: (t, 0)),
        ],
        out_specs=[
            pl.BlockSpec((group, rows, D), lambda e, t, ws, nd: (e, 0, 0), **resident),
            pl.BlockSpec((group, rows, 1), lambda e, t, ws, nd: (e, 0, 0), **resident),
        ],
    )
    return pl.pallas_call(
        functools.partial(_dispatch_kernel, n_tiles=n_tiles, sub=sub, batch=8 if sub % 8 == 0 else 1),
        grid_spec=grid_spec,
        out_shape=[jax.ShapeDtypeStruct((E, rows, D), BF16), jax.ShapeDtypeStruct((E, rows, 1), F32)],
        compiler_params=_cparams(("parallel", "arbitrary")),
        name="moe_dispatch",
    )(wstart, need, pos_t, aff_t, h)


def _ffn_kernel(x_ref, gs_ref, wg_ref, wu_ref, wd_ref, o_ref, acc, *, n_f):
    f = pl.program_id(2)

    @pl.when(f == 0)
    def _():
        acc[...] = jnp.zeros_like(acc)

    x = x_ref[...]
    hid = _silu(_dot(x, wg_ref[...].astype(BF16))) * _dot(x, wu_ref[...].astype(BF16))
    acc[...] += _dot(hid.astype(BF16), wd_ref[...].astype(BF16))

    @pl.when(f == n_f - 1)
    def _():
        o_ref[...] = (acc[...] * gs_ref[...]).astype(o_ref.dtype)


def _expert_ffn(xe, gs, wg, wu, wd, layer, C):
    E, _, D = xe.shape
    F = wg.shape[-1]
    tm = _tile(C, 2048)
    tf = _tile(F, 512)
    n_f = F // tf
    return pl.pallas_call(
        functools.partial(_ffn_kernel, n_f=n_f),
        grid=(E, C // tm, n_f),
        in_specs=[
            pl.BlockSpec((None, tm, D), lambda e, m, f: (e, m, 0)),
            pl.BlockSpec((None, tm, 1), lambda e, m, f: (e, m, 0)),
            pl.BlockSpec((None, None, D, tf), lambda e, m, f: (layer, e, 0, f)),
            pl.BlockSpec((None, None, D, tf), lambda e, m, f: (layer, e, 0, f)),
            pl.BlockSpec((None, None, tf, D), lambda e, m, f: (layer, e, f, 0)),
        ],
        out_specs=pl.BlockSpec((None, tm, D), lambda e, m, f: (e, m, 0)),
        out_shape=jax.ShapeDtypeStruct((E, C, D), BF16),
        scratch_shapes=[pltpu.VMEM((tm, D), F32)],
        compiler_params=_cparams(("parallel", "parallel", "arbitrary")),
        name="expert_ffn",
    )(xe, gs, wg, wu, wd)


COMBINE_SLAB = LANES
COMBINE_PARTS = 4
COMBINE_TILES = 2


def _combine_kernel(ws_ref, end_ref, pos_ref, ye_any, x_ref, g_ref, o_ref, *scratch,
                    n_tiles, n_steps, n_exp, tn, win, slab):
    hit_scrs = scratch[:-3]
    y_buf, late_scr, sem = scratch[-3:]
    per = n_exp // len(hit_scrs)
    tps = pos_ref.shape[0] // tn
    step = pl.program_id(0)
    slot = step % 2

    def window_copy(tile, buf, u, e):
        ws = pl.multiple_of(ws_ref[e * n_tiles + tile], BF16_SUBLANES)
        return pltpu.make_async_copy(ye_any.at[e, pl.ds(ws, slab), :],
                                     y_buf.at[buf, pl.ds((u * n_exp + e) * slab, slab), :], sem.at[buf])

    def fetch(s, buf):
        for u in range(tps):
            for e in range(n_exp):
                window_copy(s * tps + u, buf, u, e).start()

    @pl.when(step == 0)
    def _():
        fetch(0, 0)

    @pl.when(step + 1 < n_steps)
    def _():
        fetch(step + 1, 1 - slot)

    pltpu.make_async_copy(y_buf.at[1 - slot], y_buf.at[slot], sem.at[slot]).wait()

    lane = lax.broadcasted_iota(I32, (tn, slab), 1)
    for u in range(tps):
        t = step * tps + u
        rows = pl.ds(u * tn, tn)
        acc = None
        for gi, hit_scr in enumerate(hit_scrs):
            for k in range(per):
                e = gi * per + k
                prel = pos_ref[rows, e:e + 1] - ws_ref[e * n_tiles + t]
                hit_scr[:, k * slab:(k + 1) * slab] = _ones_where(lane == prel)
            base = (u * n_exp + gi * per) * slab
            part = _dot(hit_scr[...], y_buf[slot, base:base + per * slab, :])
            acc = part if acc is None else acc + part
        o_ref[rows, :] = x_ref[rows, :] + g_ref[...] * acc

        if win > slab:
            for e in range(n_exp):
                ws = ws_ref[e * n_tiles + t]

                @pl.when(end_ref[e * n_tiles + t] - ws > slab)
                def _():
                    late = pl.multiple_of(ws + slab, BF16_SUBLANES)
                    pltpu.sync_copy(ye_any.at[e, pl.ds(late, win - slab), :], late_scr)
                    prel = pos_ref[rows, e:e + 1] - ws - slab
                    hit = _ones_where(lax.broadcasted_iota(I32, (tn, win - slab), 1) == prel)
                    o_ref[rows, :] += g_ref[...] * _dot(hit, late_scr[...])


def _combine(wstart, wend, pos, ye, x, gate, tn):
    E, C, D = ye.shape
    B, T, _ = x.shape
    N = B * T
    n_tiles = N // tn
    win = tn + BF16_SUBLANES
    slab = min(COMBINE_SLAB, win)
    tps = COMBINE_TILES if (T // tn) % COMBINE_TILES == 0 else 1
    n_steps = n_tiles // tps
    spb = T // (tn * tps)
    grid_spec = pltpu.PrefetchScalarGridSpec(
        num_scalar_prefetch=2,
        grid=(n_steps,),
        in_specs=[pl.BlockSpec((tps * tn, E), lambda s, ws, we: (s, 0)),
                  pl.BlockSpec(memory_space=pl.ANY),
                  pl.BlockSpec((None, tps * tn, D), lambda s, ws, we: (s // spb, s % spb, 0)),
                  pl.BlockSpec((None, 1, D), lambda s, ws, we: (s // spb, 0, 0))],
        out_specs=pl.BlockSpec((None, tps * tn, D), lambda s, ws, we: (s // spb, s % spb, 0)),
        scratch_shapes=[pltpu.VMEM((tn, E // COMBINE_PARTS * slab), BF16)] * COMBINE_PARTS
        + [pltpu.VMEM((2, tps * E * slab, D), BF16),
           pltpu.VMEM((max(win - slab, BF16_SUBLANES), D), BF16),
           pltpu.SemaphoreType.DMA((2,))],
    )
    return pl.pallas_call(
        functools.partial(_combine_kernel, n_tiles=n_tiles, n_steps=n_steps, n_exp=E, tn=tn, win=win, slab=slab),
        grid_spec=grid_spec,
        out_shape=jax.ShapeDtypeStruct((B, T, D), F32),
        compiler_params=_cparams(("arbitrary",)),
        name="moe_combine",
    )(wstart, wend, pos, ye, x, gate)


def _moe_tile(n_tok, cap):
    tn = 256
    while tn + BF16_SUBLANES > cap or n_tok % tn:
        tn //= 2
    return tn


def _moe_layer(x, h, aff, gate, wg, wu, wd, layer):
    B, T, D = x.shape
    N = B * T
    E = aff.shape[1]
    cap = EC_CAPACITY_FACTOR * N // E
    tn = _moe_tile(T, cap)
    win = tn + BF16_SUBLANES
    pos, pre = _select(aff, cap)
    start = pre[::tn].T
    aligned = (start // BF16_SUBLANES * BF16_SUBLANES).astype(I32)
    wdisp = aligned.reshape(-1)
    wstart = jnp.minimum(aligned, cap - win).reshape(-1)
    end = jnp.concatenate([start[:, 1:], jnp.full((E, 1), cap, I32)], axis=1).astype(I32)
    wend = end.reshape(-1)
    need = (end - aligned).reshape(E // DISPATCH_GROUP, DISPATCH_GROUP, -1).max(axis=1).reshape(-1)
    pos_t = pos.T.reshape(E, N // tn, tn)
    aff_t = aff.T.reshape(E, N // tn, tn)
    xe, gs = _dispatch(wdisp, need, pos_t, aff_t, h, cap, tn)
    ye = _expert_ffn(xe, gs, wg, wu, wd, layer, cap)
    return _combine(wstart, wend, pos, ye, x, gate, tn)


def _trunk(x, mod, p):
    B, T, D = x.shape
    rc = min(RET_BLOCK, T)
    cos, sin = _rotary_tables(T)
    for i in range(DEPTH):
        sh1, sc1, g1, sh2, sc2, g2 = [mod[i, :, k][:, None, :] for k in range(6)]
        j = i // 2
        if i % 2 == 0:
            proj = _proj(x, p["norm_mix_g"][i][None], sh1, sc1, p["ret_w_in"][j], rotary=(cos, sin))
            tabs = _retention_tables(p["ret_decay_logit"][j], rc)
            y_f = _retention_dir(proj, tabs, 0, None)
            y = _retention_dir(proj, tabs, 1, y_f)
            w_out = p["ret_w_out"][j]
        else:
            qkv = _proj(x, p["norm_mix_g"][i][None], sh1, sc1, p["na_w_in"][j],
                        qk=(p["na_bd"], p["na_gain"][j], 2 * D))
            y = _na_attention(qkv, p["na_bias"][j])
            w_out = p["na_w_out"][j]
        x, h, aff = _out_proj(y, w_out, x, g1, p["norm_ffn_g"][i][None], sh2, sc2, p["moe_router"][i])
        x = _moe_layer(x, h, aff, g2, p["moe_w_gate"], p["moe_w_up"], p["moe_w_down"], i)
    return x


def kernel(x_prompt, x_sample, c_prompt, c_sample, norm_mix_g, norm_ffn_g, ada_w, ada_b, ret_w_in, ret_decay_logit, ret_w_out, na_w_in, na_q_gain, na_k_gain, na_rpb, na_w_out, moe_router, moe_w_gate, moe_w_up, moe_w_down):
    D = D_MODEL
    bp, bs = c_prompt.shape[0], c_sample.shape[0]
    c_all = jnp.concatenate([c_prompt, c_sample], axis=0)
    pad = (-c_all.shape[0]) % 8
    if pad:
        c_all = jnp.pad(c_all, ((0, pad), (0, 0)))
    mod = _ada_mod(c_all, ada_w, ada_b).reshape(DEPTH, c_all.shape[0], 6, D)

    heads_row = lambda v: jnp.tile(v.astype(F32), (1, NA_HEADS))
    na_gain = jnp.concatenate(
        [heads_row(na_q_gain) * (NA_HEAD_DIM ** -0.5), heads_row(na_k_gain),
         jnp.ones((na_q_gain.shape[0], D), F32)], axis=-1)[:, None, :]
    bd = np.kron(np.eye(MXU_DIM // NA_HEAD_DIM), np.ones((NA_HEAD_DIM, NA_HEAD_DIM)))
    p = dict(
        norm_mix_g=norm_mix_g, norm_ffn_g=norm_ffn_g,
        ret_w_in=ret_w_in.astype(BF16), ret_decay_logit=ret_decay_logit, ret_w_out=ret_w_out.astype(BF16),
        na_w_in=na_w_in.astype(BF16), na_w_out=na_w_out.astype(BF16),
        na_gain=na_gain, na_bd=jnp.asarray(bd, BF16),
        na_bias=jnp.stack([_na_bias_table(na_rpb[l]) for l in range(na_rpb.shape[0])]),
        moe_router=moe_router,
        moe_w_gate=moe_w_gate, moe_w_up=moe_w_up, moe_w_down=moe_w_down,
    )
    y_prompt = _trunk(x_prompt, mod[:, :bp], p)
    y_sample = _trunk(x_sample, mod[:, bp:bp + bs], p)
    return (y_prompt, y_sample)
```

```python
import functools
import math

import numpy as np
import jax
import jax.numpy as jnp
from jax import lax
from jax.experimental import pallas as pl
from jax.experimental.pallas import tpu as pltpu

F32 = jnp.float32
BF16 = jnp.bfloat16
I32 = jnp.int32

D_MODEL = 1024
DEPTH = 4

RET_HEADS = 4
RET_QK_DIM = 256
RET_V_DIM = 512
RET_QK_WIDTH = RET_HEADS * RET_QK_DIM
RET_V_WIDTH = RET_HEADS * RET_V_DIM
RET_IN_COLS = 2 * RET_QK_WIDTH + 3 * RET_V_WIDTH
ROPE_BASE = 10000.0
RET_BLOCK = 256

NA_HEADS = 16
NA_HEAD_DIM = 64
NA_KH = 8
NA_KW = 16
GRID_W = 64
NA_ROWS_PER_GROUP = 4
NA_WIN_ROWS = NA_ROWS_PER_GROUP + NA_KH - 1
NA_Q = NA_ROWS_PER_GROUP * GRID_W
NA_K = NA_WIN_ROWS * GRID_W

N_EXPERTS = 16
EC_CAPACITY_FACTOR = 2

EPS = 1e-6
NEG_INF = -1e30

LANES = 128
SUBLANES = 8
BF16_SUBLANES = 16
MXU_DIM = 256
VMEM_LIMIT = 56 * 1024 * 1024

HIGHEST = lax.Precision.HIGHEST


def _cparams(sem):
    return pltpu.CompilerParams(dimension_semantics=sem, vmem_limit_bytes=VMEM_LIMIT)


def _tile(n, pref):
    t = min(pref, n)
    while n % t:
        t //= 2
    return t


def _silu(x):
    return (0.5 * x) * (1.0 + jnp.tanh(0.5 * x))


def _ones_where(mask):
    return jnp.where(mask, 1.0, 0.0).astype(BF16)


def _dot(a, b):
    return jnp.dot(a, b, preferred_element_type=F32)


def _dot_nt(a, b):
    return lax.dot_general(a, b, (((1,), (1,)), ((), ())), preferred_element_type=F32)


def _dot_tn(a, b):
    return lax.dot_general(a, b, (((0,), (0,)), ((), ())), preferred_element_type=F32)


def _ada_kernel(c_ref, w_ref, b_ref, o_ref):
    ca = _silu(c_ref[...])
    o_ref[...] = jnp.dot(ca, w_ref[...], preferred_element_type=F32, precision=HIGHEST) + b_ref[...]


def _ada_mod(c_all, ada_w, ada_b):
    R, D = c_all.shape
    n_out = ada_w.shape[-1]
    tn = min(1536, n_out)
    return pl.pallas_call(
        _ada_kernel,
        grid=(DEPTH, n_out // tn),
        in_specs=[
            pl.BlockSpec((R, D), lambda l, j: (0, 0)),
            pl.BlockSpec((None, D, tn), lambda l, j: (l, 0, j)),
            pl.BlockSpec((None, 1, tn), lambda l, j: (l, 0, j)),
        ],
        out_specs=pl.BlockSpec((None, R, tn), lambda l, j: (l, 0, j)),
        out_shape=jax.ShapeDtypeStruct((DEPTH, R, n_out), F32),
        compiler_params=_cparams(("arbitrary", "arbitrary")),
        name="ada_mod",
    )(c_all, ada_w, ada_b.reshape(DEPTH, 1, n_out))


def _norm_mod(x, g, sh, sc):
    ms = jnp.mean(x * x, axis=-1, keepdims=True)
    y = x * lax.rsqrt(ms + EPS) * g
    return y * (1.0 + sc) + sh


def _proj_block(x_ref, g_ref, sh_ref, sc_ref, w_ref, h_scr):
    @pl.when(pl.program_id(2) == 0)
    def _():
        h_scr[...] = _norm_mod(x_ref[...], g_ref[...], sh_ref[...], sc_ref[...]).astype(BF16)

    return _dot(h_scr[...], w_ref[...])


def _proj_qknorm_kernel(x_ref, g_ref, sh_ref, sc_ref, w_ref, bd_ref, gain_ref, o_ref, h_scr, *, qk_blocks, tn):
    j = pl.program_id(2)
    acc = _proj_block(x_ref, g_ref, sh_ref, sc_ref, w_ref, h_scr)

    @pl.when(j < qk_blocks)
    def _():
        for c in range(tn // MXU_DIM):
            cols = slice(c * MXU_DIM, (c + 1) * MXU_DIM)
            a = acc[:, cols]
            ss = _dot((a * a).astype(BF16), bd_ref[...])
            r = lax.rsqrt(ss * (1.0 / NA_HEAD_DIM) + EPS)
            o_ref[:, cols] = (a * r * gain_ref[:, cols]).astype(o_ref.dtype)

    @pl.when(j >= qk_blocks)
    def _():
        o_ref[...] = acc.astype(o_ref.dtype)


def _proj_rotary_kernel(x_ref, g_ref, sh_ref, sc_ref, w_ref, cos_ref, sin_ref, o_ref, h_scr, *, tn):
    j = pl.program_id(2)
    acc = _proj_block(x_ref, g_ref, sh_ref, sc_ref, w_ref, h_scr)
    qk_blocks = 2 * RET_QK_WIDTH // tn
    half = RET_QK_DIM // 2

    @pl.when(j < qk_blocks)
    def _():
        scale = jnp.where(j < qk_blocks // 2, 1.0, RET_QK_DIM ** -0.5)
        cos = cos_ref[...] * scale
        sin = sin_ref[...] * scale
        for c in range(tn // RET_QK_DIM):
            a1 = acc[:, c * RET_QK_DIM:c * RET_QK_DIM + half]
            a2 = acc[:, c * RET_QK_DIM + half:(c + 1) * RET_QK_DIM]
            o_ref[:, c * RET_QK_DIM:c * RET_QK_DIM + half] = (a1 * cos - a2 * sin).astype(o_ref.dtype)
            o_ref[:, c * RET_QK_DIM + half:(c + 1) * RET_QK_DIM] = (a1 * sin + a2 * cos).astype(o_ref.dtype)

    @pl.when(j >= qk_blocks)
    def _():
        o_ref[...] = acc.astype(o_ref.dtype)


def _proj(x, g, sh, sc, w, qk=None, rotary=None):
    assert (qk is None) != (rotary is None)
    B, T, D = x.shape
    n_out = w.shape[1]
    tm = _tile(T, 2048)
    tn = _tile(n_out, 1024)
    grid = (B, T // tm, n_out // tn)
    in_specs = [
        pl.BlockSpec((None, tm, D), lambda b, i, j: (b, i, 0)),
        pl.BlockSpec((1, D), lambda b, i, j: (0, 0)),
        pl.BlockSpec((None, 1, D), lambda b, i, j: (b, 0, 0)),
        pl.BlockSpec((None, 1, D), lambda b, i, j: (b, 0, 0)),
        pl.BlockSpec((D, tn), lambda b, i, j: (0, j)),
    ]
    args = [x, g, sh, sc, w]
    if rotary is not None:
        half = RET_QK_DIM // 2
        in_specs += [pl.BlockSpec((tm, half), lambda b, i, j: (i, 0))] * 2
        args += list(rotary)
        body = functools.partial(_proj_rotary_kernel, tn=tn)
        name = "proj_rotary"
    else:
        bd, gain, n_qk = qk
        in_specs += [
            pl.BlockSpec((MXU_DIM, MXU_DIM), lambda b, i, j: (0, 0)),
            pl.BlockSpec((1, tn), lambda b, i, j: (0, j)),
        ]
        args += [bd, gain]
        body = functools.partial(_proj_qknorm_kernel, qk_blocks=n_qk // tn, tn=tn)
        name = "proj_qknorm"
    return pl.pallas_call(
        body,
        grid=grid,
        in_specs=in_specs,
        out_specs=pl.BlockSpec((None, tm, tn), lambda b, i, j: (b, i, j)),
        out_shape=jax.ShapeDtypeStruct((B, T, n_out), BF16),
        scratch_shapes=[pltpu.VMEM((tm, D), BF16)],
        compiler_params=_cparams(("parallel", "parallel", "arbitrary")),
        name=name,
    )(*args)


def _outproj_kernel(y_ref, w_ref, x_ref, g_ref, o_ref):
    o_ref[...] = x_ref[...] + g_ref[...] * _dot(y_ref[...], w_ref[...])


def _ffn_pre_kernel(x_ref, g_ref, sh_ref, sc_ref, r_ref, h_ref, aff_ref):
    h = _norm_mod(x_ref[...], g_ref[...], sh_ref[...], sc_ref[...])
    h_hi = h.astype(BF16)
    h_ref[...] = h_hi
    h_lo = (h - h_hi.astype(F32)).astype(BF16)
    r = r_ref[...]
    r_hi = r.astype(BF16)
    r_lo = (r - r_hi.astype(F32)).astype(BF16)
    logits = _dot(h_hi, r_hi) + (_dot(h_hi, r_lo) + _dot(h_lo, r_hi))
    m = jnp.max(logits, axis=-1, keepdims=True)
    e = jnp.exp(logits - m)
    aff_ref[...] = e / jnp.sum(e, axis=-1, keepdims=True)


def _out_proj(y, w, x, gate, ng, sh, sc, router):
    B, T, K = y.shape
    D = w.shape[1]
    E = router.shape[1]
    tm = _tile(T, 1024)
    xn = pl.pallas_call(
        _outproj_kernel,
        grid=(B, T // tm),
        in_specs=[
            pl.BlockSpec((None, tm, K), lambda b, i: (b, i, 0)),
            pl.BlockSpec((K, D), lambda b, i: (0, 0)),
            pl.BlockSpec((None, tm, D), lambda b, i: (b, i, 0)),
            pl.BlockSpec((None, 1, D), lambda b, i: (b, 0, 0)),
        ],
        out_specs=pl.BlockSpec((None, tm, D), lambda b, i: (b, i, 0)),
        out_shape=jax.ShapeDtypeStruct((B, T, D), F32),
        compiler_params=_cparams(("parallel", "parallel")),
        name="out_proj",
    )(y, w, x, gate)
    tp = _tile(T, 1024)
    nt = T // tp
    h, aff = pl.pallas_call(
        _ffn_pre_kernel,
        grid=(B, nt),
        in_specs=[
            pl.BlockSpec((None, tp, D), lambda b, i: (b, i, 0)),
            pl.BlockSpec((1, D), lambda b, i: (0, 0)),
            pl.BlockSpec((None, 1, D), lambda b, i: (b, 0, 0)),
            pl.BlockSpec((None, 1, D), lambda b, i: (b, 0, 0)),
            pl.BlockSpec((D, E), lambda b, i: (0, 0)),
        ],
        out_specs=[
            pl.BlockSpec((tp, D), lambda b, i: (b * nt + i, 0)),
            pl.BlockSpec((tp, E), lambda b, i: (b * nt + i, 0)),
        ],
        out_shape=[jax.ShapeDtypeStruct((B * T, D), BF16), jax.ShapeDtypeStruct((B * T, E), F32)],
        compiler_params=_cparams(("parallel", "parallel")),
        name="ffn_pre",
    )(xn, ng, sh, sc, router)
    return xn, h, aff


def _retention_kernel(*refs, has_prev):
    if has_prev:
        q_ref, k_ref, v_ref, gate_ref, intra_ref, qdec_ref, kdec_ref, cdec_ref, prev_ref, o_ref, state = refs
    else:
        q_ref, k_ref, v_ref, gate_ref, intra_ref, qdec_ref, kdec_ref, cdec_ref, o_ref, state = refs
        prev_ref = None

    @pl.when(pl.program_id(1) == 0)
    def _():
        state[...] = jnp.zeros_like(state)

    for h in range(RET_HEADS):
        qk = slice(h * RET_QK_DIM, (h + 1) * RET_QK_DIM)
        vs = slice(h * RET_V_DIM, (h + 1) * RET_V_DIM)
        qb = q_ref[:, qk]
        kb = k_ref[:, qk]
        v = v_ref[:, vs]
        scores = _dot_nt(qb, kb) * intra_ref[h]
        st = state[h]
        o = _dot(scores.astype(BF16), v) + _dot(qb, st.astype(BF16)) * qdec_ref[h]
        state[h] = st * cdec_ref[h] + _dot_tn((kb.astype(F32) * kdec_ref[h]).astype(BF16), v)

        mu = jnp.mean(o, axis=-1, keepdims=True)
        oc = o - mu
        var = jnp.mean(oc * oc, axis=-1, keepdims=True)
        y = oc * lax.rsqrt(var + EPS) * _silu(gate_ref[:, vs].astype(F32))
        if has_prev:
            y = y + prev_ref[:, vs].astype(F32)
        o_ref[:, vs] = y.astype(o_ref.dtype)


def _retention_dir(proj, tabs, direction, prev):
    B, T, _ = proj.shape
    rc = min(RET_BLOCK, T)
    nc = T // rc
    intra, qdec, kdec, cdec = tabs
    if direction == 0:
        cidx = lambda c: c
    else:
        cidx = lambda c: nc - 1 - c
    H = RET_HEADS
    v_blk = 2 * RET_QK_WIDTH // RET_V_WIDTH
    in_specs = [
        pl.BlockSpec((None, rc, RET_QK_WIDTH), lambda b, c: (b, cidx(c), 0)),
        pl.BlockSpec((None, rc, RET_QK_WIDTH), lambda b, c: (b, cidx(c), 1)),
        pl.BlockSpec((None, rc, RET_V_WIDTH), lambda b, c: (b, cidx(c), v_blk)),
        pl.BlockSpec((None, rc, RET_V_WIDTH), lambda b, c: (b, cidx(c), v_blk + 1 + direction)),
        pl.BlockSpec((H, rc, rc), lambda b, c: (direction, 0, 0)),
        pl.BlockSpec((H, rc, 1), lambda b, c: (direction, 0, 0)),
        pl.BlockSpec((H, rc, 1), lambda b, c: (direction, 0, 0)),
        pl.BlockSpec((H, 1, 1), lambda b, c: (direction, 0, 0)),
    ]
    args = [proj, proj, proj, proj, intra, qdec, kdec, cdec]
    if prev is not None:
        in_specs.append(pl.BlockSpec((None, rc, RET_V_WIDTH), lambda b, c: (b, cidx(c), 0)))
        args.append(prev)
    return pl.pallas_call(
        functools.partial(_retention_kernel, has_prev=prev is not None),
        grid=(B, nc),
        in_specs=in_specs,
        out_specs=pl.BlockSpec((None, rc, RET_V_WIDTH), lambda b, c: (b, cidx(c), 0)),
        out_shape=jax.ShapeDtypeStruct((B, T, RET_V_WIDTH), BF16),
        scratch_shapes=[pltpu.VMEM((H, RET_QK_DIM, RET_V_DIM), F32)],
        compiler_params=_cparams(("parallel", "arbitrary")),
        name="retention_bwd" if direction else "retention_fwd",
    )(*args)


def _retention_tables(decay_logit, rc):
    lg = jax.nn.log_sigmoid(decay_logit.astype(F32).reshape(-1))
    pos = jnp.arange(rc, dtype=F32)
    rel = pos[:, None] - pos[None, :]
    l3 = lg[:, None, None]
    fwd = jnp.where(rel >= 0, jnp.exp(jnp.maximum(rel, 0.0) * l3), 0.0)
    bwd = jnp.where(rel <= 0, jnp.exp(jnp.maximum(-rel, 0.0) * l3), 0.0)
    is_bwd = (jnp.arange(2 * RET_HEADS) >= RET_HEADS)
    intra = jnp.where(is_bwd[:, None, None], bwd, fwd)
    q_pow = jnp.where(is_bwd[:, None], rc - pos[None, :], pos[None, :] + 1.0)
    k_pow = jnp.where(is_bwd[:, None], pos[None, :], rc - 1.0 - pos[None, :])
    qdec = jnp.exp(q_pow * lg[:, None])[..., None]
    kdec = jnp.exp(k_pow * lg[:, None])[..., None]
    cdec = jnp.exp(rc * lg)[:, None, None]
    return intra, qdec, kdec, cdec


def _rotary_tables(T):
    d = RET_QK_DIM
    inv = 1.0 / (ROPE_BASE ** (jnp.arange(0, d, 2, dtype=F32) / d))
    ang = jnp.arange(T, dtype=F32)[:, None] * inv[None, :]
    return jnp.cos(ang), jnp.sin(ang)


def _na_kernel(q_ref, k_ref, v_ref, bias_ref, o_ref, *, rows):
    n_groups = rows // NA_ROWS_PER_GROUP
    lane = lax.broadcasted_iota(I32, (NA_Q, LANES), 1)
    first = lane < NA_HEAD_DIM

    def group(gi, carry):
        ustart = jnp.clip(gi * NA_ROWS_PER_GROUP - NA_KH // 2, 0, rows - NA_WIN_ROWS)
        pat = jnp.where(gi == 0, 0, jnp.where(gi == n_groups - 1, 2, 1))
        kstart = pl.multiple_of(ustart * GRID_W, GRID_W)
        qstart = pl.multiple_of(gi * NA_Q, NA_Q)
        kw = k_ref[pl.ds(kstart, NA_K), :]
        vw = v_ref[pl.ds(kstart, NA_K), :]
        q = q_ref[pl.ds(qstart, NA_Q), :].astype(F32)
        q2 = jnp.concatenate([jnp.where(first, q, 0.0), jnp.where(first, 0.0, q)], axis=0).astype(BF16)
        s = _dot_nt(q2, kw) + bias_ref[pat].reshape(2 * NA_Q, NA_K)
        m = jnp.max(s, axis=-1, keepdims=True)
        e = jnp.exp(s - m)
        l = jnp.sum(e, axis=-1, keepdims=True)
        o2 = _dot(e.astype(BF16), vw) / l
        o_ref[pl.ds(qstart, NA_Q), :] = jnp.where(first, o2[:NA_Q], o2[NA_Q:]).astype(o_ref.dtype)
        return carry

    lax.fori_loop(0, n_groups, group, 0, unroll=8)


def _na_attention(qkv, bias):
    B, T, _ = qkv.shape
    rows = T // GRID_W
    assert rows % NA_ROWS_PER_GROUP == 0 and rows >= NA_WIN_ROWS + 1
    n_pairs = D_MODEL // LANES
    blk = lambda off: pl.BlockSpec((None, T, LANES), lambda p, b: (b, 0, off + p))
    return pl.pallas_call(
        functools.partial(_na_kernel, rows=rows),
        grid=(n_pairs, B),
        in_specs=[
            blk(0), blk(n_pairs), blk(2 * n_pairs),
            pl.BlockSpec((3, 2, NA_Q, NA_K), lambda p, b: (0, p, 0, 0)),
        ],
        out_specs=pl.BlockSpec((None, T, LANES), lambda p, b: (b, 0, p)),
        out_shape=jax.ShapeDtypeStruct((B, T, D_MODEL), BF16),
        compiler_params=_cparams(("parallel", "parallel")),
        name="na_attention",
    )(qkv, qkv, qkv, bias)


def _na_bias_table(rpb):
    H = rpb.shape[0]
    r = rpb.astype(F32)
    pad = GRID_W
    rp = jnp.concatenate([jnp.repeat(r[..., :1], pad, -1), r, jnp.repeat(r[..., -1:], pad, -1)], -1)
    base = NA_KW - 1 + pad
    t1 = jnp.stack([rp[..., base - qc: base - qc + GRID_W] for qc in range(GRID_W)], axis=2)
    qc = np.arange(GRID_W)[:, None]
    kc = np.arange(GRID_W)[None, :]
    ws = np.clip(qc - NA_KW // 2, 0, GRID_W - NA_KW)
    t1 = jnp.where(jnp.asarray((kc >= ws) & (kc < ws + NA_KW)), t1, NEG_INF)
    n_dr = 2 * NA_KH - 1
    t1 = jnp.concatenate([t1, jnp.full((H, 1, GRID_W, GRID_W), NEG_INF, F32)], axis=1)
    pick = np.full((3, NA_ROWS_PER_GROUP, NA_WIN_ROWS), n_dr, np.int32)
    for p in range(3):
        for i in range(NA_ROWS_PER_GROUP):
            lo = (0, i, NA_ROWS_PER_GROUP - 1)[p]
            qrow = (i, i + NA_KH // 2, i + NA_KH - 1)[p]
            for u in range(lo, lo + NA_KH):
                pick[p, i, u] = u - qrow + NA_KH - 1
    table = jnp.take(t1, jnp.asarray(pick.reshape(-1)), axis=1)
    table = table.reshape(H, 3, NA_ROWS_PER_GROUP, NA_WIN_ROWS, GRID_W, GRID_W)
    return table.transpose(1, 0, 2, 4, 3, 5).reshape(3, H, NA_Q, NA_K)


def _select_kernel(aff_ref, pos_ref, pre_ref, *, cap, n_tok, rb):
    E = N_EXPERTS
    per_row = LANES // E
    rows = n_tok // per_row
    bits = pltpu.bitcast(aff_ref[...], I32)
    lane = lax.broadcasted_iota(I32, (rows, LANES), 1)
    tok = lax.broadcasted_iota(I32, (rows, LANES), 0) * per_row + lane // E

    def count(pred):
        c = jnp.sum(pred.astype(I32), axis=0, keepdims=True)
        c = jnp.broadcast_to(c, (SUBLANES, LANES))
        sh = E
        while sh < LANES:
            c = c + pltpu.roll(c, sh, 1)
            sh *= 2
        return c[0:1]

    def thr_step(i, thr):
        cand = thr | jnp.left_shift(jnp.int32(1), 30 - i)
        return jnp.where(count(bits >= cand) >= cap, cand, thr)

    thr = lax.fori_loop(0, 31, thr_step, jnp.zeros((1, LANES), I32))
    gt = bits > thr
    eq = bits == thr
    need = cap - count(gt)

    nbits = max(1, int(math.ceil(math.log2(n_tok))))

    def tie_step(i, jmax):
        cand = jmax + jnp.left_shift(jnp.int32(1), nbits - 1 - i)
        return jnp.where(count(jnp.logical_and(eq, tok < cand)) < need, cand, jmax)

    jmax = lax.fori_loop(0, nbits, tie_step, jnp.zeros((1, LANES), I32))

    li = lax.broadcasted_iota(I32, (LANES, LANES), 0)
    lj = lax.broadcasted_iota(I32, (LANES, LANES), 1)
    same_e = (li % E) == (lj % E)
    a_before = _ones_where(jnp.logical_and(same_e, li // E < lj // E))
    a_all = _ones_where(same_e)
    ri = lax.broadcasted_iota(I32, (rb, rb), 0)
    rj = lax.broadcasted_iota(I32, (rb, rb), 1)
    lower = _ones_where(rj < ri)

    running = jnp.zeros((1, LANES), F32)
    for blk in range(rows // rb):
        sl = slice(blk * rb, (blk + 1) * rb)
        b_blk = pltpu.bitcast(aff_ref[sl, :], I32)
        t_blk = ((lax.broadcasted_iota(I32, (rb, LANES), 0) + blk * rb) * per_row
                 + lax.broadcasted_iota(I32, (rb, LANES), 1) // E)
        keep = jnp.logical_or(b_blk > thr, jnp.logical_and(b_blk == thr, t_blk <= jmax))
        kb = _ones_where(keep)
        row_tot = _dot(kb, a_all)
        within = _dot(lower, row_tot.astype(BF16))
        pre = running + within + _dot(kb, a_before)
        running = running + jnp.sum(row_tot, axis=0, keepdims=True)
        pre_i = pre.astype(I32)
        pre_ref[sl, :] = pre_i
        pos_ref[sl, :] = jnp.where(keep, pre_i, -1)


def _select(aff, cap):
    n_tok, E = aff.shape
    rows = n_tok * E // LANES
    rb = min(256, rows)
    aff8 = aff.reshape(rows, LANES)
    pos, pre = pl.pallas_call(
        functools.partial(_select_kernel, cap=cap, n_tok=n_tok, rb=rb),
        out_shape=[jax.ShapeDtypeStruct((rows, LANES), I32)] * 2,
        compiler_params=pltpu.CompilerParams(vmem_limit_bytes=VMEM_LIMIT),
        name="expert_select",
    )(aff8)
    return pos.reshape(n_tok, E), pre.reshape(n_tok, E)


DISPATCH_GROUP = 4


def _dispatch_slab(tn):
    return tn // 4 + BF16_SUBLANES


def _dispatch_rounds(tn):
    return -(-(tn + BF16_SUBLANES) // _dispatch_slab(tn))


def _dispatch_kernel(ws_ref, need_ref, pos_ref, aff_ref, h_ref, xe_ref, gs_ref, *, n_tiles, sub, batch):
    eg = pl.program_id(0)
    tb = pl.program_id(1)
    group, _, tn = pos_ref.shape
    head = BF16_SUBLANES
    slab = _dispatch_slab(tn)
    n_rounds = _dispatch_rounds(tn)

    @pl.when(tb == 0)
    def _():
        xe_ref[...] = jnp.zeros_like(xe_ref)
        gs_ref[...] = jnp.zeros_like(gs_ref)

    def one_round(s, r):
        t = tb * sub + s
        hs = h_ref[pl.ds(pl.multiple_of(s * tn, tn), tn), :]
        row = lax.broadcasted_iota(I32, (slab, tn), 0) + r * slab
        ws = [pl.multiple_of(ws_ref[(eg * group + k) * n_tiles + t], head) for k in range(group)]
        hits = [row == pos_ref[k, pl.ds(s, 1), :] - ws[k] for k in range(group)]
        rows = _dot(jnp.concatenate([_ones_where(hk) for hk in hits], axis=0), hs)
        for k in range(group):
            rk = rows[k * slab:(k + 1) * slab]
            gk = jnp.sum(jnp.where(hits[k], aff_ref[k, pl.ds(s, 1), :], 0.0), axis=1, keepdims=True)
            base = pl.multiple_of(ws[k] + r * slab, head)
            acc_rows = head if r == 0 else slab
            xe_ref[k, pl.ds(base, acc_rows), :] = (
                xe_ref[k, pl.ds(base, acc_rows), :].astype(F32) + rk[:acc_rows]).astype(xe_ref.dtype)
            gs_ref[k, pl.ds(base, acc_rows), :] = gs_ref[k, pl.ds(base, acc_rows), :] + gk[:acc_rows]
            if r == 0:
                rest = pl.multiple_of(base + head, head)
                xe_ref[k, pl.ds(rest, slab - head), :] = rk[head:].astype(xe_ref.dtype)
                gs_ref[k, pl.ds(rest, slab - head), :] = gk[head:]

    def tile_batch(sb, carry):
        for u in range(batch):
            one_round(sb * batch + u, 0)
        for u in range(batch):
            s = sb * batch + u
            need = need_ref[eg * n_tiles + tb * sub + s]
            for r in range(1, n_rounds):
                pl.when(need > r * slab)(functools.partial(one_round, s, r))
        return carry

    lax.fori_loop(0, sub // batch, tile_batch, 0)


def _dispatch(wstart, need, pos_t, aff_t, h, cap, tn):
    E, n_tiles, _ = pos_t.shape
    N, D = h.shape
    rows = cap + _dispatch_rounds(tn) * _dispatch_slab(tn)
    sub = 8 if n_tiles % 8 == 0 else n_tiles
    group = DISPATCH_GROUP
    resident = dict(pipeline_mode=pl.Buffered(1))
    grid_spec = pltpu.PrefetchScalarGridSpec(
        num_scalar_prefetch=2,
        grid=(E // group, n_tiles // sub),
        in_specs=[
            pl.BlockSpec((group, sub, tn), lambda e, t, ws, nd: (e, t, 0)),
            pl.BlockSpec((group, sub, tn), lambda e, t, ws, nd: (e, t, 0)),
            pl.BlockSpec((sub * tn, D), lambda e, t, ws, nd: (t, 0)),
        ],
        out_specs=[
            pl.BlockSpec((group, rows, D), lambda e, t, ws, nd: (e, 0, 0), **resident),
            pl.BlockSpec((group, rows, 1), lambda e, t, ws, nd: (e, 0, 0), **resident),
        ],
    )
    return pl.pallas_call(
        functools.partial(_dispatch_kernel, n_tiles=n_tiles, sub=sub, batch=8 if sub % 8 == 0 else 1),
        grid_spec=grid_spec,
        out_shape=[jax.ShapeDtypeStruct((E, rows, D), BF16), jax.ShapeDtypeStruct((E, rows, 1), F32)],
        compiler_params=_cparams(("parallel", "arbitrary")),
        name="moe_dispatch",
    )(wstart, need, pos_t, aff_t, h)


def _ffn_kernel(x_ref, gs_ref, wg_ref, wu_ref, wd_ref, o_ref, acc, *, n_f):
    f = pl.program_id(2)

    @pl.when(f == 0)
    def _():
        acc[...] = jnp.zeros_like(acc)

    x = x_ref[...]
    hid = _silu(_dot(x, wg_ref[...].astype(BF16))) * _dot(x, wu_ref[...].astype(BF16))
    acc[...] += _dot(hid.astype(BF16), wd_ref[...].astype(BF16))

    @pl.when(f == n_f - 1)
    def _():
        o_ref[...] = (acc[...] * gs_ref[...]).astype(o_ref.dtype)


def _expert_ffn(xe, gs, wg, wu, wd, layer, C):
    E, _, D = xe.shape
    F = wg.shape[-1]
    tm = _tile(C, 2048)
    tf = _tile(F, 512)
    n_f = F // tf
    return pl.pallas_call(
        functools.partial(_ffn_kernel, n_f=n_f),
        grid=(E, C // tm, n_f),
        in_specs=[
            pl.BlockSpec((None, tm, D), lambda e, m, f: (e, m, 0)),
            pl.BlockSpec((None, tm, 1), lambda e, m, f: (e, m, 0)),
            pl.BlockSpec((None, None, D, tf), lambda e, m, f: (layer, e, 0, f)),
            pl.BlockSpec((None, None, D, tf), lambda e, m, f: (layer, e, 0, f)),
            pl.BlockSpec((None, None, tf, D), lambda e, m, f: (layer, e, f, 0)),
        ],
        out_specs=pl.BlockSpec((None, tm, D), lambda e, m, f: (e, m, 0)),
        out_shape=jax.ShapeDtypeStruct((E, C, D), BF16),
        scratch_shapes=[pltpu.VMEM((tm, D), F32)],
        compiler_params=_cparams(("parallel", "parallel", "arbitrary")),
        name="expert_ffn",
    )(xe, gs, wg, wu, wd)


COMBINE_SLAB = LANES
COMBINE_PARTS = 4
COMBINE_TILES = 2


def _combine_kernel(ws_ref, end_ref, pos_ref, ye_any, x_ref, g_ref, o_ref, *scratch,
                    n_tiles, n_steps, n_exp, tn, win, slab):
    hit_scrs = scratch[:-3]
    y_buf, late_scr, sem = scratch[-3:]
    per = n_exp // len(hit_scrs)
    tps = pos_ref.shape[0] // tn
    step = pl.program_id(0)
    slot = step % 2

    def window_copy(tile, buf, u, e):
        ws = pl.multiple_of(ws_ref[e * n_tiles + tile], BF16_SUBLANES)
        return pltpu.make_async_copy(ye_any.at[e, pl.ds(ws, slab), :],
                                     y_buf.at[buf, pl.ds((u * n_exp + e) * slab, slab), :], sem.at[buf])

    def fetch(s, buf):
        for u in range(tps):
            for e in range(n_exp):
                window_copy(s * tps + u, buf, u, e).start()

    @pl.when(step == 0)
    def _():
        fetch(0, 0)

    @pl.when(step + 1 < n_steps)
    def _():
        fetch(step + 1, 1 - slot)

    pltpu.make_async_copy(y_buf.at[1 - slot], y_buf.at[slot], sem.at[slot]).wait()

    width = per * slab
    spread = _ones_where(lax.broadcasted_iota(I32, (n_exp, n_exp * slab), 1) // slab
                         == lax.broadcasted_iota(I32, (n_exp, n_exp * slab), 0))
    col1 = (lax.broadcasted_iota(I32, (tn, width), 1) % slab + 1).astype(F32)
    elane = lax.broadcasted_iota(I32, (1, n_exp), 1)
    for u in range(tps):
        t = step * tps + u
        rows = pl.ds(u * tn, tn)
        ws_row = jnp.zeros((1, n_exp), I32)
        for e in range(n_exp):
            ws_row = jnp.where(elane == e, ws_ref[e * n_tiles + t], ws_row)
        q = jnp.clip(pos_ref[rows, :] - ws_row + 1, 0, 2 * slab - 1).astype(F32).astype(BF16)
        acc = None
        for gi, hit_scr in enumerate(hit_scrs):
            hit_scr[...] = _ones_where(_dot(q, spread[:, gi * width:(gi + 1) * width]) == col1)
            base = (u * n_exp + gi * per) * slab
            part = _dot(hit_scr[...], y_buf[slot, base:base + per * slab, :])
            acc = part if acc is None else acc + part
        o_ref[rows, :] = x_ref[rows, :] + g_ref[...] * acc

        if win > slab:
            for e in range(n_exp):
                ws = ws_ref[e * n_tiles + t]

                @pl.when(end_ref[e * n_tiles + t] - ws > slab)
                def _():
                    late = pl.multiple_of(ws + slab, BF16_SUBLANES)
                    pltpu.sync_copy(ye_any.at[e, pl.ds(late, win - slab), :], late_scr)
                    prel = pos_ref[rows, e:e + 1] - ws - slab
                    hit = _ones_where(lax.broadcasted_iota(I32, (tn, win - slab), 1) == prel)
                    o_ref[rows, :] += g_ref[...] * _dot(hit, late_scr[...])


def _combine(wstart, wend, pos, ye, x, gate, tn):
    E, C, D = ye.shape
    B, T, _ = x.shape
    N = B * T
    n_tiles = N // tn
    win = tn + BF16_SUBLANES
    slab = min(COMBINE_SLAB, win)
    tps = COMBINE_TILES if (T // tn) % COMBINE_TILES == 0 else 1
    n_steps = n_tiles // tps
    spb = T // (tn * tps)
    grid_spec = pltpu.PrefetchScalarGridSpec(
        num_scalar_prefetch=2,
        grid=(n_steps,),
        in_specs=[pl.BlockSpec((tps * tn, E), lambda s, ws, we: (s, 0)),
                  pl.BlockSpec(memory_space=pl.ANY),
                  pl.BlockSpec((None, tps * tn, D), lambda s, ws, we: (s // spb, s % spb, 0)),
                  pl.BlockSpec((None, 1, D), lambda s, ws, we: (s // spb, 0, 0))],
        out_specs=pl.BlockSpec((None, tps * tn, D), lambda s, ws, we: (s // spb, s % spb, 0)),
        scratch_shapes=[pltpu.VMEM((tn, E // COMBINE_PARTS * slab), BF16)] * COMBINE_PARTS
        + [pltpu.VMEM((2, tps * E * slab, D), BF16),
           pltpu.VMEM((max(win - slab, BF16_SUBLANES), D), BF16),
           pltpu.SemaphoreType.DMA((2,))],
    )
    return pl.pallas_call(
        functools.partial(_combine_kernel, n_tiles=n_tiles, n_steps=n_steps, n_exp=E, tn=tn, win=win, slab=slab),
        grid_spec=grid_spec,
        out_shape=jax.ShapeDtypeStruct((B, T, D), F32),
        compiler_params=_cparams(("arbitrary",)),
        name="moe_combine",
    )(wstart, wend, pos, ye, x, gate)


def _moe_tile(n_tok, cap):
    tn = 256
    while tn + BF16_SUBLANES > cap or n_tok % tn:
        tn //= 2
    return tn


def _moe_layer(x, h, aff, gate, wg, wu, wd, layer):
    B, T, D = x.shape
    N = B * T
    E = aff.shape[1]
    cap = EC_CAPACITY_FACTOR * N // E
    tn = _moe_tile(T, cap)
    win = tn + BF16_SUBLANES
    pos, pre = _select(aff, cap)
    start = pre[::tn].T
    aligned = (start // BF16_SUBLANES * BF16_SUBLANES).astype(I32)
    wdisp = aligned.reshape(-1)
    wstart = jnp.minimum(aligned, cap - win).reshape(-1)
    end = jnp.concatenate([start[:, 1:], jnp.full((E, 1), cap, I32)], axis=1).astype(I32)
    wend = end.reshape(-1)
    need = (end - aligned).reshape(E // DISPATCH_GROUP, DISPATCH_GROUP, -1).max(axis=1).reshape(-1)
    pos_t = pos.T.reshape(E, N // tn, tn)
    aff_t = aff.T.reshape(E, N // tn, tn)
    xe, gs = _dispatch(wdisp, need, pos_t, aff_t, h, cap, tn)
    ye = _expert_ffn(xe, gs, wg, wu, wd, layer, cap)
    return _combine(wstart, wend, pos, ye, x, gate, tn)


def _trunk(x, mod, p):
    B, T, D = x.shape
    rc = min(RET_BLOCK, T)
    cos, sin = _rotary_tables(T)
    for i in range(DEPTH):
        sh1, sc1, g1, sh2, sc2, g2 = [mod[i, :, k][:, None, :] for k in range(6)]
        j = i // 2
        if i % 2 == 0:
            proj = _proj(x, p["norm_mix_g"][i][None], sh1, sc1, p["ret_w_in"][j], rotary=(cos, sin))
            tabs = _retention_tables(p["ret_decay_logit"][j], rc)
            y_f = _retention_dir(proj, tabs, 0, None)
            y = _retention_dir(proj, tabs, 1, y_f)
            w_out = p["ret_w_out"][j]
        else:
            qkv = _proj(x, p["norm_mix_g"][i][None], sh1, sc1, p["na_w_in"][j],
                        qk=(p["na_bd"], p["na_gain"][j], 2 * D))
            y = _na_attention(qkv, p["na_bias"][j])
            w_out = p["na_w_out"][j]
        x, h, aff = _out_proj(y, w_out, x, g1, p["norm_ffn_g"][i][None], sh2, sc2, p["moe_router"][i])
        x = _moe_layer(x, h, aff, g2, p["moe_w_gate"], p["moe_w_up"], p["moe_w_down"], i)
    return x


def kernel(x_prompt, x_sample, c_prompt, c_sample, norm_mix_g, norm_ffn_g, ada_w, ada_b, ret_w_in, ret_decay_logit, ret_w_out, na_w_in, na_q_gain, na_k_gain, na_rpb, na_w_out, moe_router, moe_w_gate, moe_w_up, moe_w_down):
    D = D_MODEL
    bp, bs = c_prompt.shape[0], c_sample.shape[0]
    c_all = jnp.concatenate([c_prompt, c_sample], axis=0)
    pad = (-c_all.shape[0]) % 8
    if pad:
        c_all = jnp.pad(c_all, ((0, pad), (0, 0)))
    mod = _ada_mod(c_all, ada_w, ada_b).reshape(DEPTH, c_all.shape[0], 6, D)

    heads_row = lambda v: jnp.tile(v.astype(F32), (1, NA_HEADS))
    na_gain = jnp.concatenate(
        [heads_row(na_q_gain) * (NA_HEAD_DIM ** -0.5), heads_row(na_k_gain),
         jnp.ones((na_q_gain.shape[0], D), F32)], axis=-1)[:, None, :]
    bd = np.kron(np.eye(MXU_DIM // NA_HEAD_DIM), np.ones((NA_HEAD_DIM, NA_HEAD_DIM)))
    p = dict(
        norm_mix_g=norm_mix_g, norm_ffn_g=norm_ffn_g,
        ret_w_in=ret_w_in.astype(BF16), ret_decay_logit=ret_decay_logit, ret_w_out=ret_w_out.astype(BF16),
        na_w_in=na_w_in.astype(BF16), na_w_out=na_w_out.astype(BF16),
        na_gain=na_gain, na_bd=jnp.asarray(bd, BF16),
        na_bias=jnp.stack([_na_bias_table(na_rpb[l]) for l in range(na_rpb.shape[0])]),
        moe_router=moe_router,
        moe_w_gate=moe_w_gate, moe_w_up=moe_w_up, moe_w_down=moe_w_down,
    )
    y_prompt = _trunk(x_prompt, mod[:, :bp], p)
    y_sample = _trunk(x_sample, mod[:, bp:bp + bs], p)
    return (y_prompt, y_sample)
```

```python
import functools
import math

import numpy as np
import jax
import jax.numpy as jnp
from jax import lax
from jax.experimental import pallas as pl
from jax.experimental.pallas import tpu as pltpu

F32 = jnp.float32
BF16 = jnp.bfloat16
I32 = jnp.int32

D_MODEL = 1024
DEPTH = 4

RET_HEADS = 4
RET_QK_DIM = 256
RET_V_DIM = 512
RET_QK_WIDTH = RET_HEADS * RET_QK_DIM
RET_V_WIDTH = RET_HEADS * RET_V_DIM
RET_IN_COLS = 2 * RET_QK_WIDTH + 3 * RET_V_WIDTH
ROPE_BASE = 10000.0
RET_BLOCK = 256

NA_HEADS = 16
NA_HEAD_DIM = 64
NA_KH = 8
NA_KW = 16
GRID_W = 64
NA_ROWS_PER_GROUP = 4
NA_WIN_ROWS = NA_ROWS_PER_GROUP + NA_KH - 1
NA_Q = NA_ROWS_PER_GROUP * GRID_W
NA_K = NA_WIN_ROWS * GRID_W

N_EXPERTS = 16
EC_CAPACITY_FACTOR = 2

EPS = 1e-6
NEG_INF = -1e30

LANES = 128
SUBLANES = 8
BF16_SUBLANES = 16
MXU_DIM = 256
VMEM_LIMIT = 56 * 1024 * 1024

HIGHEST = lax.Precision.HIGHEST


def _cparams(sem):
    return pltpu.CompilerParams(dimension_semantics=sem, vmem_limit_bytes=VMEM_LIMIT)


def _tile(n, pref):
    t = min(pref, n)
    while n % t:
        t //= 2
    return t


def _silu(x):
    return (0.5 * x) * (1.0 + jnp.tanh(0.5 * x))


def _ones_where(mask):
    return jnp.where(mask, 1.0, 0.0).astype(BF16)


def _dot(a, b):
    return jnp.dot(a, b, preferred_element_type=F32)


def _dot_nt(a, b):
    return lax.dot_general(a, b, (((1,), (1,)), ((), ())), preferred_element_type=F32)


def _dot_tn(a, b):
    return lax.dot_general(a, b, (((0,), (0,)), ((), ())), preferred_element_type=F32)


def _ada_kernel(c_ref, w_ref, b_ref, o_ref):
    ca = _silu(c_ref[...])
    o_ref[...] = jnp.dot(ca, w_ref[...], preferred_element_type=F32, precision=HIGHEST) + b_ref[...]


def _ada_mod(c_all, ada_w, ada_b):
    R, D = c_all.shape
    n_out = ada_w.shape[-1]
    tn = min(1536, n_out)
    return pl.pallas_call(
        _ada_kernel,
        grid=(DEPTH, n_out // tn),
        in_specs=[
            pl.BlockSpec((R, D), lambda l, j: (0, 0)),
            pl.BlockSpec((None, D, tn), lambda l, j: (l, 0, j)),
            pl.BlockSpec((None, 1, tn), lambda l, j: (l, 0, j)),
        ],
        out_specs=pl.BlockSpec((None, R, tn), lambda l, j: (l, 0, j)),
        out_shape=jax.ShapeDtypeStruct((DEPTH, R, n_out), F32),
        compiler_params=_cparams(("arbitrary", "arbitrary")),
        name="ada_mod",
    )(c_all, ada_w, ada_b.reshape(DEPTH, 1, n_out))


def _norm_mod(x, g, sh, sc):
    ms = jnp.mean(x * x, axis=-1, keepdims=True)
    y = x * lax.rsqrt(ms + EPS) * g
    return y * (1.0 + sc) + sh


def _proj_block(x_ref, g_ref, sh_ref, sc_ref, w_ref, h_scr):
    @pl.when(pl.program_id(2) == 0)
    def _():
        h_scr[...] = _norm_mod(x_ref[...], g_ref[...], sh_ref[...], sc_ref[...]).astype(BF16)

    return _dot(h_scr[...], w_ref[...])


def _proj_qknorm_kernel(x_ref, g_ref, sh_ref, sc_ref, w_ref, bd_ref, gain_ref, o_ref, h_scr, *, qk_blocks, tn):
    j = pl.program_id(2)
    acc = _proj_block(x_ref, g_ref, sh_ref, sc_ref, w_ref, h_scr)

    @pl.when(j < qk_blocks)
    def _():
        for c in range(tn // MXU_DIM):
            cols = slice(c * MXU_DIM, (c + 1) * MXU_DIM)
            a = acc[:, cols]
            ss = _dot((a * a).astype(BF16), bd_ref[...])
            r = lax.rsqrt(ss * (1.0 / NA_HEAD_DIM) + EPS)
            o_ref[:, cols] = (a * r * gain_ref[:, cols]).astype(o_ref.dtype)

    @pl.when(j >= qk_blocks)
    def _():
        o_ref[...] = acc.astype(o_ref.dtype)


def _proj_rotary_kernel(x_ref, g_ref, sh_ref, sc_ref, w_ref, cos_ref, sin_ref, o_ref, h_scr, *, tn):
    j = pl.program_id(2)
    acc = _proj_block(x_ref, g_ref, sh_ref, sc_ref, w_ref, h_scr)
    qk_blocks = 2 * RET_QK_WIDTH // tn
    half = RET_QK_DIM // 2

    @pl.when(j < qk_blocks)
    def _():
        scale = jnp.where(j < qk_blocks // 2, 1.0, RET_QK_DIM ** -0.5)
        cos = cos_ref[...] * scale
        sin = sin_ref[...] * scale
        for c in range(tn // RET_QK_DIM):
            a1 = acc[:, c * RET_QK_DIM:c * RET_QK_DIM + half]
            a2 = acc[:, c * RET_QK_DIM + half:(c + 1) * RET_QK_DIM]
            o_ref[:, c * RET_QK_DIM:c * RET_QK_DIM + half] = (a1 * cos - a2 * sin).astype(o_ref.dtype)
            o_ref[:, c * RET_QK_DIM + half:(c + 1) * RET_QK_DIM] = (a1 * sin + a2 * cos).astype(o_ref.dtype)

    @pl.when(j >= qk_blocks)
    def _():
        o_ref[...] = acc.astype(o_ref.dtype)


def _proj(x, g, sh, sc, w, qk=None, rotary=None):
    assert (qk is None) != (rotary is None)
    B, T, D = x.shape
    n_out = w.shape[1]
    tm = _tile(T, 2048)
    tn = _tile(n_out, 1024)
    grid = (B, T // tm, n_out // tn)
    in_specs = [
        pl.BlockSpec((None, tm, D), lambda b, i, j: (b, i, 0)),
        pl.BlockSpec((1, D), lambda b, i, j: (0, 0)),
        pl.BlockSpec((None, 1, D), lambda b, i, j: (b, 0, 0)),
        pl.BlockSpec((None, 1, D), lambda b, i, j: (b, 0, 0)),
        pl.BlockSpec((D, tn), lambda b, i, j: (0, j)),
    ]
    args = [x, g, sh, sc, w]
    if rotary is not None:
        half = RET_QK_DIM // 2
        in_specs += [pl.BlockSpec((tm, half), lambda b, i, j: (i, 0))] * 2
        args += list(rotary)
        body = functools.partial(_proj_rotary_kernel, tn=tn)
        name = "proj_rotary"
    else:
        bd, gain, n_qk = qk
        in_specs += [
            pl.BlockSpec((MXU_DIM, MXU_DIM), lambda b, i, j: (0, 0)),
            pl.BlockSpec((1, tn), lambda b, i, j: (0, j)),
        ]
        args += [bd, gain]
        body = functools.partial(_proj_qknorm_kernel, qk_blocks=n_qk // tn, tn=tn)
        name = "proj_qknorm"
    return pl.pallas_call(
        body,
        grid=grid,
        in_specs=in_specs,
        out_specs=pl.BlockSpec((None, tm, tn), lambda b, i, j: (b, i, j)),
        out_shape=jax.ShapeDtypeStruct((B, T, n_out), BF16),
        scratch_shapes=[pltpu.VMEM((tm, D), BF16)],
        compiler_params=_cparams(("parallel", "parallel", "arbitrary")),
        name=name,
    )(*args)


def _outproj_kernel(y_ref, w_ref, x_ref, g_ref, o_ref):
    o_ref[...] = x_ref[...] + g_ref[...] * _dot(y_ref[...], w_ref[...])


def _ffn_pre_kernel(x_ref, g_ref, sh_ref, sc_ref, r_ref, h_ref, aff_ref):
    h = _norm_mod(x_ref[...], g_ref[...], sh_ref[...], sc_ref[...])
    h_hi = h.astype(BF16)
    h_ref[...] = h_hi
    h_lo = (h - h_hi.astype(F32)).astype(BF16)
    r = r_ref[...]
    r_hi = r.astype(BF16)
    r_lo = (r - r_hi.astype(F32)).astype(BF16)
    logits = _dot(h_hi, r_hi) + (_dot(h_hi, r_lo) + _dot(h_lo, r_hi))
    m = jnp.max(logits, axis=-1, keepdims=True)
    e = jnp.exp(logits - m)
    aff_ref[...] = e / jnp.sum(e, axis=-1, keepdims=True)


def _out_proj(y, w, x, gate, ng, sh, sc, router):
    B, T, K = y.shape
    D = w.shape[1]
    E = router.shape[1]
    tm = _tile(T, 1024)
    xn = pl.pallas_call(
        _outproj_kernel,
        grid=(B, T // tm),
        in_specs=[
            pl.BlockSpec((None, tm, K), lambda b, i: (b, i, 0)),
            pl.BlockSpec((K, D), lambda b, i: (0, 0)),
            pl.BlockSpec((None, tm, D), lambda b, i: (b, i, 0)),
            pl.BlockSpec((None, 1, D), lambda b, i: (b, 0, 0)),
        ],
        out_specs=pl.BlockSpec((None, tm, D), lambda b, i: (b, i, 0)),
        out_shape=jax.ShapeDtypeStruct((B, T, D), F32),
        compiler_params=_cparams(("parallel", "parallel")),
        name="out_proj",
    )(y, w, x, gate)
    tp = _tile(T, 1024)
    nt = T // tp
    h, aff = pl.pallas_call(
        _ffn_pre_kernel,
        grid=(B, nt),
        in_specs=[
            pl.BlockSpec((None, tp, D), lambda b, i: (b, i, 0)),
            pl.BlockSpec((1, D), lambda b, i: (0, 0)),
            pl.BlockSpec((None, 1, D), lambda b, i: (b, 0, 0)),
            pl.BlockSpec((None, 1, D), lambda b, i: (b, 0, 0)),
            pl.BlockSpec((D, E), lambda b, i: (0, 0)),
        ],
        out_specs=[
            pl.BlockSpec((tp, D), lambda b, i: (b * nt + i, 0)),
            pl.BlockSpec((tp, E), lambda b, i: (b * nt + i, 0)),
        ],
        out_shape=[jax.ShapeDtypeStruct((B * T, D), BF16), jax.ShapeDtypeStruct((B * T, E), F32)],
        compiler_params=_cparams(("parallel", "parallel")),
        name="ffn_pre",
    )(xn, ng, sh, sc, router)
    return xn, h, aff


def _retention_kernel(*refs, has_prev):
    if has_prev:
        q_ref, k_ref, v_ref, gate_ref, intra_ref, qdec_ref, kdec_ref, cdec_ref, prev_ref, o_ref, state = refs
    else:
        q_ref, k_ref, v_ref, gate_ref, intra_ref, qdec_ref, kdec_ref, cdec_ref, o_ref, state = refs
        prev_ref = None

    @pl.when(pl.program_id(1) == 0)
    def _():
        state[...] = jnp.zeros_like(state)

    for h in range(RET_HEADS):
        qk = slice(h * RET_QK_DIM, (h + 1) * RET_QK_DIM)
        vs = slice(h * RET_V_DIM, (h + 1) * RET_V_DIM)
        qb = q_ref[:, qk]
        kb = k_ref[:, qk]
        v = v_ref[:, vs]
        scores = _dot_nt(qb, kb) * intra_ref[h]
        st = state[h]
        o = _dot(scores.astype(BF16), v) + _dot(qb, st.astype(BF16)) * qdec_ref[h]
        state[h] = st * cdec_ref[h] + _dot_tn((kb.astype(F32) * kdec_ref[h]).astype(BF16), v)

        mu = jnp.mean(o, axis=-1, keepdims=True)
        oc = o - mu
        var = jnp.mean(oc * oc, axis=-1, keepdims=True)
        y = oc * lax.rsqrt(var + EPS) * _silu(gate_ref[:, vs].astype(F32))
        if has_prev:
            y = y + prev_ref[:, vs].astype(F32)
        o_ref[:, vs] = y.astype(o_ref.dtype)


def _retention_dir(proj, tabs, direction, prev):
    B, T, _ = proj.shape
    rc = min(RET_BLOCK, T)
    nc = T // rc
    intra, qdec, kdec, cdec = tabs
    if direction == 0:
        cidx = lambda c: c
    else:
        cidx = lambda c: nc - 1 - c
    H = RET_HEADS
    v_blk = 2 * RET_QK_WIDTH // RET_V_WIDTH
    in_specs = [
        pl.BlockSpec((None, rc, RET_QK_WIDTH), lambda b, c: (b, cidx(c), 0)),
        pl.BlockSpec((None, rc, RET_QK_WIDTH), lambda b, c: (b, cidx(c), 1)),
        pl.BlockSpec((None, rc, RET_V_WIDTH), lambda b, c: (b, cidx(c), v_blk)),
        pl.BlockSpec((None, rc, RET_V_WIDTH), lambda b, c: (b, cidx(c), v_blk + 1 + direction)),
        pl.BlockSpec((H, rc, rc), lambda b, c: (direction, 0, 0)),
        pl.BlockSpec((H, rc, 1), lambda b, c: (direction, 0, 0)),
        pl.BlockSpec((H, rc, 1), lambda b, c: (direction, 0, 0)),
        pl.BlockSpec((H, 1, 1), lambda b, c: (direction, 0, 0)),
    ]
    args = [proj, proj, proj, proj, intra, qdec, kdec, cdec]
    if prev is not None:
        in_specs.append(pl.BlockSpec((None, rc, RET_V_WIDTH), lambda b, c: (b, cidx(c), 0)))
        args.append(prev)
    return pl.pallas_call(
        functools.partial(_retention_kernel, has_prev=prev is not None),
        grid=(B, nc),
        in_specs=in_specs,
        out_specs=pl.BlockSpec((None, rc, RET_V_WIDTH), lambda b, c: (b, cidx(c), 0)),
        out_shape=jax.ShapeDtypeStruct((B, T, RET_V_WIDTH), BF16),
        scratch_shapes=[pltpu.VMEM((H, RET_QK_DIM, RET_V_DIM), F32)],
        compiler_params=_cparams(("parallel", "arbitrary")),
        name="retention_bwd" if direction else "retention_fwd",
    )(*args)


def _retention_tables(decay_logit, rc):
    lg = jax.nn.log_sigmoid(decay_logit.astype(F32).reshape(-1))
    pos = jnp.arange(rc, dtype=F32)
    rel = pos[:, None] - pos[None, :]
    l3 = lg[:, None, None]
    fwd = jnp.where(rel >= 0, jnp.exp(jnp.maximum(rel, 0.0) * l3), 0.0)
    bwd = jnp.where(rel <= 0, jnp.exp(jnp.maximum(-rel, 0.0) * l3), 0.0)
    is_bwd = (jnp.arange(2 * RET_HEADS) >= RET_HEADS)
    intra = jnp.where(is_bwd[:, None, None], bwd, fwd)
    q_pow = jnp.where(is_bwd[:, None], rc - pos[None, :], pos[None, :] + 1.0)
    k_pow = jnp.where(is_bwd[:, None], pos[None, :], rc - 1.0 - pos[None, :])
    qdec = jnp.exp(q_pow * lg[:, None])[..., None]
    kdec = jnp.exp(k_pow * lg[:, None])[..., None]
    cdec = jnp.exp(rc * lg)[:, None, None]
    return intra, qdec, kdec, cdec


def _rotary_tables(T):
    d = RET_QK_DIM
    inv = 1.0 / (ROPE_BASE ** (jnp.arange(0, d, 2, dtype=F32) / d))
    ang = jnp.arange(T, dtype=F32)[:, None] * inv[None, :]
    return jnp.cos(ang), jnp.sin(ang)


def _na_kernel(q_ref, k_ref, v_ref, bias_ref, o_ref, *, rows):
    n_groups = rows // NA_ROWS_PER_GROUP
    lane = lax.broadcasted_iota(I32, (NA_Q, LANES), 1)
    first = lane < NA_HEAD_DIM

    def group(gi, carry):
        ustart = jnp.clip(gi * NA_ROWS_PER_GROUP - NA_KH // 2, 0, rows - NA_WIN_ROWS)
        pat = jnp.where(gi == 0, 0, jnp.where(gi == n_groups - 1, 2, 1))
        kstart = pl.multiple_of(ustart * GRID_W, GRID_W)
        qstart = pl.multiple_of(gi * NA_Q, NA_Q)
        kw = k_ref[pl.ds(kstart, NA_K), :]
        vw = v_ref[pl.ds(kstart, NA_K), :]
        q = q_ref[pl.ds(qstart, NA_Q), :].astype(F32)
        q2 = jnp.concatenate([jnp.where(first, q, 0.0), jnp.where(first, 0.0, q)], axis=0).astype(BF16)
        s = _dot_nt(q2, kw) + bias_ref[pat].reshape(2 * NA_Q, NA_K)
        m = jnp.max(s, axis=-1, keepdims=True)
        e = jnp.exp(s - m)
        l = jnp.sum(e, axis=-1, keepdims=True)
        o2 = _dot(e.astype(BF16), vw) / l
        o_ref[pl.ds(qstart, NA_Q), :] = jnp.where(first, o2[:NA_Q], o2[NA_Q:]).astype(o_ref.dtype)
        return carry

    lax.fori_loop(0, n_groups, group, 0, unroll=16)


def _na_attention(qkv, bias):
    B, T, _ = qkv.shape
    rows = T // GRID_W
    assert rows % NA_ROWS_PER_GROUP == 0 and rows >= NA_WIN_ROWS + 1
    n_pairs = D_MODEL // LANES
    blk = lambda off: pl.BlockSpec((None, T, LANES), lambda p, b: (b, 0, off + p))
    return pl.pallas_call(
        functools.partial(_na_kernel, rows=rows),
        grid=(n_pairs, B),
        in_specs=[
            blk(0), blk(n_pairs), blk(2 * n_pairs),
            pl.BlockSpec((3, 2, NA_Q, NA_K), lambda p, b: (0, p, 0, 0)),
        ],
        out_specs=pl.BlockSpec((None, T, LANES), lambda p, b: (b, 0, p)),
        out_shape=jax.ShapeDtypeStruct((B, T, D_MODEL), BF16),
        compiler_params=_cparams(("parallel", "parallel")),
        name="na_attention",
    )(qkv, qkv, qkv, bias)


def _na_bias_table(rpb):
    H = rpb.shape[0]
    r = rpb.astype(F32)
    pad = GRID_W
    rp = jnp.concatenate([jnp.repeat(r[..., :1], pad, -1), r, jnp.repeat(r[..., -1:], pad, -1)], -1)
    base = NA_KW - 1 + pad
    t1 = jnp.stack([rp[..., base - qc: base - qc + GRID_W] for qc in range(GRID_W)], axis=2)
    qc = np.arange(GRID_W)[:, None]
    kc = np.arange(GRID_W)[None, :]
    ws = np.clip(qc - NA_KW // 2, 0, GRID_W - NA_KW)
    t1 = jnp.where(jnp.asarray((kc >= ws) & (kc < ws + NA_KW)), t1, NEG_INF)
    n_dr = 2 * NA_KH - 1
    t1 = jnp.concatenate([t1, jnp.full((H, 1, GRID_W, GRID_W), NEG_INF, F32)], axis=1)
    pick = np.full((3, NA_ROWS_PER_GROUP, NA_WIN_ROWS), n_dr, np.int32)
    for p in range(3):
        for i in range(NA_ROWS_PER_GROUP):
            lo = (0, i, NA_ROWS_PER_GROUP - 1)[p]
            qrow = (i, i + NA_KH // 2, i + NA_KH - 1)[p]
            for u in range(lo, lo + NA_KH):
                pick[p, i, u] = u - qrow + NA_KH - 1
    table = jnp.take(t1, jnp.asarray(pick.reshape(-1)), axis=1)
    table = table.reshape(H, 3, NA_ROWS_PER_GROUP, NA_WIN_ROWS, GRID_W, GRID_W)
    return table.transpose(1, 0, 2, 4, 3, 5).reshape(3, H, NA_Q, NA_K)


def _select_kernel(aff_ref, pos_ref, pre_ref, *, cap, n_tok, rb):
    E = N_EXPERTS
    per_row = LANES // E
    rows = n_tok // per_row
    bits = pltpu.bitcast(aff_ref[...], I32)
    lane = lax.broadcasted_iota(I32, (rows, LANES), 1)
    tok = lax.broadcasted_iota(I32, (rows, LANES), 0) * per_row + lane // E

    def count(pred):
        c = jnp.sum(pred.astype(I32), axis=0, keepdims=True)
        c = jnp.broadcast_to(c, (SUBLANES, LANES))
        sh = E
        while sh < LANES:
            c = c + pltpu.roll(c, sh, 1)
            sh *= 2
        return c[0:1]

    def thr_step(i, thr):
        cand = thr | jnp.left_shift(jnp.int32(1), 30 - i)
        return jnp.where(count(bits >= cand) >= cap, cand, thr)

    thr = lax.fori_loop(0, 31, thr_step, jnp.zeros((1, LANES), I32))
    gt = bits > thr
    eq = bits == thr
    need = cap - count(gt)

    nbits = max(1, int(math.ceil(math.log2(n_tok))))

    def tie_step(i, jmax):
        cand = jmax + jnp.left_shift(jnp.int32(1), nbits - 1 - i)
        return jnp.where(count(jnp.logical_and(eq, tok < cand)) < need, cand, jmax)

    jmax = lax.fori_loop(0, nbits, tie_step, jnp.zeros((1, LANES), I32))

    li = lax.broadcasted_iota(I32, (LANES, LANES), 0)
    lj = lax.broadcasted_iota(I32, (LANES, LANES), 1)
    same_e = (li % E) == (lj % E)
    a_before = _ones_where(jnp.logical_and(same_e, li // E < lj // E))
    a_all = _ones_where(same_e)
    ri = lax.broadcasted_iota(I32, (rb, rb), 0)
    rj = lax.broadcasted_iota(I32, (rb, rb), 1)
    lower = _ones_where(rj < ri)

    running = jnp.zeros((1, LANES), F32)
    for blk in range(rows // rb):
        sl = slice(blk * rb, (blk + 1) * rb)
        b_blk = pltpu.bitcast(aff_ref[sl, :], I32)
        t_blk = ((lax.broadcasted_iota(I32, (rb, LANES), 0) + blk * rb) * per_row
                 + lax.broadcasted_iota(I32, (rb, LANES), 1) // E)
        keep = jnp.logical_or(b_blk > thr, jnp.logical_and(b_blk == thr, t_blk <= jmax))
        kb = _ones_where(keep)
        row_tot = _dot(kb, a_all)
        within = _dot(lower, row_tot.astype(BF16))
        pre = running + within + _dot(kb, a_before)
        running = running + jnp.sum(row_tot, axis=0, keepdims=True)
        pre_i = pre.astype(I32)
        pre_ref[sl, :] = pre_i
        pos_ref[sl, :] = jnp.where(keep, pre_i, -1)


def _select(aff, cap):
    n_tok, E = aff.shape
    rows = n_tok * E // LANES
    rb = min(256, rows)
    aff8 = aff.reshape(rows, LANES)
    pos, pre = pl.pallas_call(
        functools.partial(_select_kernel, cap=cap, n_tok=n_tok, rb=rb),
        out_shape=[jax.ShapeDtypeStruct((rows, LANES), I32)] * 2,
        compiler_params=pltpu.CompilerParams(vmem_limit_bytes=VMEM_LIMIT),
        name="expert_select",
    )(aff8)
    return pos.reshape(n_tok, E), pre.reshape(n_tok, E)


DISPATCH_GROUP = 4


def _dispatch_slab(tn):
    return tn // 4 + BF16_SUBLANES


def _dispatch_rounds(tn):
    return -(-(tn + BF16_SUBLANES) // _dispatch_slab(tn))


def _dispatch_kernel(ws_ref, need_ref, pos_ref, aff_ref, h_ref, xe_ref, gs_ref, *, n_tiles, sub, batch):
    eg = pl.program_id(0)
    tb = pl.program_id(1)
    group, _, tn = pos_ref.shape
    head = BF16_SUBLANES
    slab = _dispatch_slab(tn)
    n_rounds = _dispatch_rounds(tn)

    @pl.when(tb == 0)
    def _():
        xe_ref[...] = jnp.zeros_like(xe_ref)
        gs_ref[...] = jnp.zeros_like(gs_ref)

    def one_round(s, r):
        t = tb * sub + s
        hs = h_ref[pl.ds(pl.multiple_of(s * tn, tn), tn), :]
        row = lax.broadcasted_iota(I32, (slab, tn), 0) + r * slab
        ws = [pl.multiple_of(ws_ref[(eg * group + k) * n_tiles + t], head) for k in range(group)]
        hits = [row == pos_ref[k, pl.ds(s, 1), :] - ws[k] for k in range(group)]
        rows = _dot(jnp.concatenate([_ones_where(hk) for hk in hits], axis=0), hs)
        for k in range(group):
            rk = rows[k * slab:(k + 1) * slab]
            gk = jnp.sum(jnp.where(hits[k], aff_ref[k, pl.ds(s, 1), :], 0.0), axis=1, keepdims=True)
            base = pl.multiple_of(ws[k] + r * slab, head)
            acc_rows = head if r == 0 else slab
            xe_ref[k, pl.ds(base, acc_rows), :] = (
                xe_ref[k, pl.ds(base, acc_rows), :].astype(F32) + rk[:acc_rows]).astype(xe_ref.dtype)
            gs_ref[k, pl.ds(base, acc_rows), :] = gs_ref[k, pl.ds(base, acc_rows), :] + gk[:acc_rows]
            if r == 0:
                rest = pl.multiple_of(base + head, head)
                xe_ref[k, pl.ds(rest, slab - head), :] = rk[head:].astype(xe_ref.dtype)
                gs_ref[k, pl.ds(rest, slab - head), :] = gk[head:]

    def tile_batch(sb, carry):
        for u in range(batch):
            one_round(sb * batch + u, 0)
        for u in range(batch):
            s = sb * batch + u
            need = need_ref[eg * n_tiles + tb * sub + s]
            for r in range(1, n_rounds):
                pl.when(need > r * slab)(functools.partial(one_round, s, r))
        return carry

    lax.fori_loop(0, sub // batch, tile_batch, 0)


def _dispatch(wstart, need, pos_t, aff_t, h, cap, tn):
    E, n_tiles, _ = pos_t.shape
    N, D = h.shape
    rows = cap + _dispatch_rounds(tn) * _dispatch_slab(tn)
    sub = 8 if n_tiles % 8 == 0 else n_tiles
    group = DISPATCH_GROUP
    resident = dict(pipeline_mode=pl.Buffered(1))
    grid_spec = pltpu.PrefetchScalarGridSpec(
        num_scalar_prefetch=2,
        grid=(E // group, n_tiles // sub),
        in_specs=[
            pl.BlockSpec((group, sub, tn), lambda e, t, ws, nd: (e, t, 0)),
            pl.BlockSpec((group, sub, tn), lambda e, t, ws, nd: (e, t, 0)),
            pl.BlockSpec((sub * tn, D), lambda e, t, ws, nd: (t, 0)),
        ],
        out_specs=[
            pl.BlockSpec((group, rows, D), lambda e, t, ws, nd: (e, 0, 0), **resident),
            pl.BlockSpec((group, rows, 1), lambda e, t, ws, nd: (e, 0, 0), **resident),
        ],
    )
    return pl.pallas_call(
        functools.partial(_dispatch_kernel, n_tiles=n_tiles, sub=sub, batch=8 if sub % 8 == 0 else 1),
        grid_spec=grid_spec,
        out_shape=[jax.ShapeDtypeStruct((E, rows, D), BF16), jax.ShapeDtypeStruct((E, rows, 1), F32)],
        compiler_params=_cparams(("parallel", "arbitrary")),
        name="moe_dispatch",
    )(wstart, need, pos_t, aff_t, h)


def _ffn_kernel(x_ref, gs_ref, wg_ref, wu_ref, wd_ref, o_ref, acc, *, n_f):
    f = pl.program_id(2)

    @pl.when(f == 0)
    def _():
        acc[...] = jnp.zeros_like(acc)

    x = x_ref[...]
    hid = _silu(_dot(x, wg_ref[...].astype(BF16))) * _dot(x, wu_ref[...].astype(BF16))
    acc[...] += _dot(hid.astype(BF16), wd_ref[...].astype(BF16))

    @pl.when(f == n_f - 1)
    def _():
        o_ref[...] = (acc[...] * gs_ref[...]).astype(o_ref.dtype)


def _expert_ffn(xe, gs, wg, wu, wd, layer, C):
    E, _, D = xe.shape
    F = wg.shape[-1]
    tm = _tile(C, 2048)
    tf = _tile(F, 512)
    n_f = F // tf
    return pl.pallas_call(
        functools.partial(_ffn_kernel, n_f=n_f),
        grid=(E, C // tm, n_f),
        in_specs=[
            pl.BlockSpec((None, tm, D), lambda e, m, f: (e, m, 0)),
            pl.BlockSpec((None, tm, 1), lambda e, m, f: (e, m, 0)),
            pl.BlockSpec((None, None, D, tf), lambda e, m, f: (layer, e, 0, f)),
            pl.BlockSpec((None, None, D, tf), lambda e, m, f: (layer, e, 0, f)),
            pl.BlockSpec((None, None, tf, D), lambda e, m, f: (layer, e, f, 0)),
        ],
        out_specs=pl.BlockSpec((None, tm, D), lambda e, m, f: (e, m, 0)),
        out_shape=jax.ShapeDtypeStruct((E, C, D), BF16),
        scratch_shapes=[pltpu.VMEM((tm, D), F32)],
        compiler_params=_cparams(("parallel", "parallel", "arbitrary")),
        name="expert_ffn",
    )(xe, gs, wg, wu, wd)


COMBINE_SLAB = LANES
COMBINE_PARTS = 4
COMBINE_TILES = 2


def _combine_kernel(ws_ref, end_ref, pos_ref, ye_any, x_ref, g_ref, o_ref, *scratch,
                    n_tiles, n_steps, n_exp, tn, win, slab):
    hit_scrs = scratch[:-3]
    y_buf, late_scr, sem = scratch[-3:]
    per = n_exp // len(hit_scrs)
    tps = pos_ref.shape[0] // tn
    step = pl.program_id(0)
    slot = step % 2

    def window_copy(tile, buf, u, e):
        ws = pl.multiple_of(ws_ref[e * n_tiles + tile], BF16_SUBLANES)
        return pltpu.make_async_copy(ye_any.at[e, pl.ds(ws, slab), :],
                                     y_buf.at[buf, pl.ds((u * n_exp + e) * slab, slab), :], sem.at[buf])

    def fetch(s, buf):
        for u in range(tps):
            for e in range(n_exp):
                window_copy(s * tps + u, buf, u, e).start()

    @pl.when(step == 0)
    def _():
        fetch(0, 0)

    @pl.when(step + 1 < n_steps)
    def _():
        fetch(step + 1, 1 - slot)

    pltpu.make_async_copy(y_buf.at[1 - slot], y_buf.at[slot], sem.at[slot]).wait()

    lane = lax.broadcasted_iota(I32, (tn, slab), 1)
    for u in range(tps):
        t = step * tps + u
        rows = pl.ds(u * tn, tn)
        acc = None
        for gi, hit_scr in enumerate(hit_scrs):
            for k in range(per):
                e = gi * per + k
                prel = pos_ref[rows, e:e + 1] - ws_ref[e * n_tiles + t]
                hit_scr[:, k * slab:(k + 1) * slab] = _ones_where(lane == prel)
            base = (u * n_exp + gi * per) * slab
            part = _dot(hit_scr[...], y_buf[slot, base:base + per * slab, :])
            acc = part if acc is None else acc + part
        o_ref[rows, :] = x_ref[rows, :] + g_ref[...] * acc

        if win > slab:
            for e in range(n_exp):
                ws = ws_ref[e * n_tiles + t]

                @pl.when(end_ref[e * n_tiles + t] - ws > slab)
                def _():
                    late = pl.multiple_of(ws + slab, BF16_SUBLANES)
                    pltpu.sync_copy(ye_any.at[e, pl.ds(late, win - slab), :], late_scr)
                    prel = pos_ref[rows, e:e + 1] - ws - slab
                    hit = _ones_where(lax.broadcasted_iota(I32, (tn, win - slab), 1) == prel)
                    o_ref[rows, :] += g_ref[...] * _dot(hit, late_scr[...])


def _combine(wstart, wend, pos, ye, x, gate, tn):
    E, C, D = ye.shape
    B, T, _ = x.shape
    N = B * T
    n_tiles = N // tn
    win = tn + BF16_SUBLANES
    slab = min(COMBINE_SLAB, win)
    tps = COMBINE_TILES if (T // tn) % COMBINE_TILES == 0 else 1
    n_steps = n_tiles // tps
    spb = T // (tn * tps)
    grid_spec = pltpu.PrefetchScalarGridSpec(
        num_scalar_prefetch=2,
        grid=(n_steps,),
        in_specs=[pl.BlockSpec((tps * tn, E), lambda s, ws, we: (s, 0)),
                  pl.BlockSpec(memory_space=pl.ANY),
                  pl.BlockSpec((None, tps * tn, D), lambda s, ws, we: (s // spb, s % spb, 0)),
                  pl.BlockSpec((None, 1, D), lambda s, ws, we: (s // spb, 0, 0))],
        out_specs=pl.BlockSpec((None, tps * tn, D), lambda s, ws, we: (s // spb, s % spb, 0)),
        scratch_shapes=[pltpu.VMEM((tn, E // COMBINE_PARTS * slab), BF16)] * COMBINE_PARTS
        + [pltpu.VMEM((2, tps * E * slab, D), BF16),
           pltpu.VMEM((max(win - slab, BF16_SUBLANES), D), BF16),
           pltpu.SemaphoreType.DMA((2,))],
    )
    return pl.pallas_call(
        functools.partial(_combine_kernel, n_tiles=n_tiles, n_steps=n_steps, n_exp=E, tn=tn, win=win, slab=slab),
        grid_spec=grid_spec,
        out_shape=jax.ShapeDtypeStruct((B, T, D), F32),
        compiler_params=_cparams(("arbitrary",)),
        name="moe_combine",
    )(wstart, wend, pos, ye, x, gate)


def _moe_tile(n_tok, cap):
    tn = 256
    while tn + BF16_SUBLANES > cap or n_tok % tn:
        tn //= 2
    return tn


def _moe_layer(x, h, aff, gate, wg, wu, wd, layer):
    B, T, D = x.shape
    N = B * T
    E = aff.shape[1]
    cap = EC_CAPACITY_FACTOR * N // E
    tn = _moe_tile(T, cap)
    win = tn + BF16_SUBLANES
    pos, pre = _select(aff, cap)
    start = pre[::tn].T
    aligned = (start // BF16_SUBLANES * BF16_SUBLANES).astype(I32)
    wdisp = aligned.reshape(-1)
    wstart = jnp.minimum(aligned, cap - win).reshape(-1)
    end = jnp.concatenate([start[:, 1:], jnp.full((E, 1), cap, I32)], axis=1).astype(I32)
    wend = end.reshape(-1)
    need = (end - aligned).reshape(E // DISPATCH_GROUP, DISPATCH_GROUP, -1).max(axis=1).reshape(-1)
    pos_t = pos.T.reshape(E, N // tn, tn)
    aff_t = aff.T.reshape(E, N // tn, tn)
    xe, gs = _dispatch(wdisp, need, pos_t, aff_t, h, cap, tn)
    ye = _expert_ffn(xe, gs, wg, wu, wd, layer, cap)
    return _combine(wstart, wend, pos, ye, x, gate, tn)


def _trunk(x, mod, p):
    B, T, D = x.shape
    rc = min(RET_BLOCK, T)
    cos, sin = _rotary_tables(T)
    for i in range(DEPTH):
        sh1, sc1, g1, sh2, sc2, g2 = [mod[i, :, k][:, None, :] for k in range(6)]
        j = i // 2
        if i % 2 == 0:
            proj = _proj(x, p["norm_mix_g"][i][None], sh1, sc1, p["ret_w_in"][j], rotary=(cos, sin))
            tabs = _retention_tables(p["ret_decay_logit"][j], rc)
            y_f = _retention_dir(proj, tabs, 0, None)
            y = _retention_dir(proj, tabs, 1, y_f)
            w_out = p["ret_w_out"][j]
        else:
            qkv = _proj(x, p["norm_mix_g"][i][None], sh1, sc1, p["na_w_in"][j],
                        qk=(p["na_bd"], p["na_gain"][j], 2 * D))
            y = _na_attention(qkv, p["na_bias"][j])
            w_out = p["na_w_out"][j]
        x, h, aff = _out_proj(y, w_out, x, g1, p["norm_ffn_g"][i][None], sh2, sc2, p["moe_router"][i])
        x = _moe_layer(x, h, aff, g2, p["moe_w_gate"], p["moe_w_up"], p["moe_w_down"], i)
    return x


def kernel(x_prompt, x_sample, c_prompt, c_sample, norm_mix_g, norm_ffn_g, ada_w, ada_b, ret_w_in, ret_decay_logit, ret_w_out, na_w_in, na_q_gain, na_k_gain, na_rpb, na_w_out, moe_router, moe_w_gate, moe_w_up, moe_w_down):
    D = D_MODEL
    bp, bs = c_prompt.shape[0], c_sample.shape[0]
    c_all = jnp.concatenate([c_prompt, c_sample], axis=0)
    pad = (-c_all.shape[0]) % 8
    if pad:
        c_all = jnp.pad(c_all, ((0, pad), (0, 0)))
    mod = _ada_mod(c_all, ada_w, ada_b).reshape(DEPTH, c_all.shape[0], 6, D)

    heads_row = lambda v: jnp.tile(v.astype(F32), (1, NA_HEADS))
    na_gain = jnp.concatenate(
        [heads_row(na_q_gain) * (NA_HEAD_DIM ** -0.5), heads_row(na_k_gain),
         jnp.ones((na_q_gain.shape[0], D), F32)], axis=-1)[:, None, :]
    bd = np.kron(np.eye(MXU_DIM // NA_HEAD_DIM), np.ones((NA_HEAD_DIM, NA_HEAD_DIM)))
    p = dict(
        norm_mix_g=norm_mix_g, norm_ffn_g=norm_ffn_g,
        ret_w_in=ret_w_in.astype(BF16), ret_decay_logit=ret_decay_logit, ret_w_out=ret_w_out.astype(BF16),
        na_w_in=na_w_in.astype(BF16), na_w_out=na_w_out.astype(BF16),
        na_gain=na_gain, na_bd=jnp.asarray(bd, BF16),
        na_bias=jnp.stack([_na_bias_table(na_rpb[l]) for l in range(na_rpb.shape[0])]),
        moe_router=moe_router,
        moe_w_gate=moe_w_gate, moe_w_up=moe_w_up, moe_w_down=moe_w_down,
    )
    y_prompt = _trunk(x_prompt, mod[:, :bp], p)
    y_sample = _trunk(x_sample, mod[:, bp:bp + bs], p)
    return (y_prompt, y_sample)
```
